```python
import math
import jax, jax.numpy as jnp
from jax import lax
import numpy as np

D_MODEL = 1024
BATCH = 4
SEQ = 4096
DEPTH = 1

HEAD_DIM = 64
DIFF_HEADS = 8
DIFF_VDIM = 2 * HEAD_DIM
NSA_HEADS = 16
NSA_GROUPS = 4
NSA_HPG = NSA_HEADS // NSA_GROUPS
CMP_BLOCK = 32
CMP_STRIDE = 16
CMP_HIDDEN = 256
SEL_BLOCK = 64
SEL_TOP = 16
WINDOW = 512
FORCED_SCORE = 1e4
D_FF = 4 * D_MODEL
ROPE_THETA = 10000.0
EPS = 1e-6
Q_BLOCK = 128

DIFF_QK = DIFF_HEADS * 2 * HEAD_DIM
DIFF_V = DIFF_HEADS * DIFF_VDIM
NSA_Q = NSA_HEADS * HEAD_DIM
NSA_KV = NSA_GROUPS * HEAD_DIM
NSA_GATE = NSA_HEADS * 3
MERGE_GATE = 2 * D_MODEL
COLUMN_SPLITS = (DIFF_QK, DIFF_QK, DIFF_V, NSA_Q, NSA_KV, NSA_KV, NSA_KV, NSA_KV, NSA_KV, NSA_KV, NSA_GATE, MERGE_GATE)
D_IN = DIFF_QK * 2 + DIFF_V + NSA_Q + 6 * NSA_KV + NSA_GATE + MERGE_GATE

kernel_name = "hybrid_diffattn_nsa_gated_block"


def rmsnorm(x, g):
    xf = x.astype(jnp.float32)
    y = xf * lax.rsqrt(jnp.mean(xf * xf, axis=-1, keepdims=True) + EPS)
    return (y * g.astype(jnp.float32)).astype(x.dtype)


def rope(x, pos):
    half = x.shape[-1] // 2
    inv_freq = ROPE_THETA ** (-jnp.arange(half, dtype=jnp.float32) / half)
    ang = pos.astype(jnp.float32)[:, None] * inv_freq[None, :]
    bshape = (1, ang.shape[0]) + (1,) * (x.ndim - 3) + (half,)
    cos = jnp.cos(ang).reshape(bshape)
    sin = jnp.sin(ang).reshape(bshape)
    xf = x.astype(jnp.float32)
    x1, x2 = xf[..., :half], xf[..., half:]
    return jnp.concatenate([x1 * cos - x2 * sin, x2 * cos + x1 * sin], axis=-1).astype(x.dtype)


def masked_softmax(s, mask):
    s = jnp.where(mask, s.astype(jnp.float32), -jnp.inf)
    m = jnp.max(s, axis=-1, keepdims=True)
    m = jnp.where(jnp.isfinite(m), m, 0.0)
    e = jnp.where(mask, jnp.exp(s - m), 0.0)
    return e / jnp.maximum(jnp.sum(e, axis=-1, keepdims=True), 1e-30)


def diff_attention(q, k, v, lam, lambda_init, subln_g):
    B, S, H, _, Dh = q.shape
    scale = Dh ** -0.5
    kpos = jnp.arange(S)

    def block(i):
        q0 = i * Q_BLOCK
        qb = lax.dynamic_slice_in_dim(q, q0, Q_BLOCK, axis=1)
        s = jnp.einsum('bqhcd,bkhcd->bhcqk', qb, k).astype(jnp.float32) * scale
        qpos = q0 + jnp.arange(Q_BLOCK)
        causal = kpos[None, :] <= qpos[:, None]
        p = jax.nn.softmax(jnp.where(causal, s, -jnp.inf), axis=-1)
        a = p[:, :, 0] - lam * p[:, :, 1]
        return jnp.einsum('bhqk,bkhe->bqhe', a.astype(v.dtype), v)

    o = lax.map(block, jnp.arange(S // Q_BLOCK))
    o = o.transpose(1, 0, 2, 3, 4).reshape(B, S, H, -1)
    o = rmsnorm(o, subln_g) * (1.0 - lambda_init)
    return o.reshape(B, S, H * o.shape[-1])


def compress(kv, pos_emb, w1, w2):
    B, S, G, Dh = kv.shape
    n_cmp = (S - CMP_BLOCK) // CMP_STRIDE + 1
    idx = jnp.arange(n_cmp)[:, None] * CMP_STRIDE + jnp.arange(CMP_BLOCK)[None, :]
    blocks = kv[:, idx] + pos_emb[None, None, :, None, :]
    blocks = blocks.transpose(0, 1, 3, 2, 4).reshape(B, n_cmp, G, CMP_BLOCK * Dh)
    return jax.nn.gelu(blocks @ w1) @ w2


def nsa_attention(q, k_cmp, v_cmp, k_slc, v_slc, k_win, v_win, gates):
    B, S, Hq, Dh = q.shape
    G, Hg = NSA_GROUPS, NSA_HPG
    scale = Dh ** -0.5
    dt = v_slc.dtype
    n_cmp = k_cmp.shape[1]
    n_sel = S // SEL_BLOCK
    n_top = min(SEL_TOP, n_sel)
    cmp_start = jnp.arange(n_cmp) * CMP_STRIDE
    cmp_end = cmp_start + CMP_BLOCK - 1
    sel_start = jnp.arange(n_sel) * SEL_BLOCK
    sel_ids = jnp.arange(n_sel)
    overlap = ((cmp_start[:, None] < sel_start[None, :] + SEL_BLOCK)
               & (cmp_end[:, None] >= sel_start[None, :])).astype(jnp.float32)
    k_blk = k_slc.reshape(B, n_sel, SEL_BLOCK, G, Dh).transpose(0, 3, 1, 2, 4)
    v_blk = v_slc.reshape(B, n_sel, SEL_BLOCK, G, Dh).transpose(0, 3, 1, 2, 4)
    kw_pad = jnp.pad(k_win, ((0, 0), (WINDOW, 0), (0, 0), (0, 0)))
    vw_pad = jnp.pad(v_win, ((0, 0), (WINDOW, 0), (0, 0), (0, 0)))
    b_idx = jnp.arange(B)[:, None, None, None]
    g_idx = jnp.arange(G)[None, :, None, None]

    def block(i):
        q0 = i * Q_BLOCK
        qpos = q0 + jnp.arange(Q_BLOCK)
        qb = lax.dynamic_slice_in_dim(q, q0, Q_BLOCK, axis=1).reshape(B, Q_BLOCK, G, Hg, Dh)
        gb = lax.dynamic_slice_in_dim(gates, q0, Q_BLOCK, axis=1)
        s = jnp.einsum('bqghd,bngd->bghqn', qb, k_cmp) * scale
        p_cmp = masked_softmax(s, cmp_end[None, :] <= qpos[:, None])
        o_cmp = jnp.einsum('bghqn,bngd->bqghd', p_cmp.astype(dt), v_cmp)
        imp = jnp.einsum('bghqn,nj->bgqj', p_cmp, overlap)
        cur = qpos // SEL_BLOCK
        forced = (sel_ids[None, :] == 0) | (sel_ids[None, :] == cur[:, None]) | (sel_ids[None, :] == cur[:, None] - 1)
        imp = jnp.where(forced, FORCED_SCORE, imp)
        imp = jnp.where(sel_start[None, :] <= qpos[:, None], imp, -jnp.inf)
        top_val, top_idx = lax.top_k(imp, n_top)
        sel_ok = jnp.isfinite(top_val)
        ks = k_blk[b_idx, g_idx, top_idx]
        vs = v_blk[b_idx, g_idx, top_idx]
        s = jnp.einsum('bqghd,bgqnld->bghqnl', qb, ks) * scale
        tok_pos = top_idx[..., None] * SEL_BLOCK + jnp.arange(SEL_BLOCK)
        m = sel_ok[..., None] & (tok_pos <= qpos[None, None, :, None, None])
        p = masked_softmax(s.reshape(B, G, Hg, Q_BLOCK, n_top * SEL_BLOCK),
                           m.reshape(B, G, 1, Q_BLOCK, n_top * SEL_BLOCK))
        p = p.reshape(B, G, Hg, Q_BLOCK, n_top, SEL_BLOCK)
        o_slc = jnp.einsum('bghqnl,bgqnld->bqghd', p.astype(dt), vs)
        kw = lax.dynamic_slice_in_dim(kw_pad, q0, WINDOW + Q_BLOCK, axis=1)
        vw = lax.dynamic_slice_in_dim(vw_pad, q0, WINDOW + Q_BLOCK, axis=1)
        kwpos = q0 - WINDOW + jnp.arange(WINDOW + Q_BLOCK)
        dist = qpos[:, None] - kwpos[None, :]
        s = jnp.einsum('bqghd,bkgd->bghqk', qb, kw) * scale
        p = masked_softmax(s, (dist >= 0) & (dist < WINDOW) & (kwpos[None, :] >= 0))
        o_win = jnp.einsum('bghqk,bkgd->bqghd', p.astype(dt), vw)
        shp = (B, Q_BLOCK, Hq, Dh)
        return (gb[..., 0:1] * o_cmp.reshape(shp) + gb[..., 1:2] * o_slc.reshape(shp)
                + gb[..., 2:3] * o_win.reshape(shp))

    o = lax.map(block, jnp.arange(S // Q_BLOCK))
    return o.transpose(1, 0, 2, 3, 4).reshape(B, S, Hq * Dh)


def setup_inputs(seed: int = 0) -> dict:
    key = jax.random.key(seed)
    k = jax.random.split(key, 26)
    f32 = jnp.float32
    nrm = lambda kk, shape, sc: jax.random.normal(kk, shape, f32) * sc
    gain = lambda kk, shape: 1.0 + 0.02 * jax.random.normal(kk, shape, f32)
    L = DEPTH
    return {
        "x": jax.random.normal(k[0], (BATCH, SEQ, D_MODEL), f32),
        "ln_mix_g": gain(k[1], (L, D_MODEL)),
        "w_in": nrm(k[2], (L, D_MODEL, D_IN), D_MODEL ** -0.5),
        "diff_q_norm_g": gain(k[3], (L, HEAD_DIM)),
        "diff_k_norm_g": gain(k[4], (L, HEAD_DIM)),
        "diff_lambda_q1": nrm(k[5], (L, HEAD_DIM), 0.1),
        "diff_lambda_k1": nrm(k[6], (L, HEAD_DIM), 0.1),
        "diff_lambda_q2": nrm(k[7], (L, HEAD_DIM), 0.1),
        "diff_lambda_k2": nrm(k[8], (L, HEAD_DIM), 0.1),
        "diff_subln_g": gain(k[9], (L, DIFF_VDIM)),
        "nsa_q_norm_g": gain(k[10], (L, HEAD_DIM)),
        "nsa_k_norm_g": gain(k[11], (L, 3, HEAD_DIM)),
        "cmp_pos_k": nrm(k[12], (L, CMP_BLOCK, HEAD_DIM), 0.1),
        "cmp_pos_v": nrm(k[13], (L, CMP_BLOCK, HEAD_DIM), 0.1),
        "cmp_k_w1": nrm(k[14], (L, CMP_BLOCK * HEAD_DIM, CMP_HIDDEN), (CMP_BLOCK * HEAD_DIM) ** -0.5),
        "cmp_k_w2": nrm(k[15], (L, CMP_HIDDEN, HEAD_DIM), CMP_HIDDEN ** -0.5),
        "cmp_v_w1": nrm(k[16], (L, CMP_BLOCK * HEAD_DIM, CMP_HIDDEN), (CMP_BLOCK * HEAD_DIM) ** -0.5),
        "cmp_v_w2": nrm(k[17], (L, CMP_HIDDEN, HEAD_DIM), CMP_HIDDEN ** -0.5),
        "w_proj_diff": nrm(k[18], (L, DIFF_V, D_MODEL), DIFF_V ** -0.5),
        "w_proj_nsa": nrm(k[19], (L, NSA_Q, D_MODEL), NSA_Q ** -0.5),
        "w_out": nrm(k[20], (L, D_MODEL, D_MODEL), D_MODEL ** -0.5),
        "ln_mlp_g": gain(k[21], (L, D_MODEL)),
        "w_mlp_up": nrm(k[22], (L, D_MODEL, D_FF), D_MODEL ** -0.5),
        "w_mlp_down": nrm(k[23], (L, D_FF, D_MODEL), D_FF ** -0.5),
    }


def reference(x, ln_mix_g, w_in, diff_q_norm_g, diff_k_norm_g, diff_lambda_q1, diff_lambda_k1,
              diff_lambda_q2, diff_lambda_k2, diff_subln_g, nsa_q_norm_g, nsa_k_norm_g,
              cmp_pos_k, cmp_pos_v, cmp_k_w1, cmp_k_w2, cmp_v_w1, cmp_v_w2,
              w_proj_diff, w_proj_nsa, w_out, ln_mlp_g, w_mlp_up, w_mlp_down):
    B, S, D = x.shape
    G = NSA_GROUPS
    pos = jnp.arange(S)
    n_cmp = (S - CMP_BLOCK) // CMP_STRIDE + 1
    cmp_center = jnp.arange(n_cmp) * CMP_STRIDE + (CMP_BLOCK - 1) / 2.0
    split_at = [int(c) for c in np.cumsum(COLUMN_SPLITS)[:-1]]
    for l in range(DEPTH):
        h = rmsnorm(x, ln_mix_g[l])
        proj = h @ w_in[l]
        (dq, dk, dv, nq, kc, vc, ksl, vsl, kwn, vwn, ng, mg) = jnp.split(proj, split_at, axis=-1)
        dq = rope(rmsnorm(dq.reshape(B, S, DIFF_HEADS, 2, HEAD_DIM), diff_q_norm_g[l]), pos)
        dk = rope(rmsnorm(dk.reshape(B, S, DIFF_HEADS, 2, HEAD_DIM), diff_k_norm_g[l]), pos)
        dv = dv.reshape(B, S, DIFF_HEADS, DIFF_VDIM)
        lambda_init = 0.8 - 0.6 * math.exp(-0.3 * l)
        lam = (jnp.exp(jnp.sum(diff_lambda_q1[l].astype(jnp.float32) * diff_lambda_k1[l].astype(jnp.float32)))
               - jnp.exp(jnp.sum(diff_lambda_q2[l].astype(jnp.float32) * diff_lambda_k2[l].astype(jnp.float32)))
               + lambda_init)
        y_diff = diff_attention(dq, dk, dv, lam, lambda_init, diff_subln_g[l])
        nq = rope(rmsnorm(nq.reshape(B, S, NSA_HEADS, HEAD_DIM), nsa_q_norm_g[l]), pos)
        k_cmp = compress(kc.reshape(B, S, G, HEAD_DIM), cmp_pos_k[l], cmp_k_w1[l], cmp_k_w2[l])
        k_cmp = rope(rmsnorm(k_cmp, nsa_k_norm_g[l, 0]), cmp_center)
        v_cmp = compress(vc.reshape(B, S, G, HEAD_DIM), cmp_pos_v[l], cmp_v_w1[l], cmp_v_w2[l])
        k_slc = rope(rmsnorm(ksl.reshape(B, S, G, HEAD_DIM), nsa_k_norm_g[l, 1]), pos)
        k_win = rope(rmsnorm(kwn.reshape(B, S, G, HEAD_DIM), nsa_k_norm_g[l, 2]), pos)
        nsa_gates = jax.nn.sigmoid(ng).reshape(B, S, NSA_HEADS, 3)
        y_nsa = nsa_attention(nq, k_cmp, v_cmp, k_slc, vsl.reshape(B, S, G, HEAD_DIM), k_win,
                              vwn.reshape(B, S, G, HEAD_DIM), nsa_gates)
        merge = jax.nn.sigmoid(mg).reshape(B, S, 2, D)
        mixed = merge[:, :, 0] * (y_diff @ w_proj_diff[l]) + merge[:, :, 1] * (y_nsa @ w_proj_nsa[l])
        x = x + mixed @ w_out[l]
        h = rmsnorm(x, ln_mlp_g[l])
        x = x + jnp.square(jax.nn.relu(h @ w_mlp_up[l])) @ w_mlp_down[l]
    return x
```

```python
import functools
import math

import numpy as np
import jax
import jax.numpy as jnp
from jax import lax
from jax.experimental import pallas as pl
from jax.experimental.pallas import tpu as pltpu

F32 = jnp.float32
BF16 = jnp.bfloat16

LANES = 128
HEAD_DIM = 64
HALF = HEAD_DIM // 2
DIFF_HEADS = 8
NSA_HEADS = 16
NSA_GROUPS = 4
NSA_HPG = NSA_HEADS // NSA_GROUPS
CMP_BLOCK = 32
CMP_STRIDE = 16
SEL_BLOCK = 64
SEL_TOP = 16
WINDOW = 512
FORCED_SCORE = 1e4
ROPE_THETA = 10000.0
EPS = 1e-6
NEG_BLOCK = -1e9
NEG_TOKEN = -1e30

PROJ_TM = 1024
PROJ_TN = 512
ATT_T = 512
NSA_T = 256
MERGE_TM = 512
MLP_TM = 1024
MLP_TF = 1024
VMEM_LIMIT = 56 * 1024 * 1024

_NT = (((1,), (1,)), ((), ()))


def _dot(a, b):
    return jnp.dot(a, b, preferred_element_type=F32)


def _dot_nt(a, b):
    return lax.dot_general(a, b, _NT, preferred_element_type=F32)


def _split_bf16(x):
    hi = x.astype(BF16)
    lo = (x - hi.astype(F32)).astype(BF16)
    return hi, lo


def _swap_halves_within_heads(y):
    lane = lax.broadcasted_iota(jnp.int32, y.shape, 1)
    first_half = (lane % HEAD_DIM) < HALF
    return jnp.where(first_half, pltpu.roll(y, LANES - HALF, axis=1), pltpu.roll(y, HALF, axis=1))


def _rope(y, cos, sin_signed):
    return y * cos + _swap_halves_within_heads(y) * sin_signed


def _inproj_body(x_ref, g_ref, w_ref, h_scr):
    @pl.when(pl.program_id(1) == 0)
    def _():
        x = x_ref[...]
        ms = jnp.mean(x * x, axis=-1, keepdims=True)
        h_scr[...] = (x * lax.rsqrt(ms + EPS) * g_ref[...]).astype(BF16)

    return _dot(h_scr[...], w_ref[...])


def _inproj_rope_kernel(x_ref, g_ref, w_ref, gain_ref, cos_ref, sin_ref, bd_ref, o_ref, h_scr):
    acc = _inproj_body(x_ref, g_ref, w_ref, h_scr)
    cos = cos_ref[...]
    sin = sin_ref[...]
    bd = bd_ref[...]
    for c in range(acc.shape[1] // LANES):
        sl = slice(c * LANES, (c + 1) * LANES)
        xc = acc[:, sl]
        hi, lo = _split_bf16(xc * xc)
        ss = _dot(hi, bd) + _dot(lo, bd)
        y = xc * lax.rsqrt(ss * (1.0 / HEAD_DIM) + EPS) * gain_ref[:, sl]
        o_ref[:, sl] = _rope(y, cos, sin).astype(o_ref.dtype)


def _inproj_plain_kernel(x_ref, g_ref, w_ref, o_ref, h_scr):
    o_ref[...] = _inproj_body(x_ref, g_ref, w_ref, h_scr).astype(o_ref.dtype)


def _inproj_sigmoid_kernel(x_ref, g_ref, w_ref, o_ref, h_scr):
    o_ref[...] = jax.nn.sigmoid(_inproj_body(x_ref, g_ref, w_ref, h_scr)).astype(o_ref.dtype)


def _inproj(x2d, g, w, mode, out_dtype, seq, extra=()):
    n, d = x2d.shape
    ncol = w.shape[1]
    tm, tn = min(PROJ_TM, seq), PROJ_TN
    assert n % tm == 0 and seq % tm == 0 and ncol % tn == 0
    in_specs = [
        pl.BlockSpec((tm, d), lambda i, j: (i, 0)),
        pl.BlockSpec((1, d), lambda i, j: (0, 0)),
        pl.BlockSpec((d, tn), lambda i, j: (0, j)),
    ]
    if mode == "rope":
        kern = _inproj_rope_kernel
        pos_blocks = seq // tm
        in_specs += [
            pl.BlockSpec((1, tn), lambda i, j: (0, j)),
            pl.BlockSpec((tm, LANES), lambda i, j: (i % pos_blocks, 0)),
            pl.BlockSpec((tm, LANES), lambda i, j: (i % pos_blocks, 0)),
            pl.BlockSpec((LANES, LANES), lambda i, j: (0, 0)),
        ]
    elif mode == "plain":
        kern = _inproj_plain_kernel
    else:
        kern = _inproj_sigmoid_kernel
    return pl.pallas_call(
        kern,
        out_shape=jax.ShapeDtypeStruct((n, ncol), out_dtype),
        grid=(n // tm, ncol // tn),
        in_specs=in_specs,
        out_specs=pl.BlockSpec((tm, tn), lambda i, j: (i, j)),
        scratch_shapes=[pltpu.VMEM((tm, d), BF16)],
        compiler_params=pltpu.CompilerParams(
            dimension_semantics=("parallel", "arbitrary"), vmem_limit_bytes=VMEM_LIMIT),
        name="inproj_" + mode,
    )(x2d, g, w, *extra)


def _compress_kernel(r_ref, pos_ref, w1_ref, w2_ref, *rest, is_key):
    if is_key:
        gain_ref, cos_ref, sin_ref, o_ref = rest
    else:
        (o_ref,) = rest
    half = r_ref.shape[1]
    r = r_ref[...].astype(F32)
    top = _dot((r + pos_ref[:, :half]).astype(BF16), w1_ref[:half, :])
    bot = _dot((r + pos_ref[:, half:]).astype(BF16), w1_ref[half:, :])
    hidden = top + pltpu.roll(bot, bot.shape[0] - 1, axis=0)
    out = _dot(jax.nn.gelu(hidden).astype(BF16), w2_ref[...])
    if is_key:
        ms = jnp.mean(out * out, axis=-1, keepdims=True)
        out = _rope(out * lax.rsqrt(ms + EPS) * gain_ref[...], cos_ref[...], sin_ref[...])
    o_ref[...] = out.astype(o_ref.dtype)


def _compress(r, pos, w1, w2dup, extra, is_key):
    b, g, rows, half = r.shape
    in_specs = [
        pl.BlockSpec((None, None, rows, half), lambda i, j: (i, j, 0, 0)),
        pl.BlockSpec(pos.shape, lambda i, j: (0, 0)),
        pl.BlockSpec(w1.shape, lambda i, j: (0, 0)),
        pl.BlockSpec(w2dup.shape, lambda i, j: (0, 0)),
    ] + [pl.BlockSpec(e.shape, lambda i, j: (0, 0)) for e in extra]
    return pl.pallas_call(
        functools.partial(_compress_kernel, is_key=is_key),
        out_shape=jax.ShapeDtypeStruct((b, g, rows, LANES), BF16),
        grid=(b, g),
        in_specs=in_specs,
        out_specs=pl.BlockSpec((None, None, rows, LANES), lambda i, j: (i, j, 0, 0)),
        compiler_params=pltpu.CompilerParams(
            dimension_semantics=("parallel", "parallel"), vmem_limit_bytes=VMEM_LIMIT),
        name="compress_k" if is_key else "compress_v",
    )(r, pos, w1, w2dup, *extra)


def _online_update(s, v, m_ref, l_ref, acc_ref, idx):
    m_prev = m_ref[idx]
    m_next = jnp.maximum(m_prev, jnp.max(s, axis=1, keepdims=True))
    alpha = jnp.exp(m_prev - m_next)
    p = jnp.exp(s - jnp.concatenate([m_next] * (s.shape[1] // LANES), axis=1))
    l_ref[idx] = alpha * l_ref[idx] + jnp.sum(p, axis=1, keepdims=True)
    m_ref[idx] = m_next
    acc_ref[idx] = acc_ref[idx] * alpha + _dot(p.astype(BF16), v)


def _init_softmax_state(m_ref, l_ref, acc_ref):
    m_ref[...] = jnp.full(m_ref.shape, -jnp.inf, F32)
    l_ref[...] = jnp.zeros(l_ref.shape, F32)
    acc_ref[...] = jnp.zeros(acc_ref.shape, F32)


def _stacked_rows_cols(rows, cols, heads=1):
    assert rows & (rows - 1) == 0
    r = lax.broadcasted_iota(jnp.int32, (heads * rows, cols), 0) & (rows - 1)
    c = lax.broadcasted_iota(jnp.int32, (heads * rows, cols), 1)
    return r, c


def _local_causal(rows, cols, heads=1):
    r, c = _stacked_rows_cols(rows, cols, heads)
    return c <= r


def _diff_attn_kernel(q_ref, k_ref, v_ref, lq1_ref, lk1_ref, lq2_ref, lk2_ref, sg_ref, o_ref,
                      m_ref, l_ref, acc_ref, *, lambda_init):
    i = pl.program_id(2)
    t = q_ref.shape[0]
    q = q_ref[...]
    lane = lax.broadcasted_iota(jnp.int32, q.shape, 1)
    zero = jnp.zeros_like(q)
    qc = (jnp.where(lane < HEAD_DIM, q, zero), jnp.where(lane >= HEAD_DIM, q, zero))
    _init_softmax_state(m_ref, l_ref, acc_ref)

    def step(j, masked):
        start = pl.multiple_of(j * t, t)
        k = k_ref[pl.ds(start, t), :]
        v = v_ref[pl.ds(start, t), :]
        for c in range(2):
            s = _dot_nt(qc[c], k)
            if masked:
                s = jnp.where(_local_causal(t, t), s, NEG_TOKEN)
            _online_update(s, v, m_ref, l_ref, acc_ref, c)

    def body(j, carry):
        step(j, False)
        return carry

    lax.fori_loop(0, i, body, 0)
    step(i, True)

    lam = (jnp.exp(jnp.sum(lq1_ref[...] * lk1_ref[...], axis=-1, keepdims=True))
           - jnp.exp(jnp.sum(lq2_ref[...] * lk2_ref[...], axis=-1, keepdims=True)) + lambda_init)
    o = acc_ref[0] / l_ref[0] - lam * (acc_ref[1] / l_ref[1])
    ms = jnp.mean(o * o, axis=-1, keepdims=True)
    o_ref[...] = (o * lax.rsqrt(ms + EPS) * sg_ref[...] * (1.0 - lambda_init)).astype(o_ref.dtype)


def _diff_attention(a, p, lams, subln_g, batch, seq, lambda_init):
    t = min(ATT_T, seq)
    nq = seq // t
    dv = 2 * HEAD_DIM
    k_base = DIFF_HEADS
    small = lambda arr: pl.BlockSpec(arr.shape, lambda b, h, i: (0, 0))
    return pl.pallas_call(
        functools.partial(_diff_attn_kernel, lambda_init=lambda_init),
        out_shape=jax.ShapeDtypeStruct((batch * seq, DIFF_HEADS * dv), BF16),
        grid=(batch, DIFF_HEADS, nq),
        in_specs=[
            pl.BlockSpec((t, LANES), lambda b, h, i: (b * nq + i, h)),
            pl.BlockSpec((seq, LANES), lambda b, h, i: (b, k_base + h)),
            pl.BlockSpec((seq, LANES), lambda b, h, i: (b, h)),
            small(lams[0]), small(lams[1]), small(lams[2]), small(lams[3]), small(subln_g),
        ],
        out_specs=pl.BlockSpec((t, dv), lambda b, h, i: (b * nq + i, h)),
        scratch_shapes=[pltpu.VMEM((2, t, LANES), F32)] * 3,
        compiler_params=pltpu.CompilerParams(
            dimension_semantics=("parallel", "parallel", "arbitrary"), vmem_limit_bytes=VMEM_LIMIT),
        name="diff_attention",
    )(a, a, p, *lams, subln_g)


def _nsa_kernel(q_ref, ksl_ref, vsl_ref, kwn_ref, vwn_ref, kc_ref, vc_ref, gate_ref, ovt_ref, o_ref,
                qaug_ref, m_ref, l_ref, acc_ref, imp_ref):
    i = pl.program_id(2)
    t = q_ref.shape[0]
    q0 = i * t
    n_sel = ksl_ref.shape[0] // SEL_BLOCK
    assert n_sel <= LANES - HEAD_DIM and imp_ref.shape[0] == LANES
    lane = lax.broadcasted_iota(jnp.int32, (t, LANES), 1)
    low = lane < HEAD_DIM

    q = q_ref[...].astype(F32)
    heads = []
    for pair in range(NSA_HPG // 2):
        qp = q[:, pair * LANES:(pair + 1) * LANES]
        heads += [qp, pltpu.roll(qp, HEAD_DIM, axis=1)]

    q4 = jnp.concatenate([jnp.where(low, hq, 0.0).astype(BF16) for hq in heads], axis=0)
    n_cmp = kc_ref.shape[0]
    s = _dot_nt(q4, kc_ref[...])
    r4, n_idx = _stacked_rows_cols(t, n_cmp, NSA_HPG)
    valid4 = n_idx * CMP_STRIDE + (CMP_BLOCK - 1) <= q0 + r4
    m = jnp.max(jnp.where(valid4, s, -jnp.inf), axis=1, keepdims=True)
    m = jnp.where(m == -jnp.inf, 0.0, m)
    e = jnp.where(valid4, jnp.exp(s - m), 0.0)
    p = e / jnp.maximum(jnp.sum(e, axis=1, keepdims=True), 1e-30)
    o_cmp = _dot(p.astype(BF16), vc_ref[...])

    psum = p[0:t] + p[t:2 * t] + p[2 * t:3 * t] + p[3 * t:4 * t]
    hi, lo = _split_bf16(psum)
    ovt = ovt_ref[...]
    imp = _dot_nt(ovt, hi) + _dot_nt(ovt, lo)
    rows = imp.shape[0]
    jrow = lax.broadcasted_iota(jnp.int32, (rows, t), 0) - SEL_BLOCK
    qcol = q0 + lax.broadcasted_iota(jnp.int32, (rows, t), 1)
    cur = qcol // SEL_BLOCK
    forced = (jrow == 0) | (jrow == cur) | (jrow == cur - 1)
    imp = jnp.where(forced, FORCED_SCORE, imp)
    imp = jnp.where((jrow >= 0) & (jrow * SEL_BLOCK <= qcol), imp, -jnp.inf)
    imp_ref[...] = imp
    imp_b = imp[SEL_BLOCK:, :]
    jb = lax.broadcasted_iota(jnp.int32, imp_b.shape, 0)
    rank = jnp.zeros(imp_b.shape, F32)
    for kk in range(n_sel):
        rk = imp_ref[SEL_BLOCK + kk:SEL_BLOCK + kk + 1, :]
        rank = rank + jnp.where(rk > imp_b, 1.0, jnp.where((rk == imp_b) & (jb > kk), 1.0, 0.0))
    chosen = (rank < float(min(SEL_TOP, n_sel))) & (imp_b > -jnp.inf)
    bias_t = jnp.concatenate([jnp.zeros((SEL_BLOCK, t), F32), jnp.where(chosen, 0.0, NEG_BLOCK)], axis=0)
    bias = bias_t.T
    qaug_ref[...] = jnp.concatenate([jnp.where(low, hq, bias).astype(BF16) for hq in heads], axis=0)

    _init_softmax_state(m_ref, l_ref, acc_ref)
    causal4 = _local_causal(t, t, NSA_HPG)
    tk_lane = lax.broadcasted_iota(jnp.int32, (t, LANES), 1)
    tk_row = lax.broadcasted_iota(jnp.int32, (t, LANES), 0)

    def slc_step(j, masked):
        start = pl.multiple_of(j * t, t)
        blk = (start + tk_row) // SEL_BLOCK
        onehot = jnp.where(tk_lane - HEAD_DIM == blk, 1.0, 0.0).astype(BF16)
        k_aug = jnp.where(tk_lane < HEAD_DIM, ksl_ref[pl.ds(start, t), :], onehot)
        s = _dot_nt(qaug_ref[...], k_aug)
        if masked:
            s = jnp.where(causal4, s, NEG_TOKEN)
        _online_update(s, vsl_ref[pl.ds(start, t), :], m_ref, l_ref, acc_ref, 0)

    def slc_body(j, carry):
        slc_step(j, False)
        return carry

    lax.fori_loop(0, i, slc_body, 0)
    slc_step(i, True)

    def win_step(j, mask):
        start = pl.multiple_of(j * t, t)
        k = jnp.where(tk_lane < HEAD_DIM, kwn_ref[pl.ds(start, t), :], jnp.zeros((t, LANES), BF16))
        s = _dot_nt(qaug_ref[...], k)
        if mask is not None:
            s = jnp.where(mask, s, NEG_TOKEN)
        _online_update(s, vwn_ref[pl.ds(start, t), :], m_ref, l_ref, acc_ref, 1)

    assert WINDOW % t == 0
    back = WINDOW // t

    @pl.when(i >= back)
    def _():
        r, c = _stacked_rows_cols(t, t, NSA_HPG)
        win_step(i - back, (r + back * t) - c < WINDOW)

    for d in range(back - 1, 0, -1):
        @pl.when(i >= d)
        def _():
            win_step(i - d, None)

    win_step(i, causal4)

    o_slc = acc_ref[0] / l_ref[0]
    o_win = acc_ref[1] / l_ref[1]
    gate = gate_ref[...]
    outs = []
    for hh in range(NSA_HPG):
        sl = slice(hh * t, (hh + 1) * t)
        outs.append(gate[:, 3 * hh:3 * hh + 1] * o_cmp[sl] + gate[:, 3 * hh + 1:3 * hh + 2] * o_slc[sl]
                    + gate[:, 3 * hh + 2:3 * hh + 3] * o_win[sl])
    for pair in range(NSA_HPG // 2):
        o_ref[:, pair * LANES:(pair + 1) * LANES] = jnp.where(
            low, outs[2 * pair], outs[2 * pair + 1]).astype(o_ref.dtype)


def _nsa_attention(a, p, gates, kcmp, vcmp, ovt, batch, seq, col):
    t = min(NSA_T, seq)
    nq = seq // t
    gw = NSA_HPG * HEAD_DIM
    rows = NSA_HPG * t
    n_cmp = kcmp.shape[2]
    return pl.pallas_call(
        _nsa_kernel,
        out_shape=jax.ShapeDtypeStruct((batch * seq, NSA_HEADS * HEAD_DIM), BF16),
        grid=(batch, NSA_GROUPS, nq),
        in_specs=[
            pl.BlockSpec((t, gw), lambda b, g, i: (b * nq + i, col["nq"] // gw + g)),
            pl.BlockSpec((seq, LANES), lambda b, g, i: (b, col["ksl"] // LANES + g)),
            pl.BlockSpec((seq, LANES), lambda b, g, i: (b, col["vsl"] // LANES + g)),
            pl.BlockSpec((seq, LANES), lambda b, g, i: (b, col["kwn"] // LANES + g)),
            pl.BlockSpec((seq, LANES), lambda b, g, i: (b, col["vwn"] // LANES + g)),
            pl.BlockSpec((None, None, n_cmp, LANES), lambda b, g, i: (b, g, 0, 0)),
            pl.BlockSpec((None, None, n_cmp, LANES), lambda b, g, i: (b, g, 0, 0)),
            pl.BlockSpec((t, LANES), lambda b, g, i: (b * nq + i, col["ng"] // LANES + g)),
            pl.BlockSpec(ovt.shape, lambda b, g, i: (0, 0)),
        ],
        out_specs=pl.BlockSpec((t, gw), lambda b, g, i: (b * nq + i, g)),
        scratch_shapes=[
            pltpu.VMEM((rows, LANES), BF16),
            pltpu.VMEM((2, rows, LANES), F32),
            pltpu.VMEM((2, rows, LANES), F32),
            pltpu.VMEM((2, rows, LANES), F32),
            pltpu.VMEM(ovt.shape[:1] + (t,), F32),
        ],
        compiler_params=pltpu.CompilerParams(
            dimension_semantics=("parallel", "parallel", "arbitrary"), vmem_limit_bytes=VMEM_LIMIT),
        name="nsa_attention",
    )(a, a, p, a, p, kcmp, vcmp, gates, ovt)


def _merge_kernel(yd_ref, yn_ref, g0_ref, g1_ref, x_ref, wd_ref, wn_ref, wo_ref, lg_ref, x1_ref, h_ref):
    mixed = g0_ref[...] * _dot(yd_ref[...], wd_ref[...]) + g1_ref[...] * _dot(yn_ref[...], wn_ref[...])
    x1 = x_ref[...] + _dot(mixed.astype(BF16), wo_ref[...])
    x1_ref[...] = x1
    ms = jnp.mean(x1 * x1, axis=-1, keepdims=True)
    h_ref[...] = (x1 * lax.rsqrt(ms + EPS) * lg_ref[...]).astype(h_ref.dtype)


def _merge(yd, yn, gates, x2d, wd, wn, wo, lg):
    n, d = x2d.shape
    tm = min(MERGE_TM, n)
    tok = lambda c: pl.BlockSpec((tm, d), lambda i: (i, c))
    full = lambda arr: pl.BlockSpec(arr.shape, lambda i: (0, 0))
    return pl.pallas_call(
        _merge_kernel,
        out_shape=(jax.ShapeDtypeStruct((n, d), F32), jax.ShapeDtypeStruct((n, d), BF16)),
        grid=(n // tm,),
        in_specs=[tok(0), tok(0), tok(0), tok(1), tok(0), full(wd), full(wn), full(wo), full(lg)],
        out_specs=(tok(0), tok(0)),
        compiler_params=pltpu.CompilerParams(
            dimension_semantics=("parallel",), vmem_limit_bytes=VMEM_LIMIT),
        name="merge_outproj",
    )(yd, yn, gates, gates, x2d, wd, wn, wo, lg)


def _mlp_kernel(h_ref, x1_ref, wu_ref, wd_ref, o_ref, acc_ref):
    f = pl.program_id(1)

    @pl.when(f == 0)
    def _():
        acc_ref[...] = x1_ref[...]

    up = jnp.maximum(_dot(h_ref[...], wu_ref[...]), 0.0)
    acc_ref[...] += _dot((up * up).astype(BF16), wd_ref[...])

    @pl.when(f == pl.num_programs(1) - 1)
    def _():
        o_ref[...] = acc_ref[...]


def _mlp(h, x1, wu, wd):
    n, d = x1.shape
    dff = wu.shape[1]
    tm, tf = min(MLP_TM, n), MLP_TF
    return pl.pallas_call(
        _mlp_kernel,
        out_shape=jax.ShapeDtypeStruct((n, d), F32),
        grid=(n // tm, dff // tf),
        in_specs=[
            pl.BlockSpec((tm, d), lambda i, f: (i, 0)),
            pl.BlockSpec((tm, d), lambda i, f: (i, 0)),
            pl.BlockSpec((d, tf), lambda i, f: (0, f)),
            pl.BlockSpec((tf, d), lambda i, f: (f, 0)),
        ],
        out_specs=pl.BlockSpec((tm, d), lambda i, f: (i, 0)),
        scratch_shapes=[pltpu.VMEM((tm, d), F32)],
        compiler_params=pltpu.CompilerParams(
            dimension_semantics=("parallel", "arbitrary"), vmem_limit_bytes=VMEM_LIMIT),
        name="mlp",
    )(h, x1, wu, wd)


def _rope_tables(pos):
    inv_freq = ROPE_THETA ** (-jnp.arange(HALF, dtype=F32) / HALF)
    ang = pos.astype(F32)[:, None] * inv_freq[None, :]
    c, s = jnp.cos(ang), jnp.sin(ang)
    return jnp.concatenate([c, c, c, c], axis=-1), jnp.concatenate([-s, s, -s, s], axis=-1)


def _dup_groups(w):
    d = w.shape[0]
    w = w.reshape(d, NSA_GROUPS, 1, HEAD_DIM)
    return jnp.broadcast_to(w, (d, NSA_GROUPS, 2, HEAD_DIM)).reshape(d, NSA_GROUPS * LANES)


def _pad_cols(w, total):
    return jnp.pad(w, ((0, 0), (0, total - w.shape[1])))


def _layer(x2d, batch, seq, layer, ln_mix_g, w_in, diff_q_norm_g, diff_k_norm_g, diff_lambda_q1,
           diff_lambda_k1, diff_lambda_q2, diff_lambda_k2, diff_subln_g, nsa_q_norm_g, nsa_k_norm_g,
           cmp_pos_k, cmp_pos_v, cmp_k_w1, cmp_k_w2, cmp_v_w1, cmp_v_w2, w_proj_diff, w_proj_nsa,
           w_out, ln_mlp_g, w_mlp_up, w_mlp_down):
    d = x2d.shape[1]
    diff_qk = DIFF_HEADS * 2 * HEAD_DIM
    diff_v = DIFF_HEADS * 2 * HEAD_DIM
    nsa_q = NSA_HEADS * HEAD_DIM
    nsa_kv = NSA_GROUPS * HEAD_DIM
    splits = np.cumsum([diff_qk, diff_qk, diff_v, nsa_q] + [nsa_kv] * 6 + [NSA_HEADS * 3, 2 * d])[:-1]
    (w_dq, w_dk, w_dv, w_nq, w_kc, w_vc, w_ksl, w_vsl, w_kwn, w_vwn, w_ng, w_mg) = jnp.split(
        w_in, [int(c) for c in splits], axis=1)
    scale = HEAD_DIM ** -0.5

    w_a = jnp.concatenate([w_dq, w_dk, w_nq, _dup_groups(w_ksl), _dup_groups(w_kwn)], axis=1).astype(BF16)
    col_a = {"dq": 0, "dk": diff_qk, "nq": 2 * diff_qk, "ksl": 2 * diff_qk + nsa_q,
             "kwn": 2 * diff_qk + nsa_q + 2 * nsa_kv}
    tile_g = lambda g, reps: jnp.tile(g.astype(F32), reps)
    gain_a = jnp.concatenate([
        tile_g(diff_q_norm_g, 2 * DIFF_HEADS) * scale,
        tile_g(diff_k_norm_g, 2 * DIFF_HEADS),
        tile_g(nsa_q_norm_g, NSA_HEADS) * scale,
        tile_g(nsa_k_norm_g[1], 2 * NSA_GROUPS),
        tile_g(nsa_k_norm_g[2], 2 * NSA_GROUPS),
    ])[None, :]
    cos, sin = _rope_tables(jnp.arange(seq))
    eye_heads = np.kron(np.eye(LANES // HEAD_DIM), np.ones((HEAD_DIM, HEAD_DIM)))
    block_ones = jnp.asarray(eye_heads, BF16)
    g_mix = ln_mix_g.astype(F32)[None, :]
    a = _inproj(x2d, g_mix, w_a, "rope", BF16, seq, (gain_a, cos, sin, block_ones))

    w_p = jnp.concatenate([w_dv, _dup_groups(w_vsl), _dup_groups(w_vwn), w_kc, w_vc], axis=1).astype(BF16)
    col_p = {"dv": 0, "vsl": diff_v, "vwn": diff_v + 2 * nsa_kv, "kc": diff_v + 4 * nsa_kv,
             "vc": diff_v + 5 * nsa_kv}
    p = _inproj(x2d, g_mix, w_p, "plain", BF16, seq)

    w_ng_g = w_ng.reshape(d, NSA_GROUPS, NSA_HPG * 3)
    w_ng_g = jnp.pad(w_ng_g, ((0, 0), (0, 0), (0, LANES - NSA_HPG * 3))).reshape(d, NSA_GROUPS * LANES)
    w_g = jnp.concatenate([w_mg, w_ng_g], axis=1)
    gates_cols = -(-w_g.shape[1] // PROJ_TN) * PROJ_TN
    w_g = _pad_cols(w_g, gates_cols).astype(BF16)
    gates = _inproj(x2d, g_mix, w_g, "sigmoid", F32, seq)

    n_runs = seq // CMP_STRIDE

    def runs(c0):
        kv = p[:, c0:c0 + nsa_kv].reshape(batch, seq, NSA_GROUPS, HEAD_DIM)
        return kv.transpose(0, 2, 1, 3).reshape(batch, NSA_GROUPS, n_runs, CMP_STRIDE * HEAD_DIM)

    dup = lambda w: jnp.concatenate([w, w], axis=1).astype(BF16)
    cmp_center = jnp.arange(n_runs) * CMP_STRIDE + (CMP_BLOCK - 1) / 2.0
    cos_c, sin_c = _rope_tables(cmp_center)
    gain_c = tile_g(nsa_k_norm_g[0], 2)[None, :]
    kcmp = _compress(runs(col_p["kc"]), cmp_pos_k.astype(F32).reshape(1, -1), cmp_k_w1.astype(BF16),
                     dup(cmp_k_w2), (gain_c, cos_c, sin_c), True)
    vcmp = _compress(runs(col_p["vc"]), cmp_pos_v.astype(F32).reshape(1, -1), cmp_v_w1.astype(BF16),
                     dup(cmp_v_w2), (), False)

    lambda_init = 0.8 - 0.6 * math.exp(-0.3 * layer)
    lams = [v.astype(F32)[None, :] for v in (diff_lambda_q1, diff_lambda_k1, diff_lambda_q2, diff_lambda_k2)]
    y_diff = _diff_attention(a, p, lams, diff_subln_g.astype(F32)[None, :], batch, seq, lambda_init)

    n_sel = seq // SEL_BLOCK
    cmp_start = np.arange(n_runs) * CMP_STRIDE
    sel_start = np.arange(n_sel) * SEL_BLOCK
    overlap = ((cmp_start[:, None] < sel_start[None, :] + SEL_BLOCK)
               & (cmp_start[:, None] + CMP_BLOCK - 1 >= sel_start[None, :]))
    ovt = np.zeros((LANES, n_runs))
    ovt[HEAD_DIM:HEAD_DIM + n_sel] = overlap.T
    col = {"nq": col_a["nq"], "ksl": col_a["ksl"], "kwn": col_a["kwn"], "vsl": col_p["vsl"],
           "vwn": col_p["vwn"], "ng": 2 * d}
    y_nsa = _nsa_attention(a, p, gates, kcmp, vcmp, jnp.asarray(ovt, BF16), batch, seq, col)

    x1, h2 = _merge(y_diff, y_nsa, gates, x2d, w_proj_diff.astype(BF16), w_proj_nsa.astype(BF16),
                    w_out.astype(BF16), ln_mlp_g.astype(F32)[None, :])
    return _mlp(h2, x1, w_mlp_up.astype(BF16), w_mlp_down.astype(BF16))


def kernel(x, ln_mix_g, w_in, diff_q_norm_g, diff_k_norm_g, diff_lambda_q1, diff_lambda_k1, diff_lambda_q2, diff_lambda_k2, diff_subln_g, nsa_q_norm_g, nsa_k_norm_g, cmp_pos_k, cmp_pos_v, cmp_k_w1, cmp_k_w2, cmp_v_w1, cmp_v_w2, w_proj_diff, w_proj_nsa, w_out, ln_mlp_g, w_mlp_up, w_mlp_down):
    batch, seq, d = x.shape
    params = (ln_mix_g, w_in, diff_q_norm_g, diff_k_norm_g, diff_lambda_q1, diff_lambda_k1, diff_lambda_q2,
              diff_lambda_k2, diff_subln_g, nsa_q_norm_g, nsa_k_norm_g, cmp_pos_k, cmp_pos_v, cmp_k_w1,
              cmp_k_w2, cmp_v_w1, cmp_v_w2, w_proj_diff, w_proj_nsa, w_out, ln_mlp_g, w_mlp_up, w_mlp_down)
    x2d = x.reshape(batch * seq, d)
    for layer in range(ln_mix_g.shape[0]):
        x2d = _layer(x2d, batch, seq, layer, *[prm[layer] for prm in params])
    return x2d.reshape(batch, seq, d)
```

```python
import functools
import math

import numpy as np
import jax
import jax.numpy as jnp
from jax import lax
from jax.experimental import pallas as pl
from jax.experimental.pallas import tpu as pltpu

F32 = jnp.float32
BF16 = jnp.bfloat16

LANES = 128
HEAD_DIM = 64
HALF = HEAD_DIM // 2
DIFF_HEADS = 8
NSA_HEADS = 16
NSA_GROUPS = 4
NSA_HPG = NSA_HEADS // NSA_GROUPS
CMP_BLOCK = 32
CMP_STRIDE = 16
SEL_BLOCK = 64
SEL_TOP = 16
WINDOW = 512
FORCED_SCORE = 1e4
ROPE_THETA = 10000.0
EPS = 1e-6
NEG_BLOCK = -1e9
NEG_TOKEN = -1e30

PROJ_TM = 1024
PROJ_TN = 512
ATT_T = 512
NSA_T = 512
MERGE_TM = 512
MLP_TM = 1024
MLP_TF = 1024
VMEM_LIMIT = 56 * 1024 * 1024

_NT = (((1,), (1,)), ((), ()))


def _dot(a, b):
    return jnp.dot(a, b, preferred_element_type=F32)


def _dot_nt(a, b):
    return lax.dot_general(a, b, _NT, preferred_element_type=F32)


def _split_bf16(x):
    hi = x.astype(BF16)
    lo = (x - hi.astype(F32)).astype(BF16)
    return hi, lo


def _swap_halves_within_heads(y):
    lane = lax.broadcasted_iota(jnp.int32, y.shape, 1)
    first_half = (lane % HEAD_DIM) < HALF
    return jnp.where(first_half, pltpu.roll(y, LANES - HALF, axis=1), pltpu.roll(y, HALF, axis=1))


def _rope(y, cos, sin_signed):
    return y * cos + _swap_halves_within_heads(y) * sin_signed


def _inproj_body(x_ref, g_ref, w_ref, h_scr):
    @pl.when(pl.program_id(1) == 0)
    def _():
        x = x_ref[...]
        ms = jnp.mean(x * x, axis=-1, keepdims=True)
        h_scr[...] = (x * lax.rsqrt(ms + EPS) * g_ref[...]).astype(BF16)

    return _dot(h_scr[...], w_ref[...])


def _inproj_rope_kernel(x_ref, g_ref, w_ref, gain_ref, cos_ref, sin_ref, bd_ref, o_ref, h_scr):
    acc = _inproj_body(x_ref, g_ref, w_ref, h_scr)
    cos = cos_ref[...]
    sin = sin_ref[...]
    bd = bd_ref[...]
    for c in range(acc.shape[1] // LANES):
        sl = slice(c * LANES, (c + 1) * LANES)
        xc = acc[:, sl]
        hi, lo = _split_bf16(xc * xc)
        ss = _dot(hi, bd) + _dot(lo, bd)
        y = xc * lax.rsqrt(ss * (1.0 / HEAD_DIM) + EPS) * gain_ref[:, sl]
        o_ref[:, sl] = _rope(y, cos, sin).astype(o_ref.dtype)


def _inproj_plain_kernel(x_ref, g_ref, w_ref, o_ref, h_scr):
    o_ref[...] = _inproj_body(x_ref, g_ref, w_ref, h_scr).astype(o_ref.dtype)


def _inproj_sigmoid_kernel(x_ref, g_ref, w_ref, o_ref, h_scr):
    o_ref[...] = jax.nn.sigmoid(_inproj_body(x_ref, g_ref, w_ref, h_scr)).astype(o_ref.dtype)


def _inproj(x2d, g, w, mode, out_dtype, seq, extra=()):
    n, d = x2d.shape
    ncol = w.shape[1]
    tm, tn = min(PROJ_TM, seq), PROJ_TN
    assert n % tm == 0 and seq % tm == 0 and ncol % tn == 0
    in_specs = [
        pl.BlockSpec((tm, d), lambda i, j: (i, 0)),
        pl.BlockSpec((1, d), lambda i, j: (0, 0)),
        pl.BlockSpec((d, tn), lambda i, j: (0, j)),
    ]
    if mode == "rope":
        kern = _inproj_rope_kernel
        pos_blocks = seq // tm
        in_specs += [
            pl.BlockSpec((1, tn), lambda i, j: (0, j)),
            pl.BlockSpec((tm, LANES), lambda i, j: (i % pos_blocks, 0)),
            pl.BlockSpec((tm, LANES), lambda i, j: (i % pos_blocks, 0)),
            pl.BlockSpec((LANES, LANES), lambda i, j: (0, 0)),
        ]
    elif mode == "plain":
        kern = _inproj_plain_kernel
    else:
        kern = _inproj_sigmoid_kernel
    return pl.pallas_call(
        kern,
        out_shape=jax.ShapeDtypeStruct((n, ncol), out_dtype),
        grid=(n // tm, ncol // tn),
        in_specs=in_specs,
        out_specs=pl.BlockSpec((tm, tn), lambda i, j: (i, j)),
        scratch_shapes=[pltpu.VMEM((tm, d), BF16)],
        compiler_params=pltpu.CompilerParams(
            dimension_semantics=("parallel", "arbitrary"), vmem_limit_bytes=VMEM_LIMIT),
        name="inproj_" + mode,
    )(x2d, g, w, *extra)


def _compress_kernel(r_ref, pos_ref, w1_ref, w2_ref, *rest, is_key):
    if is_key:
        gain_ref, cos_ref, sin_ref, o_ref = rest
    else:
        (o_ref,) = rest
    half = r_ref.shape[1]
    r = r_ref[...].astype(F32)
    top = _dot((r + pos_ref[:, :half]).astype(BF16), w1_ref[:half, :])
    bot = _dot((r + pos_ref[:, half:]).astype(BF16), w1_ref[half:, :])
    hidden = top + pltpu.roll(bot, bot.shape[0] - 1, axis=0)
    out = _dot(jax.nn.gelu(hidden).astype(BF16), w2_ref[...])
    if is_key:
        ms = jnp.mean(out * out, axis=-1, keepdims=True)
        out = _rope(out * lax.rsqrt(ms + EPS) * gain_ref[...], cos_ref[...], sin_ref[...])
    o_ref[...] = out.astype(o_ref.dtype)


def _compress(r, pos, w1, w2dup, extra, is_key):
    b, g, rows, half = r.shape
    in_specs = [
        pl.BlockSpec((None, None, rows, half), lambda i, j: (i, j, 0, 0)),
        pl.BlockSpec(pos.shape, lambda i, j: (0, 0)),
        pl.BlockSpec(w1.shape, lambda i, j: (0, 0)),
        pl.BlockSpec(w2dup.shape, lambda i, j: (0, 0)),
    ] + [pl.BlockSpec(e.shape, lambda i, j: (0, 0)) for e in extra]
    return pl.pallas_call(
        functools.partial(_compress_kernel, is_key=is_key),
        out_shape=jax.ShapeDtypeStruct((b, g, rows, LANES), BF16),
        grid=(b, g),
        in_specs=in_specs,
        out_specs=pl.BlockSpec((None, None, rows, LANES), lambda i, j: (i, j, 0, 0)),
        compiler_params=pltpu.CompilerParams(
            dimension_semantics=("parallel", "parallel"), vmem_limit_bytes=VMEM_LIMIT),
        name="compress_k" if is_key else "compress_v",
    )(r, pos, w1, w2dup, *extra)


def _online_update(s, v_ones, m_ref, acc_ref, idx):
    m_prev = m_ref[idx]
    m_next = jnp.maximum(m_prev, jnp.max(s, axis=1, keepdims=True))
    alpha = jnp.exp(m_prev - m_next)
    p = jnp.exp(s - jnp.concatenate([m_next] * (s.shape[1] // LANES), axis=1))
    m_ref[idx] = m_next
    scale = jnp.concatenate([alpha] * (acc_ref.shape[-1] // LANES), axis=1)
    acc_ref[idx] = acc_ref[idx] * scale + _dot(p.astype(BF16), v_ones)


def _init_softmax_state(m_ref, acc_ref):
    m_ref[...] = jnp.full(m_ref.shape, -jnp.inf, F32)
    acc_ref[...] = jnp.zeros(acc_ref.shape, F32)


def _stacked_rows_cols(rows, cols, heads=1):
    assert rows & (rows - 1) == 0
    r = lax.broadcasted_iota(jnp.int32, (heads * rows, cols), 0) & (rows - 1)
    c = lax.broadcasted_iota(jnp.int32, (heads * rows, cols), 1)
    return r, c


def _add_per_head(s, bias, heads):
    rows, cols = bias.shape
    return (s.reshape(heads, rows, cols) + bias[None]).reshape(heads * rows, cols)


def _diff_attn_kernel(q_ref, k_ref, v_ref, cb_ref, lq1_ref, lk1_ref, lq2_ref, lk2_ref, sg_ref, o_ref,
                      m_ref, acc_ref, vones_ref, *, lambda_init):
    i = pl.program_id(2)
    t = q_ref.shape[0]
    dv = v_ref.shape[1]

    @pl.when(i == 0)
    def _():
        vones_ref[:, :dv] = v_ref[...]
        vones_ref[:, dv:] = jnp.ones((v_ref.shape[0], vones_ref.shape[1] - dv), BF16)

    q = q_ref[...]
    lane = lax.broadcasted_iota(jnp.int32, q.shape, 1)
    zero = jnp.zeros_like(q)
    qc = (jnp.where(lane < HEAD_DIM, q, zero), jnp.where(lane >= HEAD_DIM, q, zero))
    _init_softmax_state(m_ref, acc_ref)

    def step(j, masked):
        start = pl.multiple_of(j * t, t)
        k = k_ref[pl.ds(start, t), :]
        v = vones_ref[pl.ds(start, t), :]
        for c in range(2):
            s = _dot_nt(qc[c], k)
            if masked:
                s = s + cb_ref[...]
            _online_update(s, v, m_ref, acc_ref, c)

    def body(j, carry):
        step(j, False)
        return carry

    lax.fori_loop(0, i, body, 0)
    step(i, True)

    lam = (jnp.exp(jnp.sum(lq1_ref[...] * lk1_ref[...], axis=-1, keepdims=True))
           - jnp.exp(jnp.sum(lq2_ref[...] * lk2_ref[...], axis=-1, keepdims=True)) + lambda_init)
    o = acc_ref[0, :, :dv] / acc_ref[0, :, dv:] - lam * (acc_ref[1, :, :dv] / acc_ref[1, :, dv:])
    ms = jnp.mean(o * o, axis=-1, keepdims=True)
    o_ref[...] = (o * lax.rsqrt(ms + EPS) * sg_ref[...] * (1.0 - lambda_init)).astype(o_ref.dtype)


def _diff_attention(a, p, causal_bias, lams, subln_g, batch, seq, lambda_init, col_k):
    t = causal_bias.shape[0]
    nq = seq // t
    dv = 2 * HEAD_DIM
    small = lambda arr: pl.BlockSpec(arr.shape, lambda b, h, i: (0, 0))
    return pl.pallas_call(
        functools.partial(_diff_attn_kernel, lambda_init=lambda_init),
        out_shape=jax.ShapeDtypeStruct((batch * seq, DIFF_HEADS * dv), BF16),
        grid=(batch, DIFF_HEADS, nq),
        in_specs=[
            pl.BlockSpec((t, LANES), lambda b, h, i: (b * nq + i, h)),
            pl.BlockSpec((seq, LANES), lambda b, h, i: (b, col_k // LANES + h)),
            pl.BlockSpec((seq, dv), lambda b, h, i: (b, h)),
            small(causal_bias),
            small(lams[0]), small(lams[1]), small(lams[2]), small(lams[3]), small(subln_g),
        ],
        out_specs=pl.BlockSpec((t, dv), lambda b, h, i: (b * nq + i, h)),
        scratch_shapes=[
            pltpu.VMEM((2, t, LANES), F32),
            pltpu.VMEM((2, t, 2 * dv), F32),
            pltpu.VMEM((seq, 2 * dv), BF16),
        ],
        compiler_params=pltpu.CompilerParams(
            dimension_semantics=("parallel", "parallel", "arbitrary"), vmem_limit_bytes=VMEM_LIMIT),
        name="diff_attention",
    )(a, a, p, causal_bias, *lams, subln_g)


def _nsa_kernel(q_ref, ksl_ref, vsl_ref, kwn_ref, vwn_ref, kc_ref, vc_ref, gate_ref, ovt_ref, cb_ref, wb_ref,
                o_ref, qaug_ref, m_ref, acc_ref, imp_ref, kaug_ref, vsl1_ref, vwn1_ref):
    i = pl.program_id(2)
    t = q_ref.shape[0]
    seq = ksl_ref.shape[0]
    q0 = i * t
    n_sel = seq // SEL_BLOCK
    assert n_sel <= LANES - HEAD_DIM and imp_ref.shape[0] == LANES
    lane = lax.broadcasted_iota(jnp.int32, (t, LANES), 1)
    low = lane < HEAD_DIM

    @pl.when(i == 0)
    def _():
        row = lax.broadcasted_iota(jnp.int32, (seq, LANES), 0)
        ln = lax.broadcasted_iota(jnp.int32, (seq, LANES), 1)
        kaug_ref[...] = ksl_ref[...] + jnp.where(ln - HEAD_DIM == row // SEL_BLOCK, 1.0, 0.0).astype(BF16)
        ones_hi = jnp.where(ln >= HEAD_DIM, 1.0, 0.0).astype(BF16)
        vsl1_ref[...] = vsl_ref[...] + ones_hi
        vwn1_ref[...] = vwn_ref[...] + ones_hi

    q = q_ref[...].astype(F32)
    heads = []
    for pair in range(NSA_HPG // 2):
        qp = q[:, pair * LANES:(pair + 1) * LANES]
        heads += [qp, pltpu.roll(qp, HEAD_DIM, axis=1)]

    q4 = jnp.concatenate([jnp.where(low, hq, 0.0).astype(BF16) for hq in heads], axis=0)
    n_cmp = kc_ref.shape[0]
    s = _dot_nt(q4, kc_ref[...])
    r4, n_idx = _stacked_rows_cols(t, n_cmp, NSA_HPG)
    valid4 = n_idx * CMP_STRIDE + (CMP_BLOCK - 1) <= q0 + r4
    m = jnp.max(jnp.where(valid4, s, -jnp.inf), axis=1, keepdims=True)
    m = jnp.where(m == -jnp.inf, 0.0, m)
    e = jnp.where(valid4, jnp.exp(s - m), 0.0)
    p = e / jnp.maximum(jnp.sum(e, axis=1, keepdims=True), 1e-30)
    o_cmp = _dot(p.astype(BF16), vc_ref[...])

    psum = p[0:t] + p[t:2 * t] + p[2 * t:3 * t] + p[3 * t:4 * t]
    hi, lo = _split_bf16(psum)
    ovt = ovt_ref[...]
    imp = _dot_nt(ovt, hi) + _dot_nt(ovt, lo)
    jrow = lax.broadcasted_iota(jnp.int32, imp.shape, 0) - SEL_BLOCK
    qcol = q0 + lax.broadcasted_iota(jnp.int32, imp.shape, 1)
    cur = qcol // SEL_BLOCK
    forced = (jrow == 0) | (jrow == cur) | (jrow == cur - 1)
    imp = jnp.where(forced, FORCED_SCORE, imp)
    imp = jnp.where((jrow >= 0) & (jrow * SEL_BLOCK <= qcol), imp, -jnp.inf)
    imp_ref[...] = imp
    imp_b = imp[SEL_BLOCK:, :]
    jb = lax.broadcasted_iota(jnp.int32, imp_b.shape, 0)
    rank = jnp.zeros(imp_b.shape, F32)
    for kk in range(n_sel):
        rk = imp_ref[SEL_BLOCK + kk:SEL_BLOCK + kk + 1, :]
        rank = rank + jnp.where(rk > imp_b, 1.0, jnp.where((rk == imp_b) & (jb > kk), 1.0, 0.0))
    chosen = (rank < float(min(SEL_TOP, n_sel))) & (imp_b > -jnp.inf)
    bias_t = jnp.concatenate([jnp.zeros((SEL_BLOCK, t), F32), jnp.where(chosen, 0.0, NEG_BLOCK)], axis=0)
    bias = bias_t.T
    qaug_ref[...] = jnp.concatenate([jnp.where(low, hq, bias).astype(BF16) for hq in heads], axis=0)

    _init_softmax_state(m_ref, acc_ref)

    def slc_step(j, masked):
        start = pl.multiple_of(j * t, t)
        s = _dot_nt(qaug_ref[...], kaug_ref[pl.ds(start, t), :])
        if masked:
            s = _add_per_head(s, cb_ref[...], NSA_HPG)
        _online_update(s, vsl1_ref[pl.ds(start, t), :], m_ref, acc_ref, 0)

    def slc_body(j, carry):
        slc_step(j, False)
        return carry

    lax.fori_loop(0, i, slc_body, 0)
    slc_step(i, True)

    assert WINDOW % t == 0 and wb_ref.shape[1] == WINDOW + t
    kstart = pl.multiple_of(jnp.maximum(q0 - WINDOW, 0), t)
    s = _dot_nt(qaug_ref[...], kwn_ref[pl.ds(kstart, WINDOW + t), :])
    s = _add_per_head(s, wb_ref[...], NSA_HPG)
    e = jnp.exp(s - jnp.max(s, axis=1, keepdims=True))
    acc_w = _dot(e.astype(BF16), vwn1_ref[pl.ds(kstart, WINDOW + t), :])

    def normalise(acc):
        low4 = lax.broadcasted_iota(jnp.int32, acc.shape, 1) < HEAD_DIM
        return acc / jnp.where(low4, pltpu.roll(acc, HEAD_DIM, axis=1), 1.0)

    o_slc = normalise(acc_ref[0])
    o_win = normalise(acc_w)
    gate = gate_ref[...]
    outs = []
    for hh in range(NSA_HPG):
        sl = slice(hh * t, (hh + 1) * t)
        outs.append(gate[:, 3 * hh:3 * hh + 1] * o_cmp[sl] + gate[:, 3 * hh + 1:3 * hh + 2] * o_slc[sl]
                    + gate[:, 3 * hh + 2:3 * hh + 3] * o_win[sl])
    for pair in range(NSA_HPG // 2):
        o_ref[:, pair * LANES:(pair + 1) * LANES] = jnp.where(
            low, outs[2 * pair], pltpu.roll(outs[2 * pair + 1], HEAD_DIM, axis=1)).astype(o_ref.dtype)


def _nsa_attention(a, p, gates, kcmp, vcmp, ovt, causal_bias, window_bias, batch, seq, col):
    t = causal_bias.shape[0]
    nq = seq // t
    gw = NSA_HPG * HEAD_DIM
    rows = NSA_HPG * t
    n_cmp = kcmp.shape[2]
    full = lambda arr: pl.BlockSpec(arr.shape, lambda b, g, i: (0, 0))
    return pl.pallas_call(
        _nsa_kernel,
        out_shape=jax.ShapeDtypeStruct((batch * seq, NSA_HEADS * HEAD_DIM), BF16),
        grid=(batch, NSA_GROUPS, nq),
        in_specs=[
            pl.BlockSpec((t, gw), lambda b, g, i: (b * nq + i, col["nq"] // gw + g)),
            pl.BlockSpec((seq, LANES), lambda b, g, i: (b, col["ksl"] // LANES + g)),
            pl.BlockSpec((seq, LANES), lambda b, g, i: (b, col["vsl"] // LANES + g)),
            pl.BlockSpec((seq, LANES), lambda b, g, i: (b, col["kwn"] // LANES + g)),
            pl.BlockSpec((seq, LANES), lambda b, g, i: (b, col["vwn"] // LANES + g)),
            pl.BlockSpec((None, None, n_cmp, LANES), lambda b, g, i: (b, g, 0, 0)),
            pl.BlockSpec((None, None, n_cmp, LANES), lambda b, g, i: (b, g, 0, 0)),
            pl.BlockSpec((t, LANES), lambda b, g, i: (b * nq + i, col["ng"] // LANES + g)),
            full(ovt),
            full(causal_bias),
            pl.BlockSpec((None,) + window_bias.shape[1:], lambda b, g, i: (jnp.minimum(i, 1), 0, 0)),
        ],
        out_specs=pl.BlockSpec((t, gw), lambda b, g, i: (b * nq + i, g)),
        scratch_shapes=[
            pltpu.VMEM((rows, LANES), BF16),
            pltpu.VMEM((1, rows, LANES), F32),
            pltpu.VMEM((1, rows, LANES), F32),
            pltpu.VMEM((LANES, t), F32),
            pltpu.VMEM((seq, LANES), BF16),
            pltpu.VMEM((seq, LANES), BF16),
            pltpu.VMEM((seq, LANES), BF16),
        ],
        compiler_params=pltpu.CompilerParams(
            dimension_semantics=("parallel", "parallel", "arbitrary"), vmem_limit_bytes=VMEM_LIMIT),
        name="nsa_attention",
    )(a, a, p, a, p, kcmp, vcmp, gates, ovt, causal_bias, window_bias)


def _merge_kernel(yd_ref, yn_ref, g0_ref, g1_ref, x_ref, wd_ref, wn_ref, wo_ref, lg_ref, x1_ref, h_ref):
    mixed = g0_ref[...] * _dot(yd_ref[...], wd_ref[...]) + g1_ref[...] * _dot(yn_ref[...], wn_ref[...])
    x1 = x_ref[...] + _dot(mixed.astype(BF16), wo_ref[...])
    x1_ref[...] = x1
    ms = jnp.mean(x1 * x1, axis=-1, keepdims=True)
    h_ref[...] = (x1 * lax.rsqrt(ms + EPS) * lg_ref[...]).astype(h_ref.dtype)


def _merge(yd, yn, gates, x2d, wd, wn, wo, lg):
    n, d = x2d.shape
    tm = min(MERGE_TM, n)
    tok = lambda c: pl.BlockSpec((tm, d), lambda i: (i, c))
    full = lambda arr: pl.BlockSpec(arr.shape, lambda i: (0, 0))
    return pl.pallas_call(
        _merge_kernel,
        out_shape=(jax.ShapeDtypeStruct((n, d), F32), jax.ShapeDtypeStruct((n, d), BF16)),
        grid=(n // tm,),
        in_specs=[tok(0), tok(0), tok(0), tok(1), tok(0), full(wd), full(wn), full(wo), full(lg)],
        out_specs=(tok(0), tok(0)),
        compiler_params=pltpu.CompilerParams(
            dimension_semantics=("parallel",), vmem_limit_bytes=VMEM_LIMIT),
        name="merge_outproj",
    )(yd, yn, gates, gates, x2d, wd, wn, wo, lg)


def _mlp_kernel(h_ref, x1_ref, wu_ref, wd_ref, o_ref, acc_ref):
    f = pl.program_id(1)

    @pl.when(f == 0)
    def _():
        acc_ref[...] = x1_ref[...]

    up = jnp.maximum(_dot(h_ref[...], wu_ref[...]), 0.0)
    acc_ref[...] += _dot((up * up).astype(BF16), wd_ref[...])

    @pl.when(f == pl.num_programs(1) - 1)
    def _():
        o_ref[...] = acc_ref[...]


def _mlp(h, x1, wu, wd):
    n, d = x1.shape
    dff = wu.shape[1]
    tm, tf = min(MLP_TM, n), MLP_TF
    return pl.pallas_call(
        _mlp_kernel,
        out_shape=jax.ShapeDtypeStruct((n, d), F32),
        grid=(n // tm, dff // tf),
        in_specs=[
            pl.BlockSpec((tm, d), lambda i, f: (i, 0)),
            pl.BlockSpec((tm, d), lambda i, f: (i, 0)),
            pl.BlockSpec((d, tf), lambda i, f: (0, f)),
            pl.BlockSpec((tf, d), lambda i, f: (f, 0)),
        ],
        out_specs=pl.BlockSpec((tm, d), lambda i, f: (i, 0)),
        scratch_shapes=[pltpu.VMEM((tm, d), F32)],
        compiler_params=pltpu.CompilerParams(
            dimension_semantics=("parallel", "arbitrary"), vmem_limit_bytes=VMEM_LIMIT),
        name="mlp",
    )(h, x1, wu, wd)


def _rope_tables(pos):
    inv_freq = ROPE_THETA ** (-jnp.arange(HALF, dtype=F32) / HALF)
    ang = pos.astype(F32)[:, None] * inv_freq[None, :]
    c, s = jnp.cos(ang), jnp.sin(ang)
    return jnp.concatenate([c, c, c, c], axis=-1), jnp.concatenate([-s, s, -s, s], axis=-1)


def _pad_groups(w):
    d = w.shape[0]
    w = w.reshape(d, NSA_GROUPS, HEAD_DIM)
    return jnp.pad(w, ((0, 0), (0, 0), (0, LANES - HEAD_DIM))).reshape(d, NSA_GROUPS * LANES)


def _pad_cols(w, total):
    return jnp.pad(w, ((0, 0), (0, total - w.shape[1])))


def _additive_mask(valid):
    return jnp.asarray(np.where(valid, 0.0, NEG_TOKEN), F32)


def _layer(x2d, batch, seq, layer, ln_mix_g, w_in, diff_q_norm_g, diff_k_norm_g, diff_lambda_q1,
           diff_lambda_k1, diff_lambda_q2, diff_lambda_k2, diff_subln_g, nsa_q_norm_g, nsa_k_norm_g,
           cmp_pos_k, cmp_pos_v, cmp_k_w1, cmp_k_w2, cmp_v_w1, cmp_v_w2, w_proj_diff, w_proj_nsa,
           w_out, ln_mlp_g, w_mlp_up, w_mlp_down):
    d = x2d.shape[1]
    diff_qk = DIFF_HEADS * 2 * HEAD_DIM
    diff_v = DIFF_HEADS * 2 * HEAD_DIM
    nsa_q = NSA_HEADS * HEAD_DIM
    nsa_kv = NSA_GROUPS * HEAD_DIM
    splits = np.cumsum([diff_qk, diff_qk, diff_v, nsa_q] + [nsa_kv] * 6 + [NSA_HEADS * 3, 2 * d])[:-1]
    (w_dq, w_dk, w_dv, w_nq, w_kc, w_vc, w_ksl, w_vsl, w_kwn, w_vwn, w_ng, w_mg) = jnp.split(
        w_in, [int(c) for c in splits], axis=1)
    scale = HEAD_DIM ** -0.5

    w_a = jnp.concatenate([w_dq, w_dk, w_nq, _pad_groups(w_ksl), _pad_groups(w_kwn)], axis=1).astype(BF16)
    col_a = {"dq": 0, "dk": diff_qk, "nq": 2 * diff_qk, "ksl": 2 * diff_qk + nsa_q,
             "kwn": 2 * diff_qk + nsa_q + 2 * nsa_kv}
    tile_g = lambda g, reps: jnp.tile(g.astype(F32), reps)
    gain_a = jnp.concatenate([
        tile_g(diff_q_norm_g, 2 * DIFF_HEADS) * scale,
        tile_g(diff_k_norm_g, 2 * DIFF_HEADS),
        tile_g(nsa_q_norm_g, NSA_HEADS) * scale,
        tile_g(nsa_k_norm_g[1], 2 * NSA_GROUPS),
        tile_g(nsa_k_norm_g[2], 2 * NSA_GROUPS),
    ])[None, :]
    cos, sin = _rope_tables(jnp.arange(seq))
    eye_heads = np.kron(np.eye(LANES // HEAD_DIM), np.ones((HEAD_DIM, HEAD_DIM)))
    block_ones = jnp.asarray(eye_heads, BF16)
    g_mix = ln_mix_g.astype(F32)[None, :]
    a = _inproj(x2d, g_mix, w_a, "rope", BF16, seq, (gain_a, cos, sin, block_ones))

    w_p = jnp.concatenate([w_dv, _pad_groups(w_vsl), _pad_groups(w_vwn), w_kc, w_vc], axis=1).astype(BF16)
    col_p = {"dv": 0, "vsl": diff_v, "vwn": diff_v + 2 * nsa_kv, "kc": diff_v + 4 * nsa_kv,
             "vc": diff_v + 5 * nsa_kv}
    p = _inproj(x2d, g_mix, w_p, "plain", BF16, seq)

    w_ng_g = w_ng.reshape(d, NSA_GROUPS, NSA_HPG * 3)
    w_ng_g = jnp.pad(w_ng_g, ((0, 0), (0, 0), (0, LANES - NSA_HPG * 3))).reshape(d, NSA_GROUPS * LANES)
    w_g = jnp.concatenate([w_mg, w_ng_g], axis=1)
    gates_cols = -(-w_g.shape[1] // PROJ_TN) * PROJ_TN
    w_g = _pad_cols(w_g, gates_cols).astype(BF16)
    gates = _inproj(x2d, g_mix, w_g, "sigmoid", F32, seq)

    n_runs = seq // CMP_STRIDE

    def runs(c0):
        kv = p[:, c0:c0 + nsa_kv].reshape(batch, seq, NSA_GROUPS, HEAD_DIM)
        return kv.transpose(0, 2, 1, 3).reshape(batch, NSA_GROUPS, n_runs, CMP_STRIDE * HEAD_DIM)

    dup = lambda w: jnp.concatenate([w, w], axis=1).astype(BF16)
    cmp_center = jnp.arange(n_runs) * CMP_STRIDE + (CMP_BLOCK - 1) / 2.0
    cos_c, sin_c = _rope_tables(cmp_center)
    gain_c = tile_g(nsa_k_norm_g[0], 2)[None, :]
    kcmp = _compress(runs(col_p["kc"]), cmp_pos_k.astype(F32).reshape(1, -1), cmp_k_w1.astype(BF16),
                     dup(cmp_k_w2), (gain_c, cos_c, sin_c), True)
    vcmp = _compress(runs(col_p["vc"]), cmp_pos_v.astype(F32).reshape(1, -1), cmp_v_w1.astype(BF16),
                     dup(cmp_v_w2), (), False)

    lambda_init = 0.8 - 0.6 * math.exp(-0.3 * layer)
    lams = [v.astype(F32)[None, :] for v in (diff_lambda_q1, diff_lambda_k1, diff_lambda_q2, diff_lambda_k2)]
    t_att = min(ATT_T, seq)
    tri = np.arange(t_att)
    y_diff = _diff_attention(a, p, _additive_mask(tri[None, :] <= tri[:, None]), lams,
                             diff_subln_g.astype(F32)[None, :], batch, seq, lambda_init, col_a["dk"])

    n_sel = seq // SEL_BLOCK
    cmp_start = np.arange(n_runs) * CMP_STRIDE
    sel_start = np.arange(n_sel) * SEL_BLOCK
    overlap = ((cmp_start[:, None] < sel_start[None, :] + SEL_BLOCK)
               & (cmp_start[:, None] + CMP_BLOCK - 1 >= sel_start[None, :]))
    ovt = np.zeros((LANES, n_runs))
    ovt[HEAD_DIM:HEAD_DIM + n_sel] = overlap.T
    t_nsa = min(NSA_T, seq)
    r = np.arange(t_nsa)[:, None]
    c = np.arange(WINDOW + t_nsa)[None, :]
    window_bias = jnp.stack([_additive_mask((c <= r) & (r - c < WINDOW)),
                             _additive_mask((c <= r + WINDOW) & (r + WINDOW - c < WINDOW))])
    col = {"nq": col_a["nq"], "ksl": col_a["ksl"], "kwn": col_a["kwn"], "vsl": col_p["vsl"],
           "vwn": col_p["vwn"], "ng": 2 * d}
    y_nsa = _nsa_attention(a, p, gates, kcmp, vcmp, jnp.asarray(ovt, BF16),
                           _additive_mask(c[:, :t_nsa] <= r), window_bias, batch, seq, col)

    x1, h2 = _merge(y_diff, y_nsa, gates, x2d, w_proj_diff.astype(BF16), w_proj_nsa.astype(BF16),
                    w_out.astype(BF16), ln_mlp_g.astype(F32)[None, :])
    return _mlp(h2, x1, w_mlp_up.astype(BF16), w_mlp_down.astype(BF16))


def kernel(x, ln_mix_g, w_in, diff_q_norm_g, diff_k_norm_g, diff_lambda_q1, diff_lambda_k1, diff_lambda_q2, diff_lambda_k2, diff_subln_g, nsa_q_norm_g, nsa_k_norm_g, cmp_pos_k, cmp_pos_v, cmp_k_w1, cmp_k_w2, cmp_v_w1, cmp_v_w2, w_proj_diff, w_proj_nsa, w_out, ln_mlp_g, w_mlp_up, w_mlp_down):
    batch, seq, d = x.shape
    params = (ln_mix_g, w_in, diff_q_norm_g, diff_k_norm_g, diff_lambda_q1, diff_lambda_k1, diff_lambda_q2,
              diff_lambda_k2, diff_subln_g, nsa_q_norm_g, nsa_k_norm_g, cmp_pos_k, cmp_pos_v, cmp_k_w1,
              cmp_k_w2, cmp_v_w1, cmp_v_w2, w_proj_diff, w_proj_nsa, w_out, ln_mlp_g, w_mlp_up, w_mlp_down)
    x2d = x.reshape(batch * seq, d)
    for layer in range(ln_mix_g.shape[0]):
        x2d = _layer(x2d, batch, seq, layer, *[prm[layer] for prm in params])
    return x2d.reshape(batch, seq, d)
```

```python
import functools
import math

import numpy as np
import jax
import jax.numpy as jnp
from jax import lax
from jax.experimental import pallas as pl
from jax.experimental.pallas import tpu as pltpu

F32 = jnp.float32
BF16 = jnp.bfloat16

LANES = 128
MXU_WIDTH = 256
HEAD_DIM = 64
HALF = HEAD_DIM // 2
DIFF_HEADS = 8
NSA_HEADS = 16
NSA_GROUPS = 4
NSA_HPG = NSA_HEADS // NSA_GROUPS
CMP_BLOCK = 32
CMP_STRIDE = 16
SEL_BLOCK = 64
SEL_TOP = 16
WINDOW = 512
FORCED_SCORE = 1e4
ROPE_THETA = 10000.0
EPS = 1e-6
NEG_BLOCK = -1e9
NEG_TOKEN = -1e30

PROJ_TM = 1024
PROJ_TN = 512
ATT_T = 512
NSA_T = 512
WIN_T = 256
MERGE_TM = 512
MLP_TM = 1024
MLP_TF = 1024
VMEM_LIMIT = 56 * 1024 * 1024

_NT = (((1,), (1,)), ((), ()))


def _dot(a, b):
    return jnp.dot(a, b, preferred_element_type=F32)


def _dot_nt(a, b):
    return lax.dot_general(a, b, _NT, preferred_element_type=F32)


def _split_bf16(x):
    hi = x.astype(BF16)
    lo = (x - hi.astype(F32)).astype(BF16)
    return hi, lo


def _swap_halves_within_heads(y):
    lane = lax.broadcasted_iota(jnp.int32, y.shape, 1)
    first_half = (lane % HEAD_DIM) < HALF
    return jnp.where(first_half, pltpu.roll(y, LANES - HALF, axis=1), pltpu.roll(y, HALF, axis=1))


def _rope(y, cos, sin_signed):
    return y * cos + _swap_halves_within_heads(y) * sin_signed


def _inproj_kernel(x_ref, g_ref, w_ref, gain_ref, cos_ref, sin_ref, bd_ref, ab_ref, gt_ref, h_scr,
                   *, n_rope, n_plain):
    j = pl.program_id(1)

    @pl.when(j == 0)
    def _():
        x = x_ref[...]
        ms = jnp.mean(x * x, axis=-1, keepdims=True)
        h_scr[...] = (x * lax.rsqrt(ms + EPS) * g_ref[...]).astype(BF16)

    acc = _dot(h_scr[...], w_ref[...])

    @pl.when(j < n_rope)
    def _():
        cos = cos_ref[...]
        sin = sin_ref[...]
        wide = bd_ref.shape[0]
        for c in range(acc.shape[1] // wide):
            xc = acc[:, c * wide:(c + 1) * wide]
            ms = _dot((xc * xc).astype(BF16), bd_ref[...])
            y = xc * lax.rsqrt(ms + EPS) * gain_ref[:, c * wide:(c + 1) * wide]
            for k in range(wide // LANES):
                lo = c * wide + k * LANES
                ab_ref[:, lo:lo + LANES] = _rope(y[:, k * LANES:(k + 1) * LANES], cos, sin).astype(ab_ref.dtype)

    @pl.when((j >= n_rope) & (j < n_rope + n_plain))
    def _():
        ab_ref[...] = acc.astype(ab_ref.dtype)

    @pl.when(j >= n_rope + n_plain)
    def _():
        gt_ref[...] = jax.nn.sigmoid(acc)


def _inproj(x2d, g, w, gain, cos, sin, bd, seq, n_rope, n_plain, n_gate):
    n, d = x2d.shape
    tm, tn = min(PROJ_TM, seq), PROJ_TN
    assert n % tm == 0 and seq % tm == 0 and w.shape[1] == (n_rope + n_plain + n_gate) * tn
    pos_blocks = seq // tm
    n_ab = n_rope + n_plain
    return pl.pallas_call(
        functools.partial(_inproj_kernel, n_rope=n_rope, n_plain=n_plain),
        out_shape=(jax.ShapeDtypeStruct((n, n_ab * tn), BF16), jax.ShapeDtypeStruct((n, n_gate * tn), F32)),
        grid=(n // tm, n_ab + n_gate),
        in_specs=[
            pl.BlockSpec((tm, d), lambda i, j: (i, 0)),
            pl.BlockSpec((1, d), lambda i, j: (0, 0)),
            pl.BlockSpec((d, tn), lambda i, j: (0, j)),
            pl.BlockSpec((1, tn), lambda i, j: (0, jnp.minimum(j, n_rope - 1))),
            pl.BlockSpec((tm, LANES), lambda i, j: (i % pos_blocks, 0)),
            pl.BlockSpec((tm, LANES), lambda i, j: (i % pos_blocks, 0)),
            pl.BlockSpec(bd.shape, lambda i, j: (0, 0)),
        ],
        out_specs=(
            pl.BlockSpec((tm, tn), lambda i, j: (i, jnp.minimum(j, n_ab - 1))),
            pl.BlockSpec((tm, tn), lambda i, j: (i, jnp.maximum(j - n_ab, 0))),
        ),
        scratch_shapes=[pltpu.VMEM((tm, d), BF16)],
        compiler_params=pltpu.CompilerParams(
            dimension_semantics=("parallel", "arbitrary"), vmem_limit_bytes=VMEM_LIMIT),
        name="inproj",
    )(x2d, g, w, gain, cos, sin, bd)


def _compress_kernel(r_ref, pos_ref, w1_ref, w2_ref, *rest, is_key):
    if is_key:
        gain_ref, cos_ref, sin_ref, o_ref = rest
    else:
        (o_ref,) = rest
    half = r_ref.shape[1]
    r = r_ref[...].astype(F32)
    top = _dot((r + pos_ref[:, :half]).astype(BF16), w1_ref[:half, :])
    bot = _dot((r + pos_ref[:, half:]).astype(BF16), w1_ref[half:, :])
    hidden = top + pltpu.roll(bot, bot.shape[0] - 1, axis=0)
    out = _dot(jax.nn.gelu(hidden).astype(BF16), w2_ref[...])
    if is_key:
        ms = jnp.mean(out * out, axis=-1, keepdims=True)
        out = _rope(out * lax.rsqrt(ms + EPS) * gain_ref[...], cos_ref[...], sin_ref[...])
    o_ref[...] = out.astype(o_ref.dtype)


def _compress(r, pos, w1, w2dup, extra, is_key):
    b, g, rows, half = r.shape
    in_specs = [
        pl.BlockSpec((None, None, rows, half), lambda i, j: (i, j, 0, 0)),
        pl.BlockSpec(pos.shape, lambda i, j: (0, 0)),
        pl.BlockSpec(w1.shape, lambda i, j: (0, 0)),
        pl.BlockSpec(w2dup.shape, lambda i, j: (0, 0)),
    ] + [pl.BlockSpec(e.shape, lambda i, j: (0, 0)) for e in extra]
    return pl.pallas_call(
        functools.partial(_compress_kernel, is_key=is_key),
        out_shape=jax.ShapeDtypeStruct((b, g, rows, LANES), BF16),
        grid=(b, g),
        in_specs=in_specs,
        out_specs=pl.BlockSpec((None, None, rows, LANES), lambda i, j: (i, j, 0, 0)),
        compiler_params=pltpu.CompilerParams(
            dimension_semantics=("parallel", "parallel"), vmem_limit_bytes=VMEM_LIMIT),
        name="compress_k" if is_key else "compress_v",
    )(r, pos, w1, w2dup, *extra)


def _online_update(s, v_ones, m_ref, acc_ref, idx):
    m_prev = m_ref[idx]
    m_next = jnp.maximum(m_prev, jnp.max(s, axis=1, keepdims=True))
    alpha = jnp.exp(m_prev - m_next)
    p = jnp.exp(s - jnp.concatenate([m_next] * (s.shape[1] // LANES), axis=1))
    m_ref[idx] = m_next
    scale = jnp.concatenate([alpha] * (acc_ref.shape[-1] // LANES), axis=1)
    acc_ref[idx] = acc_ref[idx] * scale + _dot(p.astype(BF16), v_ones)


def _init_softmax_state(m_ref, acc_ref):
    m_ref[...] = jnp.full(m_ref.shape, -jnp.inf, F32)
    acc_ref[...] = jnp.zeros(acc_ref.shape, F32)


def _add_per_head(s, bias, heads):
    rows, cols = bias.shape
    return (s.reshape(heads, rows, cols) + bias[None]).reshape(heads * rows, cols)


def _diff_attn_kernel(q_ref, k_ref, v_ref, cb_ref, lq1_ref, lk1_ref, lq2_ref, lk2_ref, sg_ref, o_ref,
                      m_ref, acc_ref, vones_ref, s_ref, *, lambda_init):
    i = pl.program_id(2)
    t = q_ref.shape[0]
    dv = v_ref.shape[1]

    @pl.when(i == 0)
    def _():
        vones_ref[:, :dv] = v_ref[...]
        vones_ref[:, dv:] = jnp.ones((v_ref.shape[0], vones_ref.shape[1] - dv), BF16)

    q = q_ref[...]
    lane = lax.broadcasted_iota(jnp.int32, q.shape, 1)
    zero = jnp.zeros_like(q)
    qc = (jnp.where(lane < HEAD_DIM, q, zero), jnp.where(lane >= HEAD_DIM, q, zero))
    _init_softmax_state(m_ref, acc_ref)

    def scores(j, slot):
        k = k_ref[pl.ds(pl.multiple_of(j * t, t), t), :]
        for c in range(2):
            s_ref[slot, c] = _dot_nt(qc[c], k)

    def update(j, slot, masked):
        v = vones_ref[pl.ds(pl.multiple_of(j * t, t), t), :]
        for c in range(2):
            s = s_ref[slot, c]
            if masked:
                s = s + cb_ref[...]
            _online_update(s, v, m_ref, acc_ref, c)

    scores(0, 0)

    def body(pp, carry):
        scores(2 * pp + 1, 1)
        update(2 * pp, 0, False)
        scores(2 * pp + 2, 0)
        update(2 * pp + 1, 1, False)
        return carry

    lax.fori_loop(0, i // 2, body, 0)

    @pl.when(i % 2 == 0)
    def _():
        update(i, 0, True)

    @pl.when(i % 2 == 1)
    def _():
        scores(i, 1)
        update(i - 1, 0, False)
        update(i, 1, True)

    lam = (jnp.exp(jnp.sum(lq1_ref[...] * lk1_ref[...], axis=-1, keepdims=True))
           - jnp.exp(jnp.sum(lq2_ref[...] * lk2_ref[...], axis=-1, keepdims=True)) + lambda_init)
    o = acc_ref[0, :, :dv] / acc_ref[0, :, dv:] - lam * (acc_ref[1, :, :dv] / acc_ref[1, :, dv:])
    ms = jnp.mean(o * o, axis=-1, keepdims=True)
    o_ref[...] = (o * lax.rsqrt(ms + EPS) * sg_ref[...] * (1.0 - lambda_init)).astype(o_ref.dtype)


def _diff_attention(ab, causal_bias, lams, subln_g, batch, seq, lambda_init, col):
    t = causal_bias.shape[0]
    nq = seq // t
    dv = 2 * HEAD_DIM
    small = lambda arr: pl.BlockSpec(arr.shape, lambda b, h, i: (0, 0))
    return pl.pallas_call(
        functools.partial(_diff_attn_kernel, lambda_init=lambda_init),
        out_shape=jax.ShapeDtypeStruct((batch * seq, DIFF_HEADS * dv), BF16),
        grid=(batch, DIFF_HEADS, nq),
        in_specs=[
            pl.BlockSpec((t, LANES), lambda b, h, i: (b * nq + i, col["dq"] // LANES + h)),
            pl.BlockSpec((seq, LANES), lambda b, h, i: (b, col["dk"] // LANES + h)),
            pl.BlockSpec((seq, dv), lambda b, h, i: (b, col["dv"] // dv + h)),
            small(causal_bias),
            small(lams[0]), small(lams[1]), small(lams[2]), small(lams[3]), small(subln_g),
        ],
        out_specs=pl.BlockSpec((t, dv), lambda b, h, i: (b * nq + i, h)),
        scratch_shapes=[
            pltpu.VMEM((2, t, LANES), F32),
            pltpu.VMEM((2, t, 2 * dv), F32),
            pltpu.VMEM((seq, 2 * dv), BF16),
            pltpu.VMEM((2, 2, t, t), F32),
        ],
        compiler_params=pltpu.CompilerParams(
            dimension_semantics=("parallel", "parallel", "arbitrary"), vmem_limit_bytes=VMEM_LIMIT),
        name="diff_attention",
    )(ab, ab, ab, causal_bias, *lams, subln_g)


def _nsa_kernel(q_ref, ksl_ref, vsl_ref, kwn_ref, vwn_ref, kc_ref, vc_ref, gate_ref, ovt_ref, cmpb_ref, cb_ref,
                wb_ref, o_ref, qaug_ref, m_ref, acc_ref, accw_ref, imp_ref, kaug_ref, vsl1_ref, vwn1_ref):
    i = pl.program_id(2)
    t = q_ref.shape[0]
    seq = ksl_ref.shape[0]
    q0 = i * t
    n_sel = seq // SEL_BLOCK
    assert n_sel <= LANES - HEAD_DIM and imp_ref.shape[0] == LANES
    lane = lax.broadcasted_iota(jnp.int32, (t, LANES), 1)
    low = lane < HEAD_DIM

    @pl.when(i == 0)
    def _():
        row = lax.broadcasted_iota(jnp.int32, (seq, LANES), 0)
        ln = lax.broadcasted_iota(jnp.int32, (seq, LANES), 1)
        kaug_ref[...] = ksl_ref[...] + jnp.where(ln - HEAD_DIM == row // SEL_BLOCK, 1.0, 0.0).astype(BF16)
        ones_hi = jnp.where(ln >= HEAD_DIM, 1.0, 0.0).astype(BF16)
        vsl1_ref[...] = vsl_ref[...] + ones_hi
        vwn1_ref[...] = vwn_ref[...] + ones_hi

    q = q_ref[...].astype(F32)
    heads = []
    for pair in range(NSA_HPG // 2):
        qp = q[:, pair * LANES:(pair + 1) * LANES]
        heads += [qp, pltpu.roll(qp, HEAD_DIM, axis=1)]

    q4 = jnp.concatenate([jnp.where(low, hq, 0.0).astype(BF16) for hq in heads], axis=0)
    s = _add_per_head(_dot_nt(q4, kc_ref[...]), cmpb_ref[...], NSA_HPG)
    e = jnp.exp(s - jnp.max(s, axis=1, keepdims=True))
    p = e / jnp.sum(e, axis=1, keepdims=True)
    o_cmp = _dot(p.astype(BF16), vc_ref[...])

    psum = p[0:t] + p[t:2 * t] + p[2 * t:3 * t] + p[3 * t:4 * t]
    hi, lo = _split_bf16(psum)
    ovt = ovt_ref[...]
    imp = _dot_nt(ovt, hi) + _dot_nt(ovt, lo)
    jrow = lax.broadcasted_iota(jnp.int32, imp.shape, 0) - SEL_BLOCK
    qcol = q0 + lax.broadcasted_iota(jnp.int32, imp.shape, 1)
    cur = qcol // SEL_BLOCK
    forced = (jrow == 0) | (jrow == cur) | (jrow == cur - 1)
    imp = jnp.where(forced, FORCED_SCORE, imp)
    imp = jnp.where((jrow >= 0) & (jrow * SEL_BLOCK <= qcol), imp, -jnp.inf)
    imp_ref[...] = imp
    imp_b = imp[SEL_BLOCK:, :]
    jb = lax.broadcasted_iota(jnp.int32, imp_b.shape, 0)
    rank = jnp.zeros(imp_b.shape, F32)
    for kk in range(n_sel):
        rk = imp_ref[SEL_BLOCK + kk:SEL_BLOCK + kk + 1, :]
        rank = rank + jnp.where(rk > imp_b, 1.0, jnp.where((rk == imp_b) & (jb > kk), 1.0, 0.0))
    chosen = (rank < float(min(SEL_TOP, n_sel))) & (imp_b > -jnp.inf)
    bias_t = jnp.concatenate([jnp.zeros((SEL_BLOCK, t), F32), jnp.where(chosen, 0.0, NEG_BLOCK)], axis=0)
    bias = bias_t.T
    for hh, hq in enumerate(heads):
        qaug_ref[hh] = jnp.where(low, hq, bias).astype(BF16)

    _init_softmax_state(m_ref, acc_ref)
    rows = NSA_HPG * t

    def slc_step(j, masked):
        start = pl.multiple_of(j * t, t)
        s = _dot_nt(qaug_ref[...].reshape(rows, LANES), kaug_ref[pl.ds(start, t), :])
        if masked:
            s = _add_per_head(s, cb_ref[...], NSA_HPG)
        _online_update(s, vsl1_ref[pl.ds(start, t), :], m_ref, acc_ref, 0)

    def slc_body(j, carry):
        slc_step(j, False)
        return carry

    lax.fori_loop(0, i, slc_body, 0)
    slc_step(i, True)

    tw = wb_ref.shape[1]
    wk = wb_ref.shape[2]
    assert wk == WINDOW + tw and WINDOW % tw == 0 and t % tw == 0
    for sub in range(t // tw):
        qs = qaug_ref[:, sub * tw:(sub + 1) * tw, :].reshape(NSA_HPG * tw, LANES)
        sub_idx = i * (t // tw) + sub
        kstart = pl.multiple_of(jnp.maximum(q0 + sub * tw - WINDOW, 0), tw)
        s = _dot_nt(qs, kwn_ref[pl.ds(kstart, wk), :])
        s = _add_per_head(s, wb_ref[jnp.minimum(sub_idx, WINDOW // tw)], NSA_HPG)
        e = jnp.exp(s - jnp.max(s, axis=1, keepdims=True))
        acc_w = _dot(e.astype(BF16), vwn1_ref[pl.ds(kstart, wk), :])
        accw_ref[:, sub * tw:(sub + 1) * tw, :] = acc_w.reshape(NSA_HPG, tw, LANES)

    denom = jnp.ones((t, LANES), F32)
    for hh in range(NSA_HPG):
        denom = jnp.where(lane == HEAD_DIM + 3 * hh + 1, acc_ref[0, pl.ds(hh * t, t), :],
                          jnp.where(lane == HEAD_DIM + 3 * hh + 2, accw_ref[hh], denom))
    coef = gate_ref[...] / denom
    has_cmp = q0 + lax.broadcasted_iota(jnp.int32, (t, 1), 0) >= CMP_BLOCK - 1
    outs = []
    for hh in range(NSA_HPG):
        c0 = HEAD_DIM + 3 * hh
        outs.append(jnp.where(has_cmp, coef[:, c0:c0 + 1], 0.0) * o_cmp[hh * t:(hh + 1) * t]
                    + coef[:, c0 + 1:c0 + 2] * acc_ref[0, pl.ds(hh * t, t), :]
                    + coef[:, c0 + 2:c0 + 3] * accw_ref[hh])
    for pair in range(NSA_HPG // 2):
        o_ref[:, pair * LANES:(pair + 1) * LANES] = jnp.where(
            low, outs[2 * pair], pltpu.roll(outs[2 * pair + 1], HEAD_DIM, axis=1)).astype(o_ref.dtype)


def _nsa_attention(ab, gates, kcmp, vcmp, ovt, cmp_bias, causal_bias, window_bias, batch, seq, col, col_gate):
    t = causal_bias.shape[0]
    nq = seq // t
    gw = NSA_HPG * HEAD_DIM
    rows = NSA_HPG * t
    n_cmp = kcmp.shape[2]
    full = lambda arr: pl.BlockSpec(arr.shape, lambda b, g, i: (0,) * arr.ndim)
    kv = lambda name: pl.BlockSpec((seq, LANES), lambda b, g, i: (b, col[name] // LANES + g))
    return pl.pallas_call(
        _nsa_kernel,
        out_shape=jax.ShapeDtypeStruct((batch * seq, NSA_HEADS * HEAD_DIM), BF16),
        grid=(batch, NSA_GROUPS, nq),
        in_specs=[
            pl.BlockSpec((t, gw), lambda b, g, i: (b * nq + i, col["nq"] // gw + g)),
            kv("ksl"), kv("vsl"), kv("kwn"), kv("vwn"),
            pl.BlockSpec((None, None, n_cmp, LANES), lambda b, g, i: (b, g, 0, 0)),
            pl.BlockSpec((None, None, n_cmp, LANES), lambda b, g, i: (b, g, 0, 0)),
            pl.BlockSpec((t, LANES), lambda b, g, i: (b * nq + i, col_gate // LANES + g)),
            full(ovt),
            pl.BlockSpec((t, n_cmp), lambda b, g, i: (i, 0)),
            full(causal_bias),
            full(window_bias),
        ],
        out_specs=pl.BlockSpec((t, gw), lambda b, g, i: (b * nq + i, g)),
        scratch_shapes=[
            pltpu.VMEM((NSA_HPG, t, LANES), BF16),
            pltpu.VMEM((1, rows, LANES), F32),
            pltpu.VMEM((1, rows, LANES), F32),
            pltpu.VMEM((NSA_HPG, t, LANES), F32),
            pltpu.VMEM((LANES, t), F32),
            pltpu.VMEM((seq, LANES), BF16),
            pltpu.VMEM((seq, LANES), BF16),
            pltpu.VMEM((seq, LANES), BF16),
        ],
        compiler_params=pltpu.CompilerParams(
            dimension_semantics=("parallel", "parallel", "arbitrary"), vmem_limit_bytes=VMEM_LIMIT),
        name="nsa_attention",
    )(ab, ab, ab, ab, ab, kcmp, vcmp, gates, ovt, cmp_bias, causal_bias, window_bias)


def _merge_kernel(yd_ref, yn_ref, g0_ref, g1_ref, x_ref, wd_ref, wn_ref, wo_ref, lg_ref, x1_ref, h_ref):
    mixed = g0_ref[...] * _dot(yd_ref[...], wd_ref[...]) + g1_ref[...] * _dot(yn_ref[...], wn_ref[...])
    x1 = x_ref[...] + _dot(mixed.astype(BF16), wo_ref[...])
    x1_ref[...] = x1
    ms = jnp.mean(x1 * x1, axis=-1, keepdims=True)
    h_ref[...] = (x1 * lax.rsqrt(ms + EPS) * lg_ref[...]).astype(h_ref.dtype)


def _merge(yd, yn, gates, x2d, wd, wn, wo, lg):
    n, d = x2d.shape
    tm = min(MERGE_TM, n)
    tok = lambda c: pl.BlockSpec((tm, d), lambda i: (i, c))
    full = lambda arr: pl.BlockSpec(arr.shape, lambda i: (0, 0))
    return pl.pallas_call(
        _merge_kernel,
        out_shape=(jax.ShapeDtypeStruct((n, d), F32), jax.ShapeDtypeStruct((n, d), BF16)),
        grid=(n // tm,),
        in_specs=[tok(0), tok(0), tok(0), tok(1), tok(0), full(wd), full(wn), full(wo), full(lg)],
        out_specs=(tok(0), tok(0)),
        compiler_params=pltpu.CompilerParams(
            dimension_semantics=("parallel",), vmem_limit_bytes=VMEM_LIMIT),
        name="merge_outproj",
    )(yd, yn, gates, gates, x2d, wd, wn, wo, lg)


def _mlp_kernel(h_ref, x1_ref, wu_ref, wd_ref, o_ref, acc_ref):
    f = pl.program_id(1)

    @pl.when(f == 0)
    def _():
        acc_ref[...] = x1_ref[...]

    up = jnp.maximum(_dot(h_ref[...], wu_ref[...]), 0.0)
    acc_ref[...] += _dot((up * up).astype(BF16), wd_ref[...])

    @pl.when(f == pl.num_programs(1) - 1)
    def _():
        o_ref[...] = acc_ref[...]


def _mlp(h, x1, wu, wd):
    n, d = x1.shape
    dff = wu.shape[1]
    tm, tf = min(MLP_TM, n), MLP_TF
    return pl.pallas_call(
        _mlp_kernel,
        out_shape=jax.ShapeDtypeStruct((n, d), F32),
        grid=(n // tm, dff // tf),
        in_specs=[
            pl.BlockSpec((tm, d), lambda i, f: (i, 0)),
            pl.BlockSpec((tm, d), lambda i, f: (i, 0)),
            pl.BlockSpec((d, tf), lambda i, f: (0, f)),
            pl.BlockSpec((tf, d), lambda i, f: (f, 0)),
        ],
        out_specs=pl.BlockSpec((tm, d), lambda i, f: (i, 0)),
        scratch_shapes=[pltpu.VMEM((tm, d), F32)],
        compiler_params=pltpu.CompilerParams(
            dimension_semantics=("parallel", "arbitrary"), vmem_limit_bytes=VMEM_LIMIT),
        name="mlp",
    )(h, x1, wu, wd)


def _rope_tables(pos):
    inv_freq = ROPE_THETA ** (-jnp.arange(HALF, dtype=F32) / HALF)
    ang = pos.astype(F32)[:, None] * inv_freq[None, :]
    c, s = jnp.cos(ang), jnp.sin(ang)
    return jnp.concatenate([c, c, c, c], axis=-1), jnp.concatenate([-s, s, -s, s], axis=-1)


def _pad_groups(w, width, offset=0):
    d = w.shape[0]
    w = w.reshape(d, NSA_GROUPS, width)
    return jnp.pad(w, ((0, 0), (0, 0), (offset, LANES - width - offset))).reshape(d, NSA_GROUPS * LANES)


def _additive_mask(valid):
    return jnp.asarray(np.where(valid, 0.0, NEG_TOKEN), F32)


def _layer(x2d, batch, seq, layer, ln_mix_g, w_in, diff_q_norm_g, diff_k_norm_g, diff_lambda_q1,
           diff_lambda_k1, diff_lambda_q2, diff_lambda_k2, diff_subln_g, nsa_q_norm_g, nsa_k_norm_g,
           cmp_pos_k, cmp_pos_v, cmp_k_w1, cmp_k_w2, cmp_v_w1, cmp_v_w2, w_proj_diff, w_proj_nsa,
           w_out, ln_mlp_g, w_mlp_up, w_mlp_down):
    d = x2d.shape[1]
    diff_qk = DIFF_HEADS * 2 * HEAD_DIM
    diff_v = DIFF_HEADS * 2 * HEAD_DIM
    nsa_q = NSA_HEADS * HEAD_DIM
    nsa_kv = NSA_GROUPS * HEAD_DIM
    splits = np.cumsum([diff_qk, diff_qk, diff_v, nsa_q] + [nsa_kv] * 6 + [NSA_HEADS * 3, 2 * d])[:-1]
    (w_dq, w_dk, w_dv, w_nq, w_kc, w_vc, w_ksl, w_vsl, w_kwn, w_vwn, w_ng, w_mg) = jnp.split(
        w_in, [int(c) for c in splits], axis=1)
    scale = HEAD_DIM ** -0.5
    pad_kv = lambda w: _pad_groups(w, HEAD_DIM)

    fam_rope = [("dq", w_dq), ("dk", w_dk), ("nq", w_nq), ("ksl", pad_kv(w_ksl)), ("kwn", pad_kv(w_kwn))]
    fam_plain = [("dv", w_dv), ("vsl", pad_kv(w_vsl)), ("vwn", pad_kv(w_vwn)), ("kc", w_kc), ("vc", w_vc)]
    fam_gate = [("mg", w_mg), ("ng", _pad_groups(w_ng, NSA_HPG * 3, HEAD_DIM))]
    col, blocks, tiles = {}, [], []
    for fam in (fam_rope, fam_plain, fam_gate):
        start = sum(b.shape[1] for b in blocks)
        for name, w in fam:
            col[name] = sum(b.shape[1] for b in blocks)
            blocks.append(w)
        width = sum(b.shape[1] for b in blocks) - start
        pad = -width % PROJ_TN
        if pad:
            blocks.append(jnp.zeros((d, pad), w_in.dtype))
        tiles.append((width + pad) // PROJ_TN)
    w_all = jnp.concatenate(blocks, axis=1).astype(BF16)
    gate_base = (tiles[0] + tiles[1]) * PROJ_TN

    tile_g = lambda g, reps: jnp.tile(g.astype(F32), reps)
    gain = jnp.concatenate([
        tile_g(diff_q_norm_g, 2 * DIFF_HEADS) * scale,
        tile_g(diff_k_norm_g, 2 * DIFF_HEADS),
        tile_g(nsa_q_norm_g, NSA_HEADS) * scale,
        tile_g(nsa_k_norm_g[1], 2 * NSA_GROUPS),
        tile_g(nsa_k_norm_g[2], 2 * NSA_GROUPS),
    ])[None, :]
    assert gain.shape[1] == tiles[0] * PROJ_TN
    cos, sin = _rope_tables(jnp.arange(seq))
    mean_heads = np.kron(np.eye(MXU_WIDTH // HEAD_DIM), np.full((HEAD_DIM, HEAD_DIM), 1.0 / HEAD_DIM))
    ab, gates = _inproj(x2d, ln_mix_g.astype(F32)[None, :], w_all, gain, cos, sin,
                        jnp.asarray(mean_heads, BF16), seq, *tiles)

    n_runs = seq // CMP_STRIDE

    def runs(c0):
        kv = ab[:, c0:c0 + nsa_kv].reshape(batch, seq, NSA_GROUPS, HEAD_DIM)
        return kv.transpose(0, 2, 1, 3).reshape(batch, NSA_GROUPS, n_runs, CMP_STRIDE * HEAD_DIM)

    dup = lambda w: jnp.concatenate([w, w], axis=1).astype(BF16)
    cmp_center = jnp.arange(n_runs) * CMP_STRIDE + (CMP_BLOCK - 1) / 2.0
    cos_c, sin_c = _rope_tables(cmp_center)
    gain_c = tile_g(nsa_k_norm_g[0], 2)[None, :]
    kcmp = _compress(runs(col["kc"]), cmp_pos_k.astype(F32).reshape(1, -1), cmp_k_w1.astype(BF16),
                     dup(cmp_k_w2), (gain_c, cos_c, sin_c), True)
    vcmp = _compress(runs(col["vc"]), cmp_pos_v.astype(F32).reshape(1, -1), cmp_v_w1.astype(BF16),
                     dup(cmp_v_w2), (), False)

    lambda_init = 0.8 - 0.6 * math.exp(-0.3 * layer)
    lams = [v.astype(F32)[None, :] for v in (diff_lambda_q1, diff_lambda_k1, diff_lambda_q2, diff_lambda_k2)]
    t_att = min(ATT_T, seq)
    tri = np.arange(t_att)
    y_diff = _diff_attention(ab, _additive_mask(tri[None, :] <= tri[:, None]), lams,
                             diff_subln_g.astype(F32)[None, :], batch, seq, lambda_init, col)

    n_sel = seq // SEL_BLOCK
    cmp_start = np.arange(n_runs) * CMP_STRIDE
    sel_start = np.arange(n_sel) * SEL_BLOCK
    overlap = ((cmp_start[:, None] < sel_start[None, :] + SEL_BLOCK)
               & (cmp_start[:, None] + CMP_BLOCK - 1 >= sel_start[None, :]))
    ovt = np.zeros((LANES, n_runs))
    ovt[HEAD_DIM:HEAD_DIM + n_sel] = overlap.T
    cmp_bias = _additive_mask(cmp_start[None, :] + CMP_BLOCK - 1 <= np.arange(seq)[:, None])
    t_nsa = min(NSA_T, seq)
    tri = np.arange(t_nsa)
    t_win = min(WIN_T, t_nsa)
    r = np.arange(t_win)[:, None]
    c = np.arange(WINDOW + t_win)[None, :]
    window_bias = jnp.stack([_additive_mask((c <= r + off) & (r + off - c < WINDOW))
                             for off in range(0, WINDOW + 1, t_win)])
    y_nsa = _nsa_attention(ab, gates, kcmp, vcmp, jnp.asarray(ovt, BF16), cmp_bias,
                           _additive_mask(tri[None, :] <= tri[:, None]), window_bias, batch, seq, col,
                           col["ng"] - gate_base)

    assert col["mg"] == gate_base
    x1, h2 = _merge(y_diff, y_nsa, gates, x2d, w_proj_diff.astype(BF16), w_proj_nsa.astype(BF16),
                    w_out.astype(BF16), ln_mlp_g.astype(F32)[None, :])
    return _mlp(h2, x1, w_mlp_up.astype(BF16), w_mlp_down.astype(BF16))


def kernel(x, ln_mix_g, w_in, diff_q_norm_g, diff_k_norm_g, diff_lambda_q1, diff_lambda_k1, diff_lambda_q2, diff_lambda_k2, diff_subln_g, nsa_q_norm_g, nsa_k_norm_g, cmp_pos_k, cmp_pos_v, cmp_k_w1, cmp_k_w2, cmp_v_w1, cmp_v_w2, w_proj_diff, w_proj_nsa, w_out, ln_mlp_g, w_mlp_up, w_mlp_down):
    batch, seq, d = x.shape
    params = (ln_mix_g, w_in, diff_q_norm_g, diff_k_norm_g, diff_lambda_q1, diff_lambda_k1, diff_lambda_q2,
              diff_lambda_k2, diff_subln_g, nsa_q_norm_g, nsa_k_norm_g, cmp_pos_k, cmp_pos_v, cmp_k_w1,
              cmp_k_w2, cmp_v_w1, cmp_v_w2, w_proj_diff, w_proj_nsa, w_out, ln_mlp_g, w_mlp_up, w_mlp_down)
    x2d = x.reshape(batch * seq, d)
    for layer in range(ln_mix_g.shape[0]):
        x2d = _layer(x2d, batch, seq, layer, *[prm[layer] for prm in params])
    return x2d.reshape(batch, seq, d)
```

```python
import functools
import math

import numpy as np
import jax
import jax.numpy as jnp
from jax import lax
from jax.experimental import pallas as pl
from jax.experimental.pallas import tpu as pltpu

F32 = jnp.float32
BF16 = jnp.bfloat16

LANES = 128
MXU_WIDTH = 256
HEAD_DIM = 64
HALF = HEAD_DIM // 2
DIFF_HEADS = 8
NSA_HEADS = 16
NSA_GROUPS = 4
NSA_HPG = NSA_HEADS // NSA_GROUPS
CMP_BLOCK = 32
CMP_STRIDE = 16
SEL_BLOCK = 64
SEL_TOP = 16
WINDOW = 512
FORCED_SCORE = 1e4
ROPE_THETA = 10000.0
EPS = 1e-6
NEG_BLOCK = -1e9
NEG_TOKEN = -1e30

PROJ_TM = 1024
PROJ_TN = 512
ATT_T = 512
NSA_T = 512
WIN_T = 256
MERGE_TM = 512
MLP_TM = 1024
MLP_TF = 1024
VMEM_LIMIT = 56 * 1024 * 1024

_NT = (((1,), (1,)), ((), ()))


def _dot(a, b):
    return jnp.dot(a, b, preferred_element_type=F32)


def _dot_nt(a, b):
    return lax.dot_general(a, b, _NT, preferred_element_type=F32)


def _split_bf16(x):
    hi = x.astype(BF16)
    lo = (x - hi.astype(F32)).astype(BF16)
    return hi, lo


def _swap_halves_within_heads(y):
    lane = lax.broadcasted_iota(jnp.int32, y.shape, 1)
    first_half = (lane % HEAD_DIM) < HALF
    return jnp.where(first_half, pltpu.roll(y, LANES - HALF, axis=1), pltpu.roll(y, HALF, axis=1))


def _rope(y, cos, sin_signed):
    return y * cos + _swap_halves_within_heads(y) * sin_signed


def _inproj_kernel(x_ref, g_ref, w_ref, gain_ref, cos_ref, sin_ref, bd_ref, ab_ref, gt_ref, h_scr,
                   *, n_rope, n_plain):
    j = pl.program_id(1)

    @pl.when(j == 0)
    def _():
        x = x_ref[...]
        ms = jnp.mean(x * x, axis=-1, keepdims=True)
        h_scr[...] = (x * lax.rsqrt(ms + EPS) * g_ref[...]).astype(BF16)

    wide = bd_ref.shape[0]
    chunks = [slice(c * wide, (c + 1) * wide) for c in range(w_ref.shape[1] // wide)]

    @pl.when(j < n_rope)
    def _():
        cos = cos_ref[...]
        sin = sin_ref[...]
        acc = _dot(h_scr[...], w_ref[...])
        for sl in chunks:
            xc = acc[:, sl]
            ms = _dot((xc * xc).astype(BF16), bd_ref[...])
            y = xc * lax.rsqrt(ms + EPS) * gain_ref[:, sl]
            for k in range(wide // LANES):
                lo = sl.start + k * LANES
                ab_ref[:, lo:lo + LANES] = _rope(y[:, k * LANES:(k + 1) * LANES], cos, sin).astype(ab_ref.dtype)

    @pl.when((j >= n_rope) & (j < n_rope + n_plain))
    def _():
        for sl in chunks:
            ab_ref[:, sl] = _dot(h_scr[...], w_ref[:, sl]).astype(ab_ref.dtype)

    @pl.when(j >= n_rope + n_plain)
    def _():
        for sl in chunks:
            gt_ref[:, sl] = jax.nn.sigmoid(_dot(h_scr[...], w_ref[:, sl]))


def _inproj(x2d, g, w, gain, cos, sin, bd, seq, n_rope, n_plain, n_gate):
    n, d = x2d.shape
    tm, tn = min(PROJ_TM, seq), PROJ_TN
    assert n % tm == 0 and seq % tm == 0 and w.shape[1] == (n_rope + n_plain + n_gate) * tn
    pos_blocks = seq // tm
    n_ab = n_rope + n_plain
    return pl.pallas_call(
        functools.partial(_inproj_kernel, n_rope=n_rope, n_plain=n_plain),
        out_shape=(jax.ShapeDtypeStruct((n, n_ab * tn), BF16), jax.ShapeDtypeStruct((n, n_gate * tn), F32)),
        grid=(n // tm, n_ab + n_gate),
        in_specs=[
            pl.BlockSpec((tm, d), lambda i, j: (i, 0)),
            pl.BlockSpec((1, d), lambda i, j: (0, 0)),
            pl.BlockSpec((d, tn), lambda i, j: (0, j)),
            pl.BlockSpec((1, tn), lambda i, j: (0, jnp.minimum(j, n_rope - 1))),
            pl.BlockSpec((tm, LANES), lambda i, j: (i % pos_blocks, 0)),
            pl.BlockSpec((tm, LANES), lambda i, j: (i % pos_blocks, 0)),
            pl.BlockSpec(bd.shape, lambda i, j: (0, 0)),
        ],
        out_specs=(
            pl.BlockSpec((tm, tn), lambda i, j: (i, jnp.minimum(j, n_ab - 1))),
            pl.BlockSpec((tm, tn), lambda i, j: (i, jnp.maximum(j - n_ab, 0))),
        ),
        scratch_shapes=[pltpu.VMEM((tm, d), BF16)],
        compiler_params=pltpu.CompilerParams(
            dimension_semantics=("parallel", "arbitrary"), vmem_limit_bytes=VMEM_LIMIT),
        name="inproj",
    )(x2d, g, w, gain, cos, sin, bd)


def _compress_kernel(r_ref, pos_ref, w1_ref, w2_ref, *rest, is_key):
    if is_key:
        gain_ref, cos_ref, sin_ref, o_ref = rest
    else:
        (o_ref,) = rest
    half = r_ref.shape[1]
    r = r_ref[...].astype(F32)
    top = _dot((r + pos_ref[:, :half]).astype(BF16), w1_ref[:half, :])
    bot = _dot((r + pos_ref[:, half:]).astype(BF16), w1_ref[half:, :])
    hidden = top + pltpu.roll(bot, bot.shape[0] - 1, axis=0)
    out = _dot(jax.nn.gelu(hidden).astype(BF16), w2_ref[...])
    if is_key:
        ms = jnp.mean(out * out, axis=-1, keepdims=True)
        out = _rope(out * lax.rsqrt(ms + EPS) * gain_ref[...], cos_ref[...], sin_ref[...])
    o_ref[...] = out.astype(o_ref.dtype)


def _compress(r, pos, w1, w2dup, extra, is_key):
    b, g, rows, half = r.shape
    in_specs = [
        pl.BlockSpec((None, None, rows, half), lambda i, j: (i, j, 0, 0)),
        pl.BlockSpec(pos.shape, lambda i, j: (0, 0)),
        pl.BlockSpec(w1.shape, lambda i, j: (0, 0)),
        pl.BlockSpec(w2dup.shape, lambda i, j: (0, 0)),
    ] + [pl.BlockSpec(e.shape, lambda i, j: (0, 0)) for e in extra]
    return pl.pallas_call(
        functools.partial(_compress_kernel, is_key=is_key),
        out_shape=jax.ShapeDtypeStruct((b, g, rows, LANES), BF16),
        grid=(b, g),
        in_specs=in_specs,
        out_specs=pl.BlockSpec((None, None, rows, LANES), lambda i, j: (i, j, 0, 0)),
        compiler_params=pltpu.CompilerParams(
            dimension_semantics=("parallel", "parallel"), vmem_limit_bytes=VMEM_LIMIT),
        name="compress_k" if is_key else "compress_v",
    )(r, pos, w1, w2dup, *extra)


def _online_update(s, v_ones, m_ref, acc_ref, idx):
    m_prev = m_ref[idx]
    m_next = jnp.maximum(m_prev, jnp.max(s, axis=1, keepdims=True))
    alpha = jnp.exp(m_prev - m_next)
    p = jnp.exp(s - jnp.concatenate([m_next] * (s.shape[1] // LANES), axis=1))
    m_ref[idx] = m_next
    scale = jnp.concatenate([alpha] * (acc_ref.shape[-1] // LANES), axis=1)
    acc_ref[idx] = acc_ref[idx] * scale + _dot(p.astype(BF16), v_ones)


def _init_softmax_state(m_ref, acc_ref):
    m_ref[...] = jnp.full(m_ref.shape, -jnp.inf, F32)
    acc_ref[...] = jnp.zeros(acc_ref.shape, F32)


def _add_per_head(s, bias, heads):
    rows, cols = bias.shape
    return (s.reshape(heads, rows, cols) + bias[None]).reshape(heads * rows, cols)


def _diff_attn_kernel(q_ref, k_ref, v_ref, cb_ref, lq1_ref, lk1_ref, lq2_ref, lk2_ref, sg_ref, o_ref,
                      m_ref, acc_ref, vones_ref, s_ref, *, lambda_init):
    i = pl.program_id(2)
    t = q_ref.shape[0]
    dv = v_ref.shape[1]

    @pl.when(i == 0)
    def _():
        vones_ref[:, :dv] = v_ref[...]
        vones_ref[:, dv:] = jnp.ones((v_ref.shape[0], vones_ref.shape[1] - dv), BF16)

    q = q_ref[...]
    lane = lax.broadcasted_iota(jnp.int32, q.shape, 1)
    zero = jnp.zeros_like(q)
    qc = (jnp.where(lane < HEAD_DIM, q, zero), jnp.where(lane >= HEAD_DIM, q, zero))
    _init_softmax_state(m_ref, acc_ref)

    def scores(j, slot):
        k = k_ref[pl.ds(pl.multiple_of(j * t, t), t), :]
        for c in range(2):
            s_ref[slot, c] = _dot_nt(qc[c], k)

    def update(j, slot, masked):
        v = vones_ref[pl.ds(pl.multiple_of(j * t, t), t), :]
        for c in range(2):
            s = s_ref[slot, c]
            if masked:
                s = s + cb_ref[...]
            _online_update(s, v, m_ref, acc_ref, c)

    scores(0, 0)

    def body(pp, carry):
        scores(2 * pp + 1, 1)
        update(2 * pp, 0, False)
        scores(2 * pp + 2, 0)
        update(2 * pp + 1, 1, False)
        return carry

    lax.fori_loop(0, i // 2, body, 0)

    @pl.when(i % 2 == 0)
    def _():
        update(i, 0, True)

    @pl.when(i % 2 == 1)
    def _():
        scores(i, 1)
        update(i - 1, 0, False)
        update(i, 1, True)

    lam = (jnp.exp(jnp.sum(lq1_ref[...] * lk1_ref[...], axis=-1, keepdims=True))
           - jnp.exp(jnp.sum(lq2_ref[...] * lk2_ref[...], axis=-1, keepdims=True)) + lambda_init)
    o = acc_ref[0, :, :dv] / acc_ref[0, :, dv:] - lam * (acc_ref[1, :, :dv] / acc_ref[1, :, dv:])
    ms = jnp.mean(o * o, axis=-1, keepdims=True)
    o_ref[...] = (o * lax.rsqrt(ms + EPS) * sg_ref[...] * (1.0 - lambda_init)).astype(o_ref.dtype)


def _diff_attention(ab, causal_bias, lams, subln_g, batch, seq, lambda_init, col):
    t = causal_bias.shape[0]
    nq = seq // t
    dv = 2 * HEAD_DIM
    small = lambda arr: pl.BlockSpec(arr.shape, lambda b, h, i: (0, 0))
    return pl.pallas_call(
        functools.partial(_diff_attn_kernel, lambda_init=lambda_init),
        out_shape=jax.ShapeDtypeStruct((batch * seq, DIFF_HEADS * dv), BF16),
        grid=(batch, DIFF_HEADS, nq),
        in_specs=[
            pl.BlockSpec((t, LANES), lambda b, h, i: (b * nq + i, col["dq"] // LANES + h)),
            pl.BlockSpec((seq, LANES), lambda b, h, i: (b, col["dk"] // LANES + h)),
            pl.BlockSpec((seq, dv), lambda b, h, i: (b, col["dv"] // dv + h)),
            small(causal_bias),
            small(lams[0]), small(lams[1]), small(lams[2]), small(lams[3]), small(subln_g),
        ],
        out_specs=pl.BlockSpec((t, dv), lambda b, h, i: (b * nq + i, h)),
        scratch_shapes=[
            pltpu.VMEM((2, t, LANES), F32),
            pltpu.VMEM((2, t, 2 * dv), F32),
            pltpu.VMEM((seq, 2 * dv), BF16),
            pltpu.VMEM((2, 2, t, t), F32),
        ],
        compiler_params=pltpu.CompilerParams(
            dimension_semantics=("parallel", "parallel", "arbitrary"), vmem_limit_bytes=VMEM_LIMIT),
        name="diff_attention",
    )(ab, ab, ab, causal_bias, *lams, subln_g)


def _nsa_kernel(q_ref, ksl_ref, vsl_ref, kwn_ref, vwn_ref, kc_ref, vc_ref, gate_ref, ovt_ref, cmpb_ref, cb_ref,
                wb_ref, o_ref, qaug_ref, m_ref, acc_ref, accw_ref, imp_ref, kaug_ref, vsl1_ref, vwn1_ref,
                q4_ref, ocmp_ref):
    i = pl.program_id(2)
    t = q_ref.shape[0]
    seq = ksl_ref.shape[0]
    q0 = i * t
    n_sel = seq // SEL_BLOCK
    assert n_sel <= LANES - HEAD_DIM and imp_ref.shape[0] == LANES
    lane = lax.broadcasted_iota(jnp.int32, (t, LANES), 1)
    low = lane < HEAD_DIM

    @pl.when(i == 0)
    def _():
        row = lax.broadcasted_iota(jnp.int32, (seq, LANES), 0)
        ln = lax.broadcasted_iota(jnp.int32, (seq, LANES), 1)
        kaug_ref[...] = ksl_ref[...] + jnp.where(ln - HEAD_DIM == row // SEL_BLOCK, 1.0, 0.0).astype(BF16)
        ones_hi = jnp.where(ln >= HEAD_DIM, 1.0, 0.0).astype(BF16)
        vsl1_ref[...] = vsl_ref[...] + ones_hi
        vwn1_ref[...] = vwn_ref[...] + ones_hi

    q = q_ref[...].astype(F32)
    heads = []
    for pair in range(NSA_HPG // 2):
        qp = q[:, pair * LANES:(pair + 1) * LANES]
        heads += [qp, pltpu.roll(qp, HEAD_DIM, axis=1)]

    for hh, hq in enumerate(heads):
        q4_ref[hh] = jnp.where(low, hq, 0.0).astype(BF16)
    rows = NSA_HPG * t

    tw = wb_ref.shape[1]
    wk = wb_ref.shape[2]
    assert wk == WINDOW + tw and WINDOW % tw == 0 and t % tw == 0
    for sub in range(t // tw):
        qs = q4_ref[:, sub * tw:(sub + 1) * tw, :].reshape(NSA_HPG * tw, LANES)
        sub_idx = i * (t // tw) + sub
        kstart = pl.multiple_of(jnp.maximum(q0 + sub * tw - WINDOW, 0), tw)
        s = _dot_nt(qs, kwn_ref[pl.ds(kstart, wk), :])
        s = _add_per_head(s, wb_ref[jnp.minimum(sub_idx, WINDOW // tw)], NSA_HPG)
        e = jnp.exp(s - jnp.max(s, axis=1, keepdims=True))
        acc_w = _dot(e.astype(BF16), vwn1_ref[pl.ds(kstart, wk), :])
        accw_ref[:, sub * tw:(sub + 1) * tw, :] = acc_w.reshape(NSA_HPG, tw, LANES)

    s = _add_per_head(_dot_nt(q4_ref[...].reshape(rows, LANES), kc_ref[...]), cmpb_ref[...], NSA_HPG)
    e = jnp.exp(s - jnp.max(s, axis=1, keepdims=True))
    p = e / jnp.sum(e, axis=1, keepdims=True)
    ocmp_ref[...] = _dot(p.astype(BF16), vc_ref[...])

    psum = p[0:t] + p[t:2 * t] + p[2 * t:3 * t] + p[3 * t:4 * t]
    hi, lo = _split_bf16(psum)
    ovt = ovt_ref[...]
    imp = _dot_nt(ovt, hi) + _dot_nt(ovt, lo)
    jrow = lax.broadcasted_iota(jnp.int32, imp.shape, 0) - SEL_BLOCK
    qcol = q0 + lax.broadcasted_iota(jnp.int32, imp.shape, 1)
    cur = qcol // SEL_BLOCK
    forced = (jrow == 0) | (jrow == cur) | (jrow == cur - 1)
    imp = jnp.where(forced, FORCED_SCORE, imp)
    imp = jnp.where((jrow >= 0) & (jrow * SEL_BLOCK <= qcol), imp, -jnp.inf)
    imp_ref[...] = imp
    imp_b = imp[SEL_BLOCK:, :]
    jb = lax.broadcasted_iota(jnp.int32, imp_b.shape, 0)
    rank = jnp.zeros(imp_b.shape, F32)
    for kk in range(n_sel):
        rk = imp_ref[SEL_BLOCK + kk:SEL_BLOCK + kk + 1, :]
        rank = rank + jnp.where(rk > imp_b, 1.0, jnp.where((rk == imp_b) & (jb > kk), 1.0, 0.0))
    chosen = (rank < float(min(SEL_TOP, n_sel))) & (imp_b > -jnp.inf)
    bias_t = jnp.concatenate([jnp.zeros((SEL_BLOCK, t), F32), jnp.where(chosen, 0.0, NEG_BLOCK)], axis=0)
    bias = bias_t.T
    for hh, hq in enumerate(heads):
        qaug_ref[hh] = jnp.where(low, hq, bias).astype(BF16)

    _init_softmax_state(m_ref, acc_ref)

    def slc_step(j, masked):
        start = pl.multiple_of(j * t, t)
        s = _dot_nt(qaug_ref[...].reshape(rows, LANES), kaug_ref[pl.ds(start, t), :])
        if masked:
            s = _add_per_head(s, cb_ref[...], NSA_HPG)
        _online_update(s, vsl1_ref[pl.ds(start, t), :], m_ref, acc_ref, 0)

    def slc_body(j, carry):
        slc_step(j, False)
        return carry

    lax.fori_loop(0, i, slc_body, 0)
    slc_step(i, True)

    denom = jnp.ones((t, LANES), F32)
    for hh in range(NSA_HPG):
        denom = jnp.where(lane == HEAD_DIM + 3 * hh + 1, acc_ref[0, pl.ds(hh * t, t), :],
                          jnp.where(lane == HEAD_DIM + 3 * hh + 2, accw_ref[hh], denom))
    coef = gate_ref[...] / denom
    has_cmp = q0 + lax.broadcasted_iota(jnp.int32, (t, 1), 0) >= CMP_BLOCK - 1
    outs = []
    for hh in range(NSA_HPG):
        c0 = HEAD_DIM + 3 * hh
        outs.append(jnp.where(has_cmp, coef[:, c0:c0 + 1], 0.0) * ocmp_ref[pl.ds(hh * t, t), :]
                    + coef[:, c0 + 1:c0 + 2] * acc_ref[0, pl.ds(hh * t, t), :]
                    + coef[:, c0 + 2:c0 + 3] * accw_ref[hh])
    for pair in range(NSA_HPG // 2):
        o_ref[:, pair * LANES:(pair + 1) * LANES] = jnp.where(
            low, outs[2 * pair], pltpu.roll(outs[2 * pair + 1], HEAD_DIM, axis=1)).astype(o_ref.dtype)


def _nsa_attention(ab, gates, kcmp, vcmp, ovt, cmp_bias, causal_bias, window_bias, batch, seq, col, col_gate):
    t = causal_bias.shape[0]
    nq = seq // t
    gw = NSA_HPG * HEAD_DIM
    rows = NSA_HPG * t
    n_cmp = kcmp.shape[2]
    full = lambda arr: pl.BlockSpec(arr.shape, lambda b, g, i: (0,) * arr.ndim)
    kv = lambda name: pl.BlockSpec((seq, LANES), lambda b, g, i: (b, col[name] // LANES + g))
    return pl.pallas_call(
        _nsa_kernel,
        out_shape=jax.ShapeDtypeStruct((batch * seq, NSA_HEADS * HEAD_DIM), BF16),
        grid=(batch, NSA_GROUPS, nq),
        in_specs=[
            pl.BlockSpec((t, gw), lambda b, g, i: (b * nq + i, col["nq"] // gw + g)),
            kv("ksl"), kv("vsl"), kv("kwn"), kv("vwn"),
            pl.BlockSpec((None, None, n_cmp, LANES), lambda b, g, i: (b, g, 0, 0)),
            pl.BlockSpec((None, None, n_cmp, LANES), lambda b, g, i: (b, g, 0, 0)),
            pl.BlockSpec((t, LANES), lambda b, g, i: (b * nq + i, col_gate // LANES + g)),
            full(ovt),
            pl.BlockSpec((t, n_cmp), lambda b, g, i: (i, 0)),
            full(causal_bias),
            full(window_bias),
        ],
        out_specs=pl.BlockSpec((t, gw), lambda b, g, i: (b * nq + i, g)),
        scratch_shapes=[
            pltpu.VMEM((NSA_HPG, t, LANES), BF16),
            pltpu.VMEM((1, rows, LANES), F32),
            pltpu.VMEM((1, rows, LANES), F32),
            pltpu.VMEM((NSA_HPG, t, LANES), F32),
            pltpu.VMEM((LANES, t), F32),
            pltpu.VMEM((seq, LANES), BF16),
            pltpu.VMEM((seq, LANES), BF16),
            pltpu.VMEM((seq, LANES), BF16),
            pltpu.VMEM((NSA_HPG, t, LANES), BF16),
            pltpu.VMEM((rows, LANES), F32),
        ],
        compiler_params=pltpu.CompilerParams(
            dimension_semantics=("parallel", "parallel", "arbitrary"), vmem_limit_bytes=VMEM_LIMIT),
        name="nsa_attention",
    )(ab, ab, ab, ab, ab, kcmp, vcmp, gates, ovt, cmp_bias, causal_bias, window_bias)


def _merge_kernel(yd_ref, yn_ref, g0_ref, g1_ref, x_ref, wd_ref, wn_ref, wo_ref, lg_ref, x1_ref, h_ref):
    mixed = g0_ref[...] * _dot(yd_ref[...], wd_ref[...]) + g1_ref[...] * _dot(yn_ref[...], wn_ref[...])
    x1 = x_ref[...] + _dot(mixed.astype(BF16), wo_ref[...])
    x1_ref[...] = x1
    ms = jnp.mean(x1 * x1, axis=-1, keepdims=True)
    h_ref[...] = (x1 * lax.rsqrt(ms + EPS) * lg_ref[...]).astype(h_ref.dtype)


def _merge(yd, yn, gates, x2d, wd, wn, wo, lg):
    n, d = x2d.shape
    tm = min(MERGE_TM, n)
    tok = lambda c: pl.BlockSpec((tm, d), lambda i: (i, c))
    full = lambda arr: pl.BlockSpec(arr.shape, lambda i: (0, 0))
    return pl.pallas_call(
        _merge_kernel,
        out_shape=(jax.ShapeDtypeStruct((n, d), F32), jax.ShapeDtypeStruct((n, d), BF16)),
        grid=(n // tm,),
        in_specs=[tok(0), tok(0), tok(0), tok(1), tok(0), full(wd), full(wn), full(wo), full(lg)],
        out_specs=(tok(0), tok(0)),
        compiler_params=pltpu.CompilerParams(
            dimension_semantics=("parallel",), vmem_limit_bytes=VMEM_LIMIT),
        name="merge_outproj",
    )(yd, yn, gates, gates, x2d, wd, wn, wo, lg)


def _mlp_kernel(h_ref, x1_ref, wu_ref, wd_ref, o_ref, acc_ref):
    f = pl.program_id(1)

    @pl.when(f == 0)
    def _():
        acc_ref[...] = x1_ref[...]

    up = jnp.maximum(_dot(h_ref[...], wu_ref[...]), 0.0)
    acc_ref[...] += _dot((up * up).astype(BF16), wd_ref[...])

    @pl.when(f == pl.num_programs(1) - 1)
    def _():
        o_ref[...] = acc_ref[...]


def _mlp(h, x1, wu, wd):
    n, d = x1.shape
    dff = wu.shape[1]
    tm, tf = min(MLP_TM, n), MLP_TF
    return pl.pallas_call(
        _mlp_kernel,
        out_shape=jax.ShapeDtypeStruct((n, d), F32),
        grid=(n // tm, dff // tf),
        in_specs=[
            pl.BlockSpec((tm, d), lambda i, f: (i, 0)),
            pl.BlockSpec((tm, d), lambda i, f: (i, 0)),
            pl.BlockSpec((d, tf), lambda i, f: (0, f)),
            pl.BlockSpec((tf, d), lambda i, f: (f, 0)),
        ],
        out_specs=pl.BlockSpec((tm, d), lambda i, f: (i, 0)),
        scratch_shapes=[pltpu.VMEM((tm, d), F32)],
        compiler_params=pltpu.CompilerParams(
            dimension_semantics=("parallel", "arbitrary"), vmem_limit_bytes=VMEM_LIMIT),
        name="mlp",
    )(h, x1, wu, wd)


def _rope_tables(pos):
    inv_freq = ROPE_THETA ** (-jnp.arange(HALF, dtype=F32) / HALF)
    ang = pos.astype(F32)[:, None] * inv_freq[None, :]
    c, s = jnp.cos(ang), jnp.sin(ang)
    return jnp.concatenate([c, c, c, c], axis=-1), jnp.concatenate([-s, s, -s, s], axis=-1)


def _pad_groups(w, width, offset=0):
    d = w.shape[0]
    w = w.reshape(d, NSA_GROUPS, width)
    return jnp.pad(w, ((0, 0), (0, 0), (offset, LANES - width - offset))).reshape(d, NSA_GROUPS * LANES)


def _additive_mask(valid):
    return jnp.asarray(np.where(valid, 0.0, NEG_TOKEN), F32)


def _layer(x2d, batch, seq, layer, ln_mix_g, w_in, diff_q_norm_g, diff_k_norm_g, diff_lambda_q1,
           diff_lambda_k1, diff_lambda_q2, diff_lambda_k2, diff_subln_g, nsa_q_norm_g, nsa_k_norm_g,
           cmp_pos_k, cmp_pos_v, cmp_k_w1, cmp_k_w2, cmp_v_w1, cmp_v_w2, w_proj_diff, w_proj_nsa,
           w_out, ln_mlp_g, w_mlp_up, w_mlp_down):
    d = x2d.shape[1]
    diff_qk = DIFF_HEADS * 2 * HEAD_DIM
    diff_v = DIFF_HEADS * 2 * HEAD_DIM
    nsa_q = NSA_HEADS * HEAD_DIM
    nsa_kv = NSA_GROUPS * HEAD_DIM
    splits = np.cumsum([diff_qk, diff_qk, diff_v, nsa_q] + [nsa_kv] * 6 + [NSA_HEADS * 3, 2 * d])[:-1]
    (w_dq, w_dk, w_dv, w_nq, w_kc, w_vc, w_ksl, w_vsl, w_kwn, w_vwn, w_ng, w_mg) = jnp.split(
        w_in, [int(c) for c in splits], axis=1)
    scale = HEAD_DIM ** -0.5
    pad_kv = lambda w: _pad_groups(w, HEAD_DIM)

    fam_rope = [("dq", w_dq), ("dk", w_dk), ("nq", w_nq), ("ksl", pad_kv(w_ksl)), ("kwn", pad_kv(w_kwn))]
    fam_plain = [("dv", w_dv), ("vsl", pad_kv(w_vsl)), ("vwn", pad_kv(w_vwn)), ("kc", w_kc), ("vc", w_vc)]
    fam_gate = [("mg", w_mg), ("ng", _pad_groups(w_ng, NSA_HPG * 3, HEAD_DIM))]
    col, blocks, tiles = {}, [], []
    for fam in (fam_rope, fam_plain, fam_gate):
        start = sum(b.shape[1] for b in blocks)
        for name, w in fam:
            col[name] = sum(b.shape[1] for b in blocks)
            blocks.append(w)
        width = sum(b.shape[1] for b in blocks) - start
        pad = -width % PROJ_TN
        if pad:
            blocks.append(jnp.zeros((d, pad), w_in.dtype))
        tiles.append((width + pad) // PROJ_TN)
    w_all = jnp.concatenate(blocks, axis=1).astype(BF16)
    gate_base = (tiles[0] + tiles[1]) * PROJ_TN

    tile_g = lambda g, reps: jnp.tile(g.astype(F32), reps)
    gain = jnp.concatenate([
        tile_g(diff_q_norm_g, 2 * DIFF_HEADS) * scale,
        tile_g(diff_k_norm_g, 2 * DIFF_HEADS),
        tile_g(nsa_q_norm_g, NSA_HEADS) * scale,
        tile_g(nsa_k_norm_g[1], 2 * NSA_GROUPS),
        tile_g(nsa_k_norm_g[2], 2 * NSA_GROUPS),
    ])[None, :]
    assert gain.shape[1] == tiles[0] * PROJ_TN
    cos, sin = _rope_tables(jnp.arange(seq))
    mean_heads = np.kron(np.eye(MXU_WIDTH // HEAD_DIM), np.full((HEAD_DIM, HEAD_DIM), 1.0 / HEAD_DIM))
    ab, gates = _inproj(x2d, ln_mix_g.astype(F32)[None, :], w_all, gain, cos, sin,
                        jnp.asarray(mean_heads, BF16), seq, *tiles)

    n_runs = seq // CMP_STRIDE

    def runs(c0):
        kv = ab[:, c0:c0 + nsa_kv].reshape(batch, seq, NSA_GROUPS, HEAD_DIM)
        return kv.transpose(0, 2, 1, 3).reshape(batch, NSA_GROUPS, n_runs, CMP_STRIDE * HEAD_DIM)

    dup = lambda w: jnp.concatenate([w, w], axis=1).astype(BF16)
    cmp_center = jnp.arange(n_runs) * CMP_STRIDE + (CMP_BLOCK - 1) / 2.0
    cos_c, sin_c = _rope_tables(cmp_center)
    gain_c = tile_g(nsa_k_norm_g[0], 2)[None, :]
    kcmp = _compress(runs(col["kc"]), cmp_pos_k.astype(F32).reshape(1, -1), cmp_k_w1.astype(BF16),
                     dup(cmp_k_w2), (gain_c, cos_c, sin_c), True)
    vcmp = _compress(runs(col["vc"]), cmp_pos_v.astype(F32).reshape(1, -1), cmp_v_w1.astype(BF16),
                     dup(cmp_v_w2), (), False)

    lambda_init = 0.8 - 0.6 * math.exp(-0.3 * layer)
    lams = [v.astype(F32)[None, :] for v in (diff_lambda_q1, diff_lambda_k1, diff_lambda_q2, diff_lambda_k2)]
    t_att = min(ATT_T, seq)
    tri = np.arange(t_att)
    y_diff = _diff_attention(ab, _additive_mask(tri[None, :] <= tri[:, None]), lams,
                             diff_subln_g.astype(F32)[None, :], batch, seq, lambda_init, col)

    n_sel = seq // SEL_BLOCK
    cmp_start = np.arange(n_runs) * CMP_STRIDE
    sel_start = np.arange(n_sel) * SEL_BLOCK
    overlap = ((cmp_start[:, None] < sel_start[None, :] + SEL_BLOCK)
               & (cmp_start[:, None] + CMP_BLOCK - 1 >= sel_start[None, :]))
    ovt = np.zeros((LANES, n_runs))
    ovt[HEAD_DIM:HEAD_DIM + n_sel] = overlap.T
    cmp_bias = _additive_mask(cmp_start[None, :] + CMP_BLOCK - 1 <= np.arange(seq)[:, None])
    t_nsa = min(NSA_T, seq)
    tri = np.arange(t_nsa)
    t_win = min(WIN_T, t_nsa)
    r = np.arange(t_win)[:, None]
    c = np.arange(WINDOW + t_win)[None, :]
    window_bias = jnp.stack([_additive_mask((c <= r + off) & (r + off - c < WINDOW))
                             for off in range(0, WINDOW + 1, t_win)])
    y_nsa = _nsa_attention(ab, gates, kcmp, vcmp, jnp.asarray(ovt, BF16), cmp_bias,
                           _additive_mask(tri[None, :] <= tri[:, None]), window_bias, batch, seq, col,
                           col["ng"] - gate_base)

    assert col["mg"] == gate_base
    x1, h2 = _merge(y_diff, y_nsa, gates, x2d, w_proj_diff.astype(BF16), w_proj_nsa.astype(BF16),
                    w_out.astype(BF16), ln_mlp_g.astype(F32)[None, :])
    return _mlp(h2, x1, w_mlp_up.astype(BF16), w_mlp_down.astype(BF16))


def kernel(x, ln_mix_g, w_in, diff_q_norm_g, diff_k_norm_g, diff_lambda_q1, diff_lambda_k1, diff_lambda_q2, diff_lambda_k2, diff_subln_g, nsa_q_norm_g, nsa_k_norm_g, cmp_pos_k, cmp_pos_v, cmp_k_w1, cmp_k_w2, cmp_v_w1, cmp_v_w2, w_proj_diff, w_proj_nsa, w_out, ln_mlp_g, w_mlp_up, w_mlp_down):
    batch, seq, d = x.shape
    params = (ln_mix_g, w_in, diff_q_norm_g, diff_k_norm_g, diff_lambda_q1, diff_lambda_k1, diff_lambda_q2,
              diff_lambda_k2, diff_subln_g, nsa_q_norm_g, nsa_k_norm_g, cmp_pos_k, cmp_pos_v, cmp_k_w1,
              cmp_k_w2, cmp_v_w1, cmp_v_w2, w_proj_diff, w_proj_nsa, w_out, ln_mlp_g, w_mlp_up, w_mlp_down)
    x2d = x.reshape(batch * seq, d)
    for layer in range(ln_mix_g.shape[0]):
        x2d = _layer(x2d, batch, seq, layer, *[prm[layer] for prm in params])
    return x2d.reshape(batch, seq, d)
```

```python
import functools
import math

import numpy as np
import jax
import jax.numpy as jnp
from jax import lax
from jax.experimental import pallas as pl
from jax.experimental.pallas import tpu as pltpu

F32 = jnp.float32
BF16 = jnp.bfloat16

LANES = 128
MXU_WIDTH = 256
HEAD_DIM = 64
HALF = HEAD_DIM // 2
DIFF_HEADS = 8
NSA_HEADS = 16
NSA_GROUPS = 4
NSA_HPG = NSA_HEADS // NSA_GROUPS
CMP_BLOCK = 32
CMP_STRIDE = 16
SEL_BLOCK = 64
SEL_TOP = 16
WINDOW = 512
FORCED_SCORE = 1e4
ROPE_THETA = 10000.0
EPS = 1e-6
NEG_BLOCK = -1e9
NEG_TOKEN = -1e30

PROJ_TM = 1024
PROJ_TN = 512
PROJ_ROWS = 256
ATT_T = 512
NSA_T = 512
WIN_T = 256
MERGE_TM = 512
MLP_TM = 1024
MLP_TF = 1024
VMEM_LIMIT = 56 * 1024 * 1024

_NT = (((1,), (1,)), ((), ()))


def _dot(a, b):
    return jnp.dot(a, b, preferred_element_type=F32)


def _dot_nt(a, b):
    return lax.dot_general(a, b, _NT, preferred_element_type=F32)


def _split_bf16(x):
    hi = x.astype(BF16)
    lo = (x - hi.astype(F32)).astype(BF16)
    return hi, lo


def _swap_halves_within_heads(y):
    lane = lax.broadcasted_iota(jnp.int32, y.shape, 1)
    first_half = (lane % HEAD_DIM) < HALF
    return jnp.where(first_half, pltpu.roll(y, LANES - HALF, axis=1), pltpu.roll(y, HALF, axis=1))


def _rope(y, cos, sin_signed):
    return y * cos + _swap_halves_within_heads(y) * sin_signed


def _inproj_kernel(x_ref, g_ref, w_ref, gain_ref, cos_ref, sin_ref, bd_ref, ab_ref, gt_ref, h_scr,
                   *, n_rope, n_plain):
    j = pl.program_id(1)

    @pl.when(j == 0)
    def _():
        x = x_ref[...]
        ms = jnp.mean(x * x, axis=-1, keepdims=True)
        h_scr[...] = (x * lax.rsqrt(ms + EPS) * g_ref[...]).astype(BF16)

    wide = bd_ref.shape[0]
    row_chunks = [slice(r, r + PROJ_ROWS) for r in range(0, h_scr.shape[0], PROJ_ROWS)]

    @pl.when(j < n_rope)
    def _():
        for rs in row_chunks:
            acc = _dot(h_scr[rs, :], w_ref[...])
            cos = cos_ref[rs, :]
            sin = sin_ref[rs, :]
            for c in range(acc.shape[1] // wide):
                sl = slice(c * wide, (c + 1) * wide)
                xc = acc[:, sl]
                ms = _dot((xc * xc).astype(BF16), bd_ref[...])
                y = xc * lax.rsqrt(ms + EPS) * gain_ref[:, sl]
                for k in range(wide // LANES):
                    lo = sl.start + k * LANES
                    ab_ref[rs, lo:lo + LANES] = _rope(
                        y[:, k * LANES:(k + 1) * LANES], cos, sin).astype(ab_ref.dtype)

    @pl.when((j >= n_rope) & (j < n_rope + n_plain))
    def _():
        for rs in row_chunks:
            ab_ref[rs, :] = _dot(h_scr[rs, :], w_ref[...]).astype(ab_ref.dtype)

    @pl.when(j >= n_rope + n_plain)
    def _():
        for rs in row_chunks:
            gt_ref[rs, :] = jax.nn.sigmoid(_dot(h_scr[rs, :], w_ref[...]))


def _inproj(x2d, g, w, gain, cos, sin, bd, seq, n_rope, n_plain, n_gate):
    n, d = x2d.shape
    tm, tn = min(PROJ_TM, seq), PROJ_TN
    assert n % tm == 0 and seq % tm == 0 and w.shape[1] == (n_rope + n_plain + n_gate) * tn
    pos_blocks = seq // tm
    n_ab = n_rope + n_plain
    return pl.pallas_call(
        functools.partial(_inproj_kernel, n_rope=n_rope, n_plain=n_plain),
        out_shape=(jax.ShapeDtypeStruct((n, n_ab * tn), BF16), jax.ShapeDtypeStruct((n, n_gate * tn), F32)),
        grid=(n // tm, n_ab + n_gate),
        in_specs=[
            pl.BlockSpec((tm, d), lambda i, j: (i, 0)),
            pl.BlockSpec((1, d), lambda i, j: (0, 0)),
            pl.BlockSpec((d, tn), lambda i, j: (0, j)),
            pl.BlockSpec((1, tn), lambda i, j: (0, jnp.minimum(j, n_rope - 1))),
            pl.BlockSpec((tm, LANES), lambda i, j: (i % pos_blocks, 0)),
            pl.BlockSpec((tm, LANES), lambda i, j: (i % pos_blocks, 0)),
            pl.BlockSpec(bd.shape, lambda i, j: (0, 0)),
        ],
        out_specs=(
            pl.BlockSpec((tm, tn), lambda i, j: (i, jnp.minimum(j, n_ab - 1))),
            pl.BlockSpec((tm, tn), lambda i, j: (i, jnp.maximum(j - n_ab, 0))),
        ),
        scratch_shapes=[pltpu.VMEM((tm, d), BF16)],
        compiler_params=pltpu.CompilerParams(
            dimension_semantics=("parallel", "arbitrary"), vmem_limit_bytes=VMEM_LIMIT),
        name="inproj",
    )(x2d, g, w, gain, cos, sin, bd)


def _compress_kernel(x_ref, pos_ref, w1_ref, w2_ref, *rest, is_key):
    if is_key:
        gain_ref, cos_ref, sin_ref, o_ref, xf_ref = rest
    else:
        o_ref, xf_ref = rest
    n_runs = o_ref.shape[1]
    per_chunk = LANES // HEAD_DIM
    for c in range(xf_ref.shape[0]):
        xf_ref[c] = x_ref[:, c * LANES:(c + 1) * LANES].astype(F32)
    lane_group = lax.broadcasted_iota(jnp.int32, (n_runs, LANES), 1) // HEAD_DIM
    for g in range(NSA_GROUPS):
        halves = []
        for half in range(CMP_BLOCK // CMP_STRIDE):
            acc = None
            for tt in range(CMP_STRIDE):
                l = half * CMP_STRIDE + tt
                xt = xf_ref[g // per_chunk, pl.ds(tt, n_runs, stride=CMP_STRIDE), :] + pos_ref[l:l + 1, :]
                part = _dot(jnp.where(lane_group == g % per_chunk, xt, 0.0).astype(BF16), w1_ref[l])
                acc = part if acc is None else acc + part
            halves.append(acc)
        hidden = halves[0] + pltpu.roll(halves[1], n_runs - 1, axis=0)
        out = _dot(jax.nn.gelu(hidden).astype(BF16), w2_ref[...])
        if is_key:
            ms = jnp.mean(out * out, axis=-1, keepdims=True)
            out = _rope(out * lax.rsqrt(ms + EPS) * gain_ref[...], cos_ref[...], sin_ref[...])
        o_ref[g] = out.astype(o_ref.dtype)


def _compress(ab, col0, seq, pos, w1, w2dup, extra, is_key):
    batch = ab.shape[0] // seq
    width = NSA_GROUPS * HEAD_DIM
    n_runs = seq // CMP_STRIDE
    full = lambda arr: pl.BlockSpec(arr.shape, lambda b: (0,) * arr.ndim)
    return pl.pallas_call(
        functools.partial(_compress_kernel, is_key=is_key),
        out_shape=jax.ShapeDtypeStruct((batch, NSA_GROUPS, n_runs, LANES), BF16),
        grid=(batch,),
        in_specs=[pl.BlockSpec((seq, width), lambda b: (b, col0 // width)), full(pos), full(w1), full(w2dup)]
        + [full(e) for e in extra],
        out_specs=pl.BlockSpec((None, NSA_GROUPS, n_runs, LANES), lambda b: (b, 0, 0, 0)),
        scratch_shapes=[pltpu.VMEM((width // LANES, seq, LANES), F32)],
        compiler_params=pltpu.CompilerParams(dimension_semantics=("parallel",), vmem_limit_bytes=VMEM_LIMIT),
        name="compress_k" if is_key else "compress_v",
    )(ab, pos, w1, w2dup, *extra)


def _online_update(s, v_ones, m_ref, acc_ref, idx):
    m_prev = m_ref[idx]
    m_next = jnp.maximum(m_prev, jnp.max(s, axis=1, keepdims=True))
    alpha = jnp.exp(m_prev - m_next)
    p = jnp.exp(s - jnp.concatenate([m_next] * (s.shape[1] // LANES), axis=1))
    m_ref[idx] = m_next
    scale = jnp.concatenate([alpha] * (acc_ref.shape[-1] // LANES), axis=1)
    acc_ref[idx] = acc_ref[idx] * scale + _dot(p.astype(BF16), v_ones)


def _init_softmax_state(m_ref, acc_ref):
    m_ref[...] = jnp.full(m_ref.shape, -jnp.inf, F32)
    acc_ref[...] = jnp.zeros(acc_ref.shape, F32)


def _add_per_head(s, bias, heads):
    rows, cols = bias.shape
    return (s.reshape(heads, rows, cols) + bias[None]).reshape(heads * rows, cols)


def _diff_attn_kernel(q_ref, k_ref, v_ref, cb_ref, lq1_ref, lk1_ref, lq2_ref, lk2_ref, sg_ref, o_ref,
                      m_ref, acc_ref, vones_ref, s_ref, *, lambda_init):
    i = pl.program_id(2)
    t = q_ref.shape[0]
    dv = v_ref.shape[1]

    @pl.when(i == 0)
    def _():
        vones_ref[:, :dv] = v_ref[...]
        vones_ref[:, dv:] = jnp.ones((v_ref.shape[0], vones_ref.shape[1] - dv), BF16)

    q = q_ref[...]
    lane = lax.broadcasted_iota(jnp.int32, q.shape, 1)
    zero = jnp.zeros_like(q)
    qc = (jnp.where(lane < HEAD_DIM, q, zero), jnp.where(lane >= HEAD_DIM, q, zero))
    _init_softmax_state(m_ref, acc_ref)

    def scores(j, slot):
        k = k_ref[pl.ds(pl.multiple_of(j * t, t), t), :]
        for c in range(2):
            s_ref[slot, c] = _dot_nt(qc[c], k)

    def update(j, slot, masked):
        v = vones_ref[pl.ds(pl.multiple_of(j * t, t), t), :]
        for c in range(2):
            s = s_ref[slot, c]
            if masked:
                s = s + cb_ref[...]
            _online_update(s, v, m_ref, acc_ref, c)

    scores(0, 0)

    def body(pp, carry):
        scores(2 * pp + 1, 1)
        update(2 * pp, 0, False)
        scores(2 * pp + 2, 0)
        update(2 * pp + 1, 1, False)
        return carry

    lax.fori_loop(0, i // 2, body, 0)

    @pl.when(i % 2 == 0)
    def _():
        update(i, 0, True)

    @pl.when(i % 2 == 1)
    def _():
        scores(i, 1)
        update(i - 1, 0, False)
        update(i, 1, True)

    lam = (jnp.exp(jnp.sum(lq1_ref[...] * lk1_ref[...], axis=-1, keepdims=True))
           - jnp.exp(jnp.sum(lq2_ref[...] * lk2_ref[...], axis=-1, keepdims=True)) + lambda_init)
    o = acc_ref[0, :, :dv] / acc_ref[0, :, dv:] - lam * (acc_ref[1, :, :dv] / acc_ref[1, :, dv:])
    ms = jnp.mean(o * o, axis=-1, keepdims=True)
    o_ref[...] = (o * lax.rsqrt(ms + EPS) * sg_ref[...] * (1.0 - lambda_init)).astype(o_ref.dtype)


def _diff_attention(ab, causal_bias, lams, subln_g, batch, seq, lambda_init, col):
    t = causal_bias.shape[0]
    nq = seq // t
    dv = 2 * HEAD_DIM
    small = lambda arr: pl.BlockSpec(arr.shape, lambda b, h, i: (0, 0))
    return pl.pallas_call(
        functools.partial(_diff_attn_kernel, lambda_init=lambda_init),
        out_shape=jax.ShapeDtypeStruct((batch * seq, DIFF_HEADS * dv), BF16),
        grid=(batch, DIFF_HEADS, nq),
        in_specs=[
            pl.BlockSpec((t, LANES), lambda b, h, i: (b * nq + i, col["dq"] // LANES + h)),
            pl.BlockSpec((seq, LANES), lambda b, h, i: (b, col["dk"] // LANES + h)),
            pl.BlockSpec((seq, dv), lambda b, h, i: (b, col["dv"] // dv + h)),
            small(causal_bias),
            small(lams[0]), small(lams[1]), small(lams[2]), small(lams[3]), small(subln_g),
        ],
        out_specs=pl.BlockSpec((t, dv), lambda b, h, i: (b * nq + i, h)),
        scratch_shapes=[
            pltpu.VMEM((2, t, LANES), F32),
            pltpu.VMEM((2, t, 2 * dv), F32),
            pltpu.VMEM((seq, 2 * dv), BF16),
            pltpu.VMEM((2, 2, t, t), F32),
        ],
        compiler_params=pltpu.CompilerParams(
            dimension_semantics=("parallel", "parallel", "arbitrary"), vmem_limit_bytes=VMEM_LIMIT),
        name="diff_attention",
    )(ab, ab, ab, causal_bias, *lams, subln_g)


def _nsa_kernel(q_ref, ksl_ref, vsl_ref, kwn_ref, vwn_ref, kc_ref, vc_ref, gate_ref, ovt_ref, cmpb_ref, cb_ref,
                wb_ref, o_ref, qaug_ref, m_ref, acc_ref, accw_ref, imp_ref, kaug_ref, vsl1_ref, vwn1_ref,
                q4_ref, ocmp_ref, rank_ref):
    i = pl.program_id(2)
    t = q_ref.shape[0]
    seq = ksl_ref.shape[0]
    q0 = i * t
    n_sel = seq // SEL_BLOCK
    assert n_sel <= LANES - HEAD_DIM and imp_ref.shape[0] == LANES
    lane = lax.broadcasted_iota(jnp.int32, (t, LANES), 1)
    low = lane < HEAD_DIM

    @pl.when(i == 0)
    def _():
        row = lax.broadcasted_iota(jnp.int32, (seq, LANES), 0)
        ln = lax.broadcasted_iota(jnp.int32, (seq, LANES), 1)
        kaug_ref[...] = ksl_ref[...] + jnp.where(ln - HEAD_DIM == row // SEL_BLOCK, 1.0, 0.0).astype(BF16)
        ones_hi = jnp.where(ln >= HEAD_DIM, 1.0, 0.0).astype(BF16)
        vsl1_ref[...] = vsl_ref[...] + ones_hi
        vwn1_ref[...] = vwn_ref[...] + ones_hi

    q = q_ref[...].astype(F32)
    heads = []
    for pair in range(NSA_HPG // 2):
        qp = q[:, pair * LANES:(pair + 1) * LANES]
        heads += [qp, pltpu.roll(qp, HEAD_DIM, axis=1)]

    for hh, hq in enumerate(heads):
        q4_ref[hh] = jnp.where(low, hq, 0.0).astype(BF16)
    rows = NSA_HPG * t

    tw = wb_ref.shape[1]
    wk = wb_ref.shape[2]
    assert wk == WINDOW + tw and WINDOW % tw == 0 and t % tw == 0
    for sub in range(t // tw):
        qs = q4_ref[:, sub * tw:(sub + 1) * tw, :].reshape(NSA_HPG * tw, LANES)
        sub_idx = i * (t // tw) + sub
        kstart = pl.multiple_of(jnp.maximum(q0 + sub * tw - WINDOW, 0), tw)
        s = _dot_nt(qs, kwn_ref[pl.ds(kstart, wk), :])
        s = _add_per_head(s, wb_ref[jnp.minimum(sub_idx, WINDOW // tw)], NSA_HPG)
        e = jnp.exp(s - jnp.max(s, axis=1, keepdims=True))
        acc_w = _dot(e.astype(BF16), vwn1_ref[pl.ds(kstart, wk), :])
        accw_ref[:, sub * tw:(sub + 1) * tw, :] = acc_w.reshape(NSA_HPG, tw, LANES)

    s = _add_per_head(_dot_nt(q4_ref[...].reshape(rows, LANES), kc_ref[...]), cmpb_ref[...], NSA_HPG)
    e = jnp.exp(s - jnp.max(s, axis=1, keepdims=True))
    p = e / jnp.sum(e, axis=1, keepdims=True)
    ocmp_ref[...] = _dot(p.astype(BF16), vc_ref[...])

    psum = p[0:t] + p[t:2 * t] + p[2 * t:3 * t] + p[3 * t:4 * t]
    hi, lo = _split_bf16(psum)
    ovt = ovt_ref[...]
    imp = _dot_nt(ovt, hi) + _dot_nt(ovt, lo)
    jrow = lax.broadcasted_iota(jnp.int32, imp.shape, 0) - SEL_BLOCK
    qcol = q0 + lax.broadcasted_iota(jnp.int32, imp.shape, 1)
    cur = qcol // SEL_BLOCK
    forced = (jrow == 0) | (jrow == cur) | (jrow == cur - 1)
    imp = jnp.where(forced, FORCED_SCORE, imp)
    imp = jnp.where((jrow >= 0) & (jrow * SEL_BLOCK <= qcol), imp, -jnp.inf)
    imp_ref[...] = imp
    imp_b = imp[SEL_BLOCK:, :]
    jb = lax.broadcasted_iota(jnp.int32, imp_b.shape, 0)
    per_tile = t // SEL_BLOCK
    rank_ref[...] = jnp.zeros(rank_ref.shape, F32)

    def count_group(g8):
        rank = rank_ref[...]
        for kk in range(g8 * per_tile, (g8 + 1) * per_tile):
            rk = imp_ref[SEL_BLOCK + kk:SEL_BLOCK + kk + 1, :]
            rank = rank + jnp.where(rk > imp_b, 1.0, jnp.where((rk == imp_b) & (jb > kk), 1.0, 0.0))
        rank_ref[...] = rank

    count_group(0)
    for g8 in range(1, n_sel // per_tile):
        pl.when(i >= g8)(functools.partial(count_group, g8))
    chosen = (rank_ref[...] < float(min(SEL_TOP, n_sel))) & (imp_b > -jnp.inf)
    bias_t = jnp.concatenate([jnp.zeros((SEL_BLOCK, t), F32), jnp.where(chosen, 0.0, NEG_BLOCK)], axis=0)
    bias = bias_t.T
    for hh, hq in enumerate(heads):
        qaug_ref[hh] = jnp.where(low, hq, bias).astype(BF16)

    _init_softmax_state(m_ref, acc_ref)

    def slc_step(j, masked):
        start = pl.multiple_of(j * t, t)
        s = _dot_nt(qaug_ref[...].reshape(rows, LANES), kaug_ref[pl.ds(start, t), :])
        if masked:
            s = _add_per_head(s, cb_ref[...], NSA_HPG)
        _online_update(s, vsl1_ref[pl.ds(start, t), :], m_ref, acc_ref, 0)

    def slc_body(j, carry):
        slc_step(j, False)
        return carry

    lax.fori_loop(0, i, slc_body, 0)
    slc_step(i, True)

    denom = jnp.ones((t, LANES), F32)
    for hh in range(NSA_HPG):
        denom = jnp.where(lane == HEAD_DIM + 3 * hh + 1, acc_ref[0, pl.ds(hh * t, t), :],
                          jnp.where(lane == HEAD_DIM + 3 * hh + 2, accw_ref[hh], denom))
    coef = gate_ref[...] / denom
    has_cmp = q0 + lax.broadcasted_iota(jnp.int32, (t, 1), 0) >= CMP_BLOCK - 1
    outs = []
    for hh in range(NSA_HPG):
        c0 = HEAD_DIM + 3 * hh
        outs.append(jnp.where(has_cmp, coef[:, c0:c0 + 1], 0.0) * ocmp_ref[pl.ds(hh * t, t), :]
                    + coef[:, c0 + 1:c0 + 2] * acc_ref[0, pl.ds(hh * t, t), :]
                    + coef[:, c0 + 2:c0 + 3] * accw_ref[hh])
    for pair in range(NSA_HPG // 2):
        o_ref[:, pair * LANES:(pair + 1) * LANES] = jnp.where(
            low, outs[2 * pair], pltpu.roll(outs[2 * pair + 1], HEAD_DIM, axis=1)).astype(o_ref.dtype)


def _nsa_attention(ab, gates, kcmp, vcmp, ovt, cmp_bias, causal_bias, window_bias, batch, seq, col, col_gate):
    t = causal_bias.shape[0]
    nq = seq // t
    gw = NSA_HPG * HEAD_DIM
    rows = NSA_HPG * t
    n_cmp = kcmp.shape[2]
    full = lambda arr: pl.BlockSpec(arr.shape, lambda b, g, i: (0,) * arr.ndim)
    kv = lambda name: pl.BlockSpec((seq, LANES), lambda b, g, i: (b, col[name] // LANES + g))
    return pl.pallas_call(
        _nsa_kernel,
        out_shape=jax.ShapeDtypeStruct((batch * seq, NSA_HEADS * HEAD_DIM), BF16),
        grid=(batch, NSA_GROUPS, nq),
        in_specs=[
            pl.BlockSpec((t, gw), lambda b, g, i: (b * nq + i, col["nq"] // gw + g)),
            kv("ksl"), kv("vsl"), kv("kwn"), kv("vwn"),
            pl.BlockSpec((None, None, n_cmp, LANES), lambda b, g, i: (b, g, 0, 0)),
            pl.BlockSpec((None, None, n_cmp, LANES), lambda b, g, i: (b, g, 0, 0)),
            pl.BlockSpec((t, LANES), lambda b, g, i: (b * nq + i, col_gate // LANES + g)),
            full(ovt),
            pl.BlockSpec((t, n_cmp), lambda b, g, i: (i, 0)),
            full(causal_bias),
            full(window_bias),
        ],
        out_specs=pl.BlockSpec((t, gw), lambda b, g, i: (b * nq + i, g)),
        scratch_shapes=[
            pltpu.VMEM((NSA_HPG, t, LANES), BF16),
            pltpu.VMEM((1, rows, LANES), F32),
            pltpu.VMEM((1, rows, LANES), F32),
            pltpu.VMEM((NSA_HPG, t, LANES), F32),
            pltpu.VMEM((LANES, t), F32),
            pltpu.VMEM((seq, LANES), BF16),
            pltpu.VMEM((seq, LANES), BF16),
            pltpu.VMEM((seq, LANES), BF16),
            pltpu.VMEM((NSA_HPG, t, LANES), BF16),
            pltpu.VMEM((rows, LANES), F32),
            pltpu.VMEM((SEL_BLOCK, t), F32),
        ],
        compiler_params=pltpu.CompilerParams(
            dimension_semantics=("parallel", "parallel", "arbitrary"), vmem_limit_bytes=VMEM_LIMIT),
        name="nsa_attention",
    )(ab, ab, ab, ab, ab, kcmp, vcmp, gates, ovt, cmp_bias, causal_bias, window_bias)


def _merge_kernel(yd_ref, yn_ref, g0_ref, g1_ref, x_ref, wd_ref, wn_ref, wo_ref, lg_ref, x1_ref, h_ref):
    mixed = g0_ref[...] * _dot(yd_ref[...], wd_ref[...]) + g1_ref[...] * _dot(yn_ref[...], wn_ref[...])
    x1 = x_ref[...] + _dot(mixed.astype(BF16), wo_ref[...])
    x1_ref[...] = x1
    ms = jnp.mean(x1 * x1, axis=-1, keepdims=True)
    h_ref[...] = (x1 * lax.rsqrt(ms + EPS) * lg_ref[...]).astype(h_ref.dtype)


def _merge(yd, yn, gates, x2d, wd, wn, wo, lg):
    n, d = x2d.shape
    tm = min(MERGE_TM, n)
    tok = lambda c: pl.BlockSpec((tm, d), lambda i: (i, c))
    full = lambda arr: pl.BlockSpec(arr.shape, lambda i: (0, 0))
    return pl.pallas_call(
        _merge_kernel,
        out_shape=(jax.ShapeDtypeStruct((n, d), F32), jax.ShapeDtypeStruct((n, d), BF16)),
        grid=(n // tm,),
        in_specs=[tok(0), tok(0), tok(0), tok(1), tok(0), full(wd), full(wn), full(wo), full(lg)],
        out_specs=(tok(0), tok(0)),
        compiler_params=pltpu.CompilerParams(
            dimension_semantics=("parallel",), vmem_limit_bytes=VMEM_LIMIT),
        name="merge_outproj",
    )(yd, yn, gates, gates, x2d, wd, wn, wo, lg)


def _mlp_kernel(h_ref, x1_ref, wu_ref, wd_ref, o_ref, acc_ref):
    f = pl.program_id(1)

    @pl.when(f == 0)
    def _():
        acc_ref[...] = x1_ref[...]

    up = jnp.maximum(_dot(h_ref[...], wu_ref[...]), 0.0)
    acc_ref[...] += _dot((up * up).astype(BF16), wd_ref[...])

    @pl.when(f == pl.num_programs(1) - 1)
    def _():
        o_ref[...] = acc_ref[...]


def _mlp(h, x1, wu, wd):
    n, d = x1.shape
    dff = wu.shape[1]
    tm, tf = min(MLP_TM, n), MLP_TF
    return pl.pallas_call(
        _mlp_kernel,
        out_shape=jax.ShapeDtypeStruct((n, d), F32),
        grid=(n // tm, dff // tf),
        in_specs=[
            pl.BlockSpec((tm, d), lambda i, f: (i, 0)),
            pl.BlockSpec((tm, d), lambda i, f: (i, 0)),
            pl.BlockSpec((d, tf), lambda i, f: (0, f)),
            pl.BlockSpec((tf, d), lambda i, f: (f, 0)),
        ],
        out_specs=pl.BlockSpec((tm, d), lambda i, f: (i, 0)),
        scratch_shapes=[pltpu.VMEM((tm, d), F32)],
        compiler_params=pltpu.CompilerParams(
            dimension_semantics=("parallel", "arbitrary"), vmem_limit_bytes=VMEM_LIMIT),
        name="mlp",
    )(h, x1, wu, wd)


def _rope_tables(pos):
    inv_freq = ROPE_THETA ** (-jnp.arange(HALF, dtype=F32) / HALF)
    ang = pos.astype(F32)[:, None] * inv_freq[None, :]
    c, s = jnp.cos(ang), jnp.sin(ang)
    return jnp.concatenate([c, c, c, c], axis=-1), jnp.concatenate([-s, s, -s, s], axis=-1)


def _pad_groups(w, width, offset=0):
    d = w.shape[0]
    w = w.reshape(d, NSA_GROUPS, width)
    return jnp.pad(w, ((0, 0), (0, 0), (offset, LANES - width - offset))).reshape(d, NSA_GROUPS * LANES)


def _additive_mask(valid):
    return jnp.asarray(np.where(valid, 0.0, NEG_TOKEN), F32)


def _layer(x2d, batch, seq, layer, ln_mix_g, w_in, diff_q_norm_g, diff_k_norm_g, diff_lambda_q1,
           diff_lambda_k1, diff_lambda_q2, diff_lambda_k2, diff_subln_g, nsa_q_norm_g, nsa_k_norm_g,
           cmp_pos_k, cmp_pos_v, cmp_k_w1, cmp_k_w2, cmp_v_w1, cmp_v_w2, w_proj_diff, w_proj_nsa,
           w_out, ln_mlp_g, w_mlp_up, w_mlp_down):
    d = x2d.shape[1]
    diff_qk = DIFF_HEADS * 2 * HEAD_DIM
    diff_v = DIFF_HEADS * 2 * HEAD_DIM
    nsa_q = NSA_HEADS * HEAD_DIM
    nsa_kv = NSA_GROUPS * HEAD_DIM
    splits = np.cumsum([diff_qk, diff_qk, diff_v, nsa_q] + [nsa_kv] * 6 + [NSA_HEADS * 3, 2 * d])[:-1]
    (w_dq, w_dk, w_dv, w_nq, w_kc, w_vc, w_ksl, w_vsl, w_kwn, w_vwn, w_ng, w_mg) = jnp.split(
        w_in, [int(c) for c in splits], axis=1)
    scale = HEAD_DIM ** -0.5
    pad_kv = lambda w: _pad_groups(w, HEAD_DIM)

    fam_rope = [("dq", w_dq), ("dk", w_dk), ("nq", w_nq), ("ksl", pad_kv(w_ksl)), ("kwn", pad_kv(w_kwn))]
    fam_plain = [("dv", w_dv), ("vsl", pad_kv(w_vsl)), ("vwn", pad_kv(w_vwn)), ("kc", w_kc), ("vc", w_vc)]
    fam_gate = [("mg", w_mg), ("ng", _pad_groups(w_ng, NSA_HPG * 3, HEAD_DIM))]
    col, blocks, tiles = {}, [], []
    for fam in (fam_rope, fam_plain, fam_gate):
        start = sum(b.shape[1] for b in blocks)
        for name, w in fam:
            col[name] = sum(b.shape[1] for b in blocks)
            blocks.append(w)
        width = sum(b.shape[1] for b in blocks) - start
        pad = -width % PROJ_TN
        if pad:
            blocks.append(jnp.zeros((d, pad), w_in.dtype))
        tiles.append((width + pad) // PROJ_TN)
    w_all = jnp.concatenate(blocks, axis=1).astype(BF16)
    gate_base = (tiles[0] + tiles[1]) * PROJ_TN

    tile_g = lambda g, reps: jnp.tile(g.astype(F32), reps)
    gain = jnp.concatenate([
        tile_g(diff_q_norm_g, 2 * DIFF_HEADS) * scale,
        tile_g(diff_k_norm_g, 2 * DIFF_HEADS),
        tile_g(nsa_q_norm_g, NSA_HEADS) * scale,
        tile_g(nsa_k_norm_g[1], 2 * NSA_GROUPS),
        tile_g(nsa_k_norm_g[2], 2 * NSA_GROUPS),
    ])[None, :]
    assert gain.shape[1] == tiles[0] * PROJ_TN
    cos, sin = _rope_tables(jnp.arange(seq))
    mean_heads = np.kron(np.eye(MXU_WIDTH // HEAD_DIM), np.full((HEAD_DIM, HEAD_DIM), 1.0 / HEAD_DIM))
    ab, gates = _inproj(x2d, ln_mix_g.astype(F32)[None, :], w_all, gain, cos, sin,
                        jnp.asarray(mean_heads, BF16), seq, *tiles)

    n_runs = seq // CMP_STRIDE
    dup = lambda w: jnp.concatenate([w, w], axis=1).astype(BF16)
    reps = LANES // HEAD_DIM
    per_group = lambda pos: jnp.tile(pos.astype(F32), (1, reps))
    rows_per_group = lambda w1: jnp.tile(
        w1.reshape(CMP_BLOCK, HEAD_DIM, w1.shape[1]), (1, reps, 1)).astype(BF16)
    cmp_center = jnp.arange(n_runs) * CMP_STRIDE + (CMP_BLOCK - 1) / 2.0
    cos_c, sin_c = _rope_tables(cmp_center)
    gain_c = tile_g(nsa_k_norm_g[0], 2)[None, :]
    kcmp = _compress(ab, col["kc"], seq, per_group(cmp_pos_k), rows_per_group(cmp_k_w1), dup(cmp_k_w2),
                     (gain_c, cos_c, sin_c), True)
    vcmp = _compress(ab, col["vc"], seq, per_group(cmp_pos_v), rows_per_group(cmp_v_w1), dup(cmp_v_w2),
                     (), False)

    lambda_init = 0.8 - 0.6 * math.exp(-0.3 * layer)
    lams = [v.astype(F32)[None, :] for v in (diff_lambda_q1, diff_lambda_k1, diff_lambda_q2, diff_lambda_k2)]
    t_att = min(ATT_T, seq)
    tri = np.arange(t_att)
    y_diff = _diff_attention(ab, _additive_mask(tri[None, :] <= tri[:, None]), lams,
                             diff_subln_g.astype(F32)[None, :], batch, seq, lambda_init, col)

    n_sel = seq // SEL_BLOCK
    cmp_start = np.arange(n_runs) * CMP_STRIDE
    sel_start = np.arange(n_sel) * SEL_BLOCK
    overlap = ((cmp_start[:, None] < sel_start[None, :] + SEL_BLOCK)
               & (cmp_start[:, None] + CMP_BLOCK - 1 >= sel_start[None, :]))
    ovt = np.zeros((LANES, n_runs))
    ovt[HEAD_DIM:HEAD_DIM + n_sel] = overlap.T
    cmp_bias = _additive_mask(cmp_start[None, :] + CMP_BLOCK - 1 <= np.arange(seq)[:, None])
    t_nsa = min(NSA_T, seq)
    tri = np.arange(t_nsa)
    t_win = min(WIN_T, t_nsa)
    r = np.arange(t_win)[:, None]
    c = np.arange(WINDOW + t_win)[None, :]
    window_bias = jnp.stack([_additive_mask((c <= r + off) & (r + off - c < WINDOW))
                             for off in range(0, WINDOW + 1, t_win)])
    y_nsa = _nsa_attention(ab, gates, kcmp, vcmp, jnp.asarray(ovt, BF16), cmp_bias,
                           _additive_mask(tri[None, :] <= tri[:, None]), window_bias, batch, seq, col,
                           col["ng"] - gate_base)

    assert col["mg"] == gate_base
    x1, h2 = _merge(y_diff, y_nsa, gates, x2d, w_proj_diff.astype(BF16), w_proj_nsa.astype(BF16),
                    w_out.astype(BF16), ln_mlp_g.astype(F32)[None, :])
    return _mlp(h2, x1, w_mlp_up.astype(BF16), w_mlp_down.astype(BF16))


def kernel(x, ln_mix_g, w_in, diff_q_norm_g, diff_k_norm_g, diff_lambda_q1, diff_lambda_k1, diff_lambda_q2, diff_lambda_k2, diff_subln_g, nsa_q_norm_g, nsa_k_norm_g, cmp_pos_k, cmp_pos_v, cmp_k_w1, cmp_k_w2, cmp_v_w1, cmp_v_w2, w_proj_diff, w_proj_nsa, w_out, ln_mlp_g, w_mlp_up, w_mlp_down):
    batch, seq, d = x.shape
    params = (ln_mix_g, w_in, diff_q_norm_g, diff_k_norm_g, diff_lambda_q1, diff_lambda_k1, diff_lambda_q2,
              diff_lambda_k2, diff_subln_g, nsa_q_norm_g, nsa_k_norm_g, cmp_pos_k, cmp_pos_v, cmp_k_w1,
              cmp_k_w2, cmp_v_w1, cmp_v_w2, w_proj_diff, w_proj_nsa, w_out, ln_mlp_g, w_mlp_up, w_mlp_down)
    x2d = x.reshape(batch * seq, d)
    for layer in range(ln_mix_g.shape[0]):
        x2d = _layer(x2d, batch, seq, layer, *[prm[layer] for prm in params])
    return x2d.reshape(batch, seq, d)
```

```python
import functools
import math

import numpy as np
import jax
import jax.numpy as jnp
from jax import lax
from jax.experimental import pallas as pl
from jax.experimental.pallas import tpu as pltpu

F32 = jnp.float32
BF16 = jnp.bfloat16

LANES = 128
MXU_WIDTH = 256
HEAD_DIM = 64
HALF = HEAD_DIM // 2
DIFF_HEADS = 8
NSA_HEADS = 16
NSA_GROUPS = 4
NSA_HPG = NSA_HEADS // NSA_GROUPS
CMP_BLOCK = 32
CMP_STRIDE = 16
SEL_BLOCK = 64
SEL_TOP = 16
WINDOW = 512
FORCED_SCORE = 1e4
ROPE_THETA = 10000.0
EPS = 1e-6
NEG_BLOCK = -1e9
NEG_TOKEN = -1e30

PROJ_TM = 2048
PROJ_TN = 512
PROJ_ROWS = 256
ATT_T = 512
NSA_T = 512
WIN_T = 256
MERGE_TM = 512
MLP_TM = 1024
MLP_TF = 1024
VMEM_LIMIT = 56 * 1024 * 1024

_NT = (((1,), (1,)), ((), ()))


def _dot(a, b):
    return jnp.dot(a, b, preferred_element_type=F32)


def _dot_nt(a, b):
    return lax.dot_general(a, b, _NT, preferred_element_type=F32)


def _split_bf16(x):
    hi = x.astype(BF16)
    lo = (x - hi.astype(F32)).astype(BF16)
    return hi, lo


def _swap_halves_within_heads(y):
    lane = lax.broadcasted_iota(jnp.int32, y.shape, 1)
    first_half = (lane % HEAD_DIM) < HALF
    return jnp.where(first_half, pltpu.roll(y, LANES - HALF, axis=1), pltpu.roll(y, HALF, axis=1))


def _rope(y, cos, sin_signed):
    return y * cos + _swap_halves_within_heads(y) * sin_signed


def _inproj_kernel(x_ref, g_ref, w_ref, gain_ref, cos_ref, sin_ref, bd_ref, ab_ref, gt_ref, h_scr,
                   *, n_rope, n_plain):
    j = pl.program_id(1)

    @pl.when(j == 0)
    def _():
        x = x_ref[...]
        ms = jnp.mean(x * x, axis=-1, keepdims=True)
        h_scr[...] = (x * lax.rsqrt(ms + EPS) * g_ref[...]).astype(BF16)

    wide = bd_ref.shape[0]
    row_chunks = [slice(r, r + PROJ_ROWS) for r in range(0, h_scr.shape[0], PROJ_ROWS)]

    @pl.when(j < n_rope)
    def _():
        for rs in row_chunks:
            acc = _dot(h_scr[rs, :], w_ref[...])
            cos = cos_ref[rs, :]
            sin = sin_ref[rs, :]
            for c in range(acc.shape[1] // wide):
                sl = slice(c * wide, (c + 1) * wide)
                xc = acc[:, sl]
                ms = _dot((xc * xc).astype(BF16), bd_ref[...])
                y = xc * lax.rsqrt(ms + EPS) * gain_ref[:, sl]
                for k in range(wide // LANES):
                    lo = sl.start + k * LANES
                    ab_ref[rs, lo:lo + LANES] = _rope(
                        y[:, k * LANES:(k + 1) * LANES], cos, sin).astype(ab_ref.dtype)

    @pl.when((j >= n_rope) & (j < n_rope + n_plain))
    def _():
        for rs in row_chunks:
            ab_ref[rs, :] = _dot(h_scr[rs, :], w_ref[...]).astype(ab_ref.dtype)

    @pl.when(j >= n_rope + n_plain)
    def _():
        for rs in row_chunks:
            gt_ref[rs, :] = jax.nn.sigmoid(_dot(h_scr[rs, :], w_ref[...]))


def _inproj(x2d, g, w, gain, cos, sin, bd, seq, n_rope, n_plain, n_gate):
    n, d = x2d.shape
    tm, tn = min(PROJ_TM, seq), PROJ_TN
    assert n % tm == 0 and seq % tm == 0 and w.shape[1] == (n_rope + n_plain + n_gate) * tn
    pos_blocks = seq // tm
    n_ab = n_rope + n_plain
    return pl.pallas_call(
        functools.partial(_inproj_kernel, n_rope=n_rope, n_plain=n_plain),
        out_shape=(jax.ShapeDtypeStruct((n, n_ab * tn), BF16), jax.ShapeDtypeStruct((n, n_gate * tn), F32)),
        grid=(n // tm, n_ab + n_gate),
        in_specs=[
            pl.BlockSpec((tm, d), lambda i, j: (i, 0)),
            pl.BlockSpec((1, d), lambda i, j: (0, 0)),
            pl.BlockSpec((d, tn), lambda i, j: (0, j)),
            pl.BlockSpec((1, tn), lambda i, j: (0, jnp.minimum(j, n_rope - 1))),
            pl.BlockSpec((tm, LANES), lambda i, j: (i % pos_blocks, 0)),
            pl.BlockSpec((tm, LANES), lambda i, j: (i % pos_blocks, 0)),
            pl.BlockSpec(bd.shape, lambda i, j: (0, 0)),
        ],
        out_specs=(
            pl.BlockSpec((tm, tn), lambda i, j: (i, jnp.minimum(j, n_ab - 1))),
            pl.BlockSpec((tm, tn), lambda i, j: (i, jnp.maximum(j - n_ab, 0))),
        ),
        scratch_shapes=[pltpu.VMEM((tm, d), BF16)],
        compiler_params=pltpu.CompilerParams(
            dimension_semantics=("parallel", "arbitrary"), vmem_limit_bytes=VMEM_LIMIT),
        name="inproj",
    )(x2d, g, w, gain, cos, sin, bd)


def _compress_kernel(x_ref, pos_ref, w1_ref, w2_ref, *rest, is_key):
    if is_key:
        gain_ref, cos_ref, sin_ref, o_ref, xf_ref = rest
    else:
        o_ref, xf_ref = rest
    n_runs = o_ref.shape[1]
    per_chunk = LANES // HEAD_DIM
    for c in range(xf_ref.shape[0]):
        xf_ref[c] = x_ref[:, c * LANES:(c + 1) * LANES].astype(F32)
    lane_group = lax.broadcasted_iota(jnp.int32, (n_runs, LANES), 1) // HEAD_DIM
    for g in range(NSA_GROUPS):
        halves = []
        for half in range(CMP_BLOCK // CMP_STRIDE):
            acc = None
            for tt in range(CMP_STRIDE):
                l = half * CMP_STRIDE + tt
                xt = xf_ref[g // per_chunk, pl.ds(tt, n_runs, stride=CMP_STRIDE), :] + pos_ref[l:l + 1, :]
                part = _dot(jnp.where(lane_group == g % per_chunk, xt, 0.0).astype(BF16), w1_ref[l])
                acc = part if acc is None else acc + part
            halves.append(acc)
        hidden = halves[0] + pltpu.roll(halves[1], n_runs - 1, axis=0)
        out = _dot(jax.nn.gelu(hidden).astype(BF16), w2_ref[...])
        if is_key:
            ms = jnp.mean(out * out, axis=-1, keepdims=True)
            out = _rope(out * lax.rsqrt(ms + EPS) * gain_ref[...], cos_ref[...], sin_ref[...])
        o_ref[g] = out.astype(o_ref.dtype)


def _compress(ab, col0, seq, pos, w1, w2dup, extra, is_key):
    batch = ab.shape[0] // seq
    width = NSA_GROUPS * HEAD_DIM
    n_runs = seq // CMP_STRIDE
    full = lambda arr: pl.BlockSpec(arr.shape, lambda b: (0,) * arr.ndim)
    return pl.pallas_call(
        functools.partial(_compress_kernel, is_key=is_key),
        out_shape=jax.ShapeDtypeStruct((batch, NSA_GROUPS, n_runs, LANES), BF16),
        grid=(batch,),
        in_specs=[pl.BlockSpec((seq, width), lambda b: (b, col0 // width)), full(pos), full(w1), full(w2dup)]
        + [full(e) for e in extra],
        out_specs=pl.BlockSpec((None, NSA_GROUPS, n_runs, LANES), lambda b: (b, 0, 0, 0)),
        scratch_shapes=[pltpu.VMEM((width // LANES, seq, LANES), F32)],
        compiler_params=pltpu.CompilerParams(dimension_semantics=("parallel",), vmem_limit_bytes=VMEM_LIMIT),
        name="compress_k" if is_key else "compress_v",
    )(ab, pos, w1, w2dup, *extra)


def _online_update(s, v_ones, m_ref, acc_ref, idx):
    m_prev = m_ref[idx]
    m_next = jnp.maximum(m_prev, jnp.max(s, axis=1, keepdims=True))
    alpha = jnp.exp(m_prev - m_next)
    p = jnp.exp(s - jnp.concatenate([m_next] * (s.shape[1] // LANES), axis=1))
    m_ref[idx] = m_next
    scale = jnp.concatenate([alpha] * (acc_ref.shape[-1] // LANES), axis=1)
    acc_ref[idx] = acc_ref[idx] * scale + _dot(p.astype(BF16), v_ones)


def _init_softmax_state(m_ref, acc_ref):
    m_ref[...] = jnp.full(m_ref.shape, -jnp.inf, F32)
    acc_ref[...] = jnp.zeros(acc_ref.shape, F32)


def _add_per_head(s, bias, heads):
    rows, cols = bias.shape
    return (s.reshape(heads, rows, cols) + bias[None]).reshape(heads * rows, cols)


def _diff_attn_kernel(q_ref, k_ref, v_ref, cb_ref, lq1_ref, lk1_ref, lq2_ref, lk2_ref, sg_ref, o_ref,
                      m_ref, acc_ref, vones_ref, s_ref, *, lambda_init):
    i = pl.program_id(2)
    t = q_ref.shape[0]
    dv = v_ref.shape[1]

    @pl.when(i == 0)
    def _():
        vones_ref[:, :dv] = v_ref[...]
        vones_ref[:, dv:] = jnp.ones((v_ref.shape[0], vones_ref.shape[1] - dv), BF16)

    q = q_ref[...]
    lane = lax.broadcasted_iota(jnp.int32, q.shape, 1)
    zero = jnp.zeros_like(q)
    qc = (jnp.where(lane < HEAD_DIM, q, zero), jnp.where(lane >= HEAD_DIM, q, zero))
    _init_softmax_state(m_ref, acc_ref)

    def scores(j, slot):
        k = k_ref[pl.ds(pl.multiple_of(j * t, t), t), :]
        for c in range(2):
            s_ref[slot, c] = _dot_nt(qc[c], k)

    def update(j, slot, masked):
        v = vones_ref[pl.ds(pl.multiple_of(j * t, t), t), :]
        for c in range(2):
            s = s_ref[slot, c]
            if masked:
                s = s + cb_ref[...]
            _online_update(s, v, m_ref, acc_ref, c)

    scores(0, 0)

    def body(pp, carry):
        scores(2 * pp + 1, 1)
        update(2 * pp, 0, False)
        scores(2 * pp + 2, 0)
        update(2 * pp + 1, 1, False)
        return carry

    lax.fori_loop(0, i // 2, body, 0)

    @pl.when(i % 2 == 0)
    def _():
        update(i, 0, True)

    @pl.when(i % 2 == 1)
    def _():
        scores(i, 1)
        update(i - 1, 0, False)
        update(i, 1, True)

    lam = (jnp.exp(jnp.sum(lq1_ref[...] * lk1_ref[...], axis=-1, keepdims=True))
           - jnp.exp(jnp.sum(lq2_ref[...] * lk2_ref[...], axis=-1, keepdims=True)) + lambda_init)
    o = acc_ref[0, :, :dv] / acc_ref[0, :, dv:] - lam * (acc_ref[1, :, :dv] / acc_ref[1, :, dv:])
    ms = jnp.mean(o * o, axis=-1, keepdims=True)
    o_ref[...] = (o * lax.rsqrt(ms + EPS) * sg_ref[...] * (1.0 - lambda_init)).astype(o_ref.dtype)


def _diff_attention(ab, causal_bias, lams, subln_g, batch, seq, lambda_init, col):
    t = causal_bias.shape[0]
    nq = seq // t
    dv = 2 * HEAD_DIM
    small = lambda arr: pl.BlockSpec(arr.shape, lambda b, h, i: (0, 0))
    return pl.pallas_call(
        functools.partial(_diff_attn_kernel, lambda_init=lambda_init),
        out_shape=jax.ShapeDtypeStruct((batch * seq, DIFF_HEADS * dv), BF16),
        grid=(batch, DIFF_HEADS, nq),
        in_specs=[
            pl.BlockSpec((t, LANES), lambda b, h, i: (b * nq + i, col["dq"] // LANES + h)),
            pl.BlockSpec((seq, LANES), lambda b, h, i: (b, col["dk"] // LANES + h)),
            pl.BlockSpec((seq, dv), lambda b, h, i: (b, col["dv"] // dv + h)),
            small(causal_bias),
            small(lams[0]), small(lams[1]), small(lams[2]), small(lams[3]), small(subln_g),
        ],
        out_specs=pl.BlockSpec((t, dv), lambda b, h, i: (b * nq + i, h)),
        scratch_shapes=[
            pltpu.VMEM((2, t, LANES), F32),
            pltpu.VMEM((2, t, 2 * dv), F32),
            pltpu.VMEM((seq, 2 * dv), BF16),
            pltpu.VMEM((2, 2, t, t), F32),
        ],
        compiler_params=pltpu.CompilerParams(
            dimension_semantics=("parallel", "parallel", "arbitrary"), vmem_limit_bytes=VMEM_LIMIT),
        name="diff_attention",
    )(ab, ab, ab, causal_bias, *lams, subln_g)


def _nsa_kernel(q_ref, ksl_ref, vsl_ref, kwn_ref, vwn_ref, kc_ref, vc_ref, gate_ref, ovt_ref, cmpb_ref, cb_ref,
                wb_ref, o_ref, qaug_ref, m_ref, acc_ref, accw_ref, imp_ref, kaug_ref, vsl1_ref, vwn1_ref,
                q4_ref, part_ref, rank_ref):
    i = pl.program_id(2)
    t = q_ref.shape[0]
    seq = ksl_ref.shape[0]
    q0 = i * t
    n_sel = seq // SEL_BLOCK
    assert n_sel <= LANES - HEAD_DIM and imp_ref.shape[0] == LANES
    lane = lax.broadcasted_iota(jnp.int32, (t, LANES), 1)
    low = lane < HEAD_DIM

    @pl.when(i == 0)
    def _():
        row = lax.broadcasted_iota(jnp.int32, (seq, LANES), 0)
        ln = lax.broadcasted_iota(jnp.int32, (seq, LANES), 1)
        kaug_ref[...] = ksl_ref[...] + jnp.where(ln - HEAD_DIM == row // SEL_BLOCK, 1.0, 0.0).astype(BF16)
        ones_hi = jnp.where(ln >= HEAD_DIM, 1.0, 0.0).astype(BF16)
        vsl1_ref[...] = vsl_ref[...] + ones_hi
        vwn1_ref[...] = vwn_ref[...] + ones_hi

    q = q_ref[...].astype(F32)
    heads = []
    for pair in range(NSA_HPG // 2):
        qp = q[:, pair * LANES:(pair + 1) * LANES]
        heads += [qp, pltpu.roll(qp, HEAD_DIM, axis=1)]

    for hh, hq in enumerate(heads):
        q4_ref[hh] = jnp.where(low, hq, 0.0).astype(BF16)
    rows = NSA_HPG * t

    tw = wb_ref.shape[1]
    wk = wb_ref.shape[2]
    assert wk == WINDOW + tw and WINDOW % tw == 0 and t % tw == 0
    for sub in range(t // tw):
        qs = q4_ref[:, sub * tw:(sub + 1) * tw, :].reshape(NSA_HPG * tw, LANES)
        sub_idx = i * (t // tw) + sub
        kstart = pl.multiple_of(jnp.maximum(q0 + sub * tw - WINDOW, 0), tw)
        s = _dot_nt(qs, kwn_ref[pl.ds(kstart, wk), :])
        s = _add_per_head(s, wb_ref[jnp.minimum(sub_idx, WINDOW // tw)], NSA_HPG)
        e = jnp.exp(s - jnp.max(s, axis=1, keepdims=True))
        acc_w = _dot(e.astype(BF16), vwn1_ref[pl.ds(kstart, wk), :])
        accw_ref[:, sub * tw:(sub + 1) * tw, :] = acc_w.reshape(NSA_HPG, tw, LANES)

    s = _add_per_head(_dot_nt(q4_ref[...].reshape(rows, LANES), kc_ref[...]), cmpb_ref[...], NSA_HPG)
    e = jnp.exp(s - jnp.max(s, axis=1, keepdims=True))
    p = e / jnp.sum(e, axis=1, keepdims=True)
    o_cmp = _dot(p.astype(BF16), vc_ref[...])

    denom = jnp.ones((t, LANES), F32)
    for hh in range(NSA_HPG):
        denom = jnp.where(lane == HEAD_DIM + 3 * hh + 2, accw_ref[hh], denom)
    coef = gate_ref[...] / denom
    has_cmp = q0 + lax.broadcasted_iota(jnp.int32, (t, 1), 0) >= CMP_BLOCK - 1
    for hh in range(NSA_HPG):
        c0 = HEAD_DIM + 3 * hh
        part_ref[pl.ds(hh * t, t), :] = (jnp.where(has_cmp, coef[:, c0:c0 + 1], 0.0) * o_cmp[hh * t:(hh + 1) * t]
                                         + coef[:, c0 + 2:c0 + 3] * accw_ref[hh])

    psum = p[0:t] + p[t:2 * t] + p[2 * t:3 * t] + p[3 * t:4 * t]
    hi, lo = _split_bf16(psum)
    ovt = ovt_ref[...]
    imp = _dot_nt(ovt, hi) + _dot_nt(ovt, lo)
    jrow = lax.broadcasted_iota(jnp.int32, imp.shape, 0) - SEL_BLOCK
    qcol = q0 + lax.broadcasted_iota(jnp.int32, imp.shape, 1)
    cur = qcol // SEL_BLOCK
    forced = (jrow == 0) | (jrow == cur) | (jrow == cur - 1)
    imp = jnp.where(forced, FORCED_SCORE, imp)
    imp = jnp.where((jrow >= 0) & (jrow * SEL_BLOCK <= qcol), imp, -jnp.inf)
    imp_ref[...] = imp
    imp_b = imp[SEL_BLOCK:, :]
    jb = lax.broadcasted_iota(jnp.int32, imp_b.shape, 0)
    per_tile = t // SEL_BLOCK
    rank_ref[...] = jnp.zeros(rank_ref.shape, F32)

    def count_group(g8):
        rank = rank_ref[...]
        for kk in range(g8 * per_tile, (g8 + 1) * per_tile):
            rk = imp_ref[SEL_BLOCK + kk:SEL_BLOCK + kk + 1, :]
            rank = rank + jnp.where(rk > imp_b, 1.0, jnp.where((rk == imp_b) & (jb > kk), 1.0, 0.0))
        rank_ref[...] = rank

    count_group(0)
    for g8 in range(1, n_sel // per_tile):
        pl.when(i >= g8)(functools.partial(count_group, g8))
    chosen = (rank_ref[...] < float(min(SEL_TOP, n_sel))) & (imp_b > -jnp.inf)
    bias_t = jnp.concatenate([jnp.zeros((SEL_BLOCK, t), F32), jnp.where(chosen, 0.0, NEG_BLOCK)], axis=0)
    bias = bias_t.T
    for hh, hq in enumerate(heads):
        qaug_ref[hh] = jnp.where(low, hq, bias).astype(BF16)

    _init_softmax_state(m_ref, acc_ref)

    def slc_step(j, masked):
        start = pl.multiple_of(j * t, t)
        s = _dot_nt(qaug_ref[...].reshape(rows, LANES), kaug_ref[pl.ds(start, t), :])
        if masked:
            s = _add_per_head(s, cb_ref[...], NSA_HPG)
        _online_update(s, vsl1_ref[pl.ds(start, t), :], m_ref, acc_ref, 0)

    def slc_body(j, carry):
        slc_step(j, False)
        return carry

    lax.fori_loop(0, i, slc_body, 0)
    slc_step(i, True)

    denom = jnp.ones((t, LANES), F32)
    for hh in range(NSA_HPG):
        denom = jnp.where(lane == HEAD_DIM + 3 * hh + 1, acc_ref[0, pl.ds(hh * t, t), :], denom)
    coef = gate_ref[...] / denom
    outs = []
    for hh in range(NSA_HPG):
        c1 = HEAD_DIM + 3 * hh + 1
        outs.append(part_ref[pl.ds(hh * t, t), :] + coef[:, c1:c1 + 1] * acc_ref[0, pl.ds(hh * t, t), :])
    for pair in range(NSA_HPG // 2):
        o_ref[:, pair * LANES:(pair + 1) * LANES] = jnp.where(
            low, outs[2 * pair], pltpu.roll(outs[2 * pair + 1], HEAD_DIM, axis=1)).astype(o_ref.dtype)


def _nsa_attention(ab, gates, kcmp, vcmp, ovt, cmp_bias, causal_bias, window_bias, batch, seq, col, col_gate):
    t = causal_bias.shape[0]
    nq = seq // t
    gw = NSA_HPG * HEAD_DIM
    rows = NSA_HPG * t
    n_cmp = kcmp.shape[2]
    full = lambda arr: pl.BlockSpec(arr.shape, lambda b, g, i: (0,) * arr.ndim)
    kv = lambda name: pl.BlockSpec((seq, LANES), lambda b, g, i: (b, col[name] // LANES + g))
    return pl.pallas_call(
        _nsa_kernel,
        out_shape=jax.ShapeDtypeStruct((batch * seq, NSA_HEADS * HEAD_DIM), BF16),
        grid=(batch, NSA_GROUPS, nq),
        in_specs=[
            pl.BlockSpec((t, gw), lambda b, g, i: (b * nq + i, col["nq"] // gw + g)),
            kv("ksl"), kv("vsl"), kv("kwn"), kv("vwn"),
            pl.BlockSpec((None, None, n_cmp, LANES), lambda b, g, i: (b, g, 0, 0)),
            pl.BlockSpec((None, None, n_cmp, LANES), lambda b, g, i: (b, g, 0, 0)),
            pl.BlockSpec((t, LANES), lambda b, g, i: (b * nq + i, col_gate // LANES + g)),
            full(ovt),
            pl.BlockSpec((t, n_cmp), lambda b, g, i: (i, 0)),
            full(causal_bias),
            full(window_bias),
        ],
        out_specs=pl.BlockSpec((t, gw), lambda b, g, i: (b * nq + i, g)),
        scratch_shapes=[
            pltpu.VMEM((NSA_HPG, t, LANES), BF16),
            pltpu.VMEM((1, rows, LANES), F32),
            pltpu.VMEM((1, rows, LANES), F32),
            pltpu.VMEM((NSA_HPG, t, LANES), F32),
            pltpu.VMEM((LANES, t), F32),
            pltpu.VMEM((seq, LANES), BF16),
            pltpu.VMEM((seq, LANES), BF16),
            pltpu.VMEM((seq, LANES), BF16),
            pltpu.VMEM((NSA_HPG, t, LANES), BF16),
            pltpu.VMEM((rows, LANES), F32),
            pltpu.VMEM((SEL_BLOCK, t), F32),
        ],
        compiler_params=pltpu.CompilerParams(
            dimension_semantics=("parallel", "parallel", "arbitrary"), vmem_limit_bytes=VMEM_LIMIT),
        name="nsa_attention",
    )(ab, ab, ab, ab, ab, kcmp, vcmp, gates, ovt, cmp_bias, causal_bias, window_bias)


def _merge_kernel(yd_ref, yn_ref, g0_ref, g1_ref, x_ref, wd_ref, wn_ref, wo_ref, lg_ref, x1_ref, h_ref):
    mixed = g0_ref[...] * _dot(yd_ref[...], wd_ref[...]) + g1_ref[...] * _dot(yn_ref[...], wn_ref[...])
    x1 = x_ref[...] + _dot(mixed.astype(BF16), wo_ref[...])
    x1_ref[...] = x1
    ms = jnp.mean(x1 * x1, axis=-1, keepdims=True)
    h_ref[...] = (x1 * lax.rsqrt(ms + EPS) * lg_ref[...]).astype(h_ref.dtype)


def _merge(yd, yn, gates, x2d, wd, wn, wo, lg):
    n, d = x2d.shape
    tm = min(MERGE_TM, n)
    tok = lambda c: pl.BlockSpec((tm, d), lambda i: (i, c))
    full = lambda arr: pl.BlockSpec(arr.shape, lambda i: (0, 0))
    return pl.pallas_call(
        _merge_kernel,
        out_shape=(jax.ShapeDtypeStruct((n, d), F32), jax.ShapeDtypeStruct((n, d), BF16)),
        grid=(n // tm,),
        in_specs=[tok(0), tok(0), tok(0), tok(1), tok(0), full(wd), full(wn), full(wo), full(lg)],
        out_specs=(tok(0), tok(0)),
        compiler_params=pltpu.CompilerParams(
            dimension_semantics=("parallel",), vmem_limit_bytes=VMEM_LIMIT),
        name="merge_outproj",
    )(yd, yn, gates, gates, x2d, wd, wn, wo, lg)


def _mlp_kernel(h_ref, x1_ref, wu_ref, wd_ref, o_ref, acc_ref):
    f = pl.program_id(1)

    @pl.when(f == 0)
    def _():
        acc_ref[...] = x1_ref[...]

    up = jnp.maximum(_dot(h_ref[...], wu_ref[...]), 0.0)
    acc_ref[...] += _dot((up * up).astype(BF16), wd_ref[...])

    @pl.when(f == pl.num_programs(1) - 1)
    def _():
        o_ref[...] = acc_ref[...]


def _mlp(h, x1, wu, wd):
    n, d = x1.shape
    dff = wu.shape[1]
    tm, tf = min(MLP_TM, n), MLP_TF
    return pl.pallas_call(
        _mlp_kernel,
        out_shape=jax.ShapeDtypeStruct((n, d), F32),
        grid=(n // tm, dff // tf),
        in_specs=[
            pl.BlockSpec((tm, d), lambda i, f: (i, 0)),
            pl.BlockSpec((tm, d), lambda i, f: (i, 0)),
            pl.BlockSpec((d, tf), lambda i, f: (0, f)),
            pl.BlockSpec((tf, d), lambda i, f: (f, 0)),
        ],
        out_specs=pl.BlockSpec((tm, d), lambda i, f: (i, 0)),
        scratch_shapes=[pltpu.VMEM((tm, d), F32)],
        compiler_params=pltpu.CompilerParams(
            dimension_semantics=("parallel", "arbitrary"), vmem_limit_bytes=VMEM_LIMIT),
        name="mlp",
    )(h, x1, wu, wd)


def _rope_tables(pos):
    inv_freq = ROPE_THETA ** (-jnp.arange(HALF, dtype=F32) / HALF)
    ang = pos.astype(F32)[:, None] * inv_freq[None, :]
    c, s = jnp.cos(ang), jnp.sin(ang)
    return jnp.concatenate([c, c, c, c], axis=-1), jnp.concatenate([-s, s, -s, s], axis=-1)


def _pad_groups(w, width, offset=0):
    d = w.shape[0]
    w = w.reshape(d, NSA_GROUPS, width)
    return jnp.pad(w, ((0, 0), (0, 0), (offset, LANES - width - offset))).reshape(d, NSA_GROUPS * LANES)


def _additive_mask(valid):
    return jnp.asarray(np.where(valid, 0.0, NEG_TOKEN), F32)


def _layer(x2d, batch, seq, layer, ln_mix_g, w_in, diff_q_norm_g, diff_k_norm_g, diff_lambda_q1,
           diff_lambda_k1, diff_lambda_q2, diff_lambda_k2, diff_subln_g, nsa_q_norm_g, nsa_k_norm_g,
           cmp_pos_k, cmp_pos_v, cmp_k_w1, cmp_k_w2, cmp_v_w1, cmp_v_w2, w_proj_diff, w_proj_nsa,
           w_out, ln_mlp_g, w_mlp_up, w_mlp_down):
    d = x2d.shape[1]
    diff_qk = DIFF_HEADS * 2 * HEAD_DIM
    diff_v = DIFF_HEADS * 2 * HEAD_DIM
    nsa_q = NSA_HEADS * HEAD_DIM
    nsa_kv = NSA_GROUPS * HEAD_DIM
    splits = np.cumsum([diff_qk, diff_qk, diff_v, nsa_q] + [nsa_kv] * 6 + [NSA_HEADS * 3, 2 * d])[:-1]
    (w_dq, w_dk, w_dv, w_nq, w_kc, w_vc, w_ksl, w_vsl, w_kwn, w_vwn, w_ng, w_mg) = jnp.split(
        w_in, [int(c) for c in splits], axis=1)
    scale = HEAD_DIM ** -0.5
    pad_kv = lambda w: _pad_groups(w, HEAD_DIM)

    fam_rope = [("dq", w_dq), ("dk", w_dk), ("nq", w_nq), ("ksl", pad_kv(w_ksl)), ("kwn", pad_kv(w_kwn))]
    fam_plain = [("dv", w_dv), ("vsl", pad_kv(w_vsl)), ("vwn", pad_kv(w_vwn)), ("kc", w_kc), ("vc", w_vc)]
    fam_gate = [("mg", w_mg), ("ng", _pad_groups(w_ng, NSA_HPG * 3, HEAD_DIM))]
    col, blocks, tiles = {}, [], []
    for fam in (fam_rope, fam_plain, fam_gate):
        start = sum(b.shape[1] for b in blocks)
        for name, w in fam:
            col[name] = sum(b.shape[1] for b in blocks)
            blocks.append(w)
        width = sum(b.shape[1] for b in blocks) - start
        pad = -width % PROJ_TN
        if pad:
            blocks.append(jnp.zeros((d, pad), w_in.dtype))
        tiles.append((width + pad) // PROJ_TN)
    w_all = jnp.concatenate(blocks, axis=1).astype(BF16)
    gate_base = (tiles[0] + tiles[1]) * PROJ_TN

    tile_g = lambda g, reps: jnp.tile(g.astype(F32), reps)
    gain = jnp.concatenate([
        tile_g(diff_q_norm_g, 2 * DIFF_HEADS) * scale,
        tile_g(diff_k_norm_g, 2 * DIFF_HEADS),
        tile_g(nsa_q_norm_g, NSA_HEADS) * scale,
        tile_g(nsa_k_norm_g[1], 2 * NSA_GROUPS),
        tile_g(nsa_k_norm_g[2], 2 * NSA_GROUPS),
    ])[None, :]
    assert gain.shape[1] == tiles[0] * PROJ_TN
    cos, sin = _rope_tables(jnp.arange(seq))
    mean_heads = np.kron(np.eye(MXU_WIDTH // HEAD_DIM), np.full((HEAD_DIM, HEAD_DIM), 1.0 / HEAD_DIM))
    ab, gates = _inproj(x2d, ln_mix_g.astype(F32)[None, :], w_all, gain, cos, sin,
                        jnp.asarray(mean_heads, BF16), seq, *tiles)

    n_runs = seq // CMP_STRIDE
    dup = lambda w: jnp.concatenate([w, w], axis=1).astype(BF16)
    reps = LANES // HEAD_DIM
    per_group = lambda pos: jnp.tile(pos.astype(F32), (1, reps))
    rows_per_group = lambda w1: jnp.tile(
        w1.reshape(CMP_BLOCK, HEAD_DIM, w1.shape[1]), (1, reps, 1)).astype(BF16)
    cmp_center = jnp.arange(n_runs) * CMP_STRIDE + (CMP_BLOCK - 1) / 2.0
    cos_c, sin_c = _rope_tables(cmp_center)
    gain_c = tile_g(nsa_k_norm_g[0], 2)[None, :]
    kcmp = _compress(ab, col["kc"], seq, per_group(cmp_pos_k), rows_per_group(cmp_k_w1), dup(cmp_k_w2),
                     (gain_c, cos_c, sin_c), True)
    vcmp = _compress(ab, col["vc"], seq, per_group(cmp_pos_v), rows_per_group(cmp_v_w1), dup(cmp_v_w2),
                     (), False)

    lambda_init = 0.8 - 0.6 * math.exp(-0.3 * layer)
    lams = [v.astype(F32)[None, :] for v in (diff_lambda_q1, diff_lambda_k1, diff_lambda_q2, diff_lambda_k2)]
    t_att = min(ATT_T, seq)
    tri = np.arange(t_att)
    y_diff = _diff_attention(ab, _additive_mask(tri[None, :] <= tri[:, None]), lams,
                             diff_subln_g.astype(F32)[None, :], batch, seq, lambda_init, col)

    n_sel = seq // SEL_BLOCK
    cmp_start = np.arange(n_runs) * CMP_STRIDE
    sel_start = np.arange(n_sel) * SEL_BLOCK
    overlap = ((cmp_start[:, None] < sel_start[None, :] + SEL_BLOCK)
               & (cmp_start[:, None] + CMP_BLOCK - 1 >= sel_start[None, :]))
    ovt = np.zeros((LANES, n_runs))
    ovt[HEAD_DIM:HEAD_DIM + n_sel] = overlap.T
    cmp_bias = _additive_mask(cmp_start[None, :] + CMP_BLOCK - 1 <= np.arange(seq)[:, None])
    t_nsa = min(NSA_T, seq)
    tri = np.arange(t_nsa)
    t_win = min(WIN_T, t_nsa)
    r = np.arange(t_win)[:, None]
    c = np.arange(WINDOW + t_win)[None, :]
    window_bias = jnp.stack([_additive_mask((c <= r + off) & (r + off - c < WINDOW))
                             for off in range(0, WINDOW + 1, t_win)])
    y_nsa = _nsa_attention(ab, gates, kcmp, vcmp, jnp.asarray(ovt, BF16), cmp_bias,
                           _additive_mask(tri[None, :] <= tri[:, None]), window_bias, batch, seq, col,
                           col["ng"] - gate_base)

    assert col["mg"] == gate_base
    x1, h2 = _merge(y_diff, y_nsa, gates, x2d, w_proj_diff.astype(BF16), w_proj_nsa.astype(BF16),
                    w_out.astype(BF16), ln_mlp_g.astype(F32)[None, :])
    return _mlp(h2, x1, w_mlp_up.astype(BF16), w_mlp_down.astype(BF16))


def kernel(x, ln_mix_g, w_in, diff_q_norm_g, diff_k_norm_g, diff_lambda_q1, diff_lambda_k1, diff_lambda_q2, diff_lambda_k2, diff_subln_g, nsa_q_norm_g, nsa_k_norm_g, cmp_pos_k, cmp_pos_v, cmp_k_w1, cmp_k_w2, cmp_v_w1, cmp_v_w2, w_proj_diff, w_proj_nsa, w_out, ln_mlp_g, w_mlp_up, w_mlp_down):
    batch, seq, d = x.shape
    params = (ln_mix_g, w_in, diff_q_norm_g, diff_k_norm_g, diff_lambda_q1, diff_lambda_k1, diff_lambda_q2,
              diff_lambda_k2, diff_subln_g, nsa_q_norm_g, nsa_k_norm_g, cmp_pos_k, cmp_pos_v, cmp_k_w1,
              cmp_k_w2, cmp_v_w1, cmp_v_w2, w_proj_diff, w_proj_nsa, w_out, ln_mlp_g, w_mlp_up, w_mlp_down)
    x2d = x.reshape(batch * seq, d)
    for layer in range(ln_mix_g.shape[0]):
        x2d = _layer(x2d, batch, seq, layer, *[prm[layer] for prm in params])
    return x2d.reshape(batch, seq, d)
```

```python
import functools
import math

import numpy as np
import jax
import jax.numpy as jnp
from jax import lax
from jax.experimental import pallas as pl
from jax.experimental.pallas import tpu as pltpu

F32 = jnp.float32
BF16 = jnp.bfloat16

LANES = 128
MXU_WIDTH = 256
HEAD_DIM = 64
HALF = HEAD_DIM // 2
DIFF_HEADS = 8
NSA_HEADS = 16
NSA_GROUPS = 4
NSA_HPG = NSA_HEADS // NSA_GROUPS
CMP_BLOCK = 32
CMP_STRIDE = 16
SEL_BLOCK = 64
SEL_TOP = 16
WINDOW = 512
FORCED_SCORE = 1e4
ROPE_THETA = 10000.0
EPS = 1e-6
NEG_BLOCK = -1e9
NEG_TOKEN = -1e30

PROJ_TM = 2048
PROJ_TN = 512
PROJ_ROWS = 256
ATT_T = 512
NSA_T = 512
WIN_T = 256
MERGE_TM = 512
MLP_TM = 1024
MLP_TF = 1024
VMEM_LIMIT = 56 * 1024 * 1024

_NT = (((1,), (1,)), ((), ()))


def _dot(a, b):
    return jnp.dot(a, b, preferred_element_type=F32)


def _dot_nt(a, b):
    return lax.dot_general(a, b, _NT, preferred_element_type=F32)


def _split_bf16(x):
    hi = x.astype(BF16)
    lo = (x - hi.astype(F32)).astype(BF16)
    return hi, lo


def _swap_halves_within_heads(y):
    lane = lax.broadcasted_iota(jnp.int32, y.shape, 1)
    first_half = (lane % HEAD_DIM) < HALF
    return jnp.where(first_half, pltpu.roll(y, LANES - HALF, axis=1), pltpu.roll(y, HALF, axis=1))


def _rope(y, cos, sin_signed):
    return y * cos + _swap_halves_within_heads(y) * sin_signed


def _inproj_kernel(x_ref, g_ref, w_ref, gain_ref, cos_ref, sin_ref, bd_ref, ab_ref, gt_ref, h_scr,
                   *, n_rope, n_plain):
    j = pl.program_id(1)

    @pl.when(j == 0)
    def _():
        x = x_ref[...]
        ms = jnp.mean(x * x, axis=-1, keepdims=True)
        h_scr[...] = (x * lax.rsqrt(ms + EPS) * g_ref[...]).astype(BF16)

    wide = bd_ref.shape[0]
    row_chunks = [slice(r, r + PROJ_ROWS) for r in range(0, h_scr.shape[0], PROJ_ROWS)]

    @pl.when(j < n_rope)
    def _():
        for rs in row_chunks:
            acc = _dot(h_scr[rs, :], w_ref[...])
            cos = cos_ref[rs, :]
            sin = sin_ref[rs, :]
            for c in range(acc.shape[1] // wide):
                sl = slice(c * wide, (c + 1) * wide)
                xc = acc[:, sl]
                ms = _dot((xc * xc).astype(BF16), bd_ref[...])
                y = xc * lax.rsqrt(ms + EPS) * gain_ref[:, sl]
                for k in range(wide // LANES):
                    lo = sl.start + k * LANES
                    ab_ref[rs, lo:lo + LANES] = _rope(
                        y[:, k * LANES:(k + 1) * LANES], cos, sin).astype(ab_ref.dtype)

    @pl.when((j >= n_rope) & (j < n_rope + n_plain))
    def _():
        for rs in row_chunks:
            ab_ref[rs, :] = _dot(h_scr[rs, :], w_ref[...]).astype(ab_ref.dtype)

    @pl.when(j >= n_rope + n_plain)
    def _():
        for rs in row_chunks:
            gt_ref[rs, :] = jax.nn.sigmoid(_dot(h_scr[rs, :], w_ref[...]))


def _inproj(x2d, g, w, gain, cos, sin, bd, seq, n_rope, n_plain, n_gate):
    n, d = x2d.shape
    tm, tn = min(PROJ_TM, seq), PROJ_TN
    assert n % tm == 0 and seq % tm == 0 and w.shape[1] == (n_rope + n_plain + n_gate) * tn
    pos_blocks = seq // tm
    n_ab = n_rope + n_plain
    return pl.pallas_call(
        functools.partial(_inproj_kernel, n_rope=n_rope, n_plain=n_plain),
        out_shape=(jax.ShapeDtypeStruct((n, n_ab * tn), BF16), jax.ShapeDtypeStruct((n, n_gate * tn), F32)),
        grid=(n // tm, n_ab + n_gate),
        in_specs=[
            pl.BlockSpec((tm, d), lambda i, j: (i, 0)),
            pl.BlockSpec((1, d), lambda i, j: (0, 0)),
            pl.BlockSpec((d, tn), lambda i, j: (0, j)),
            pl.BlockSpec((1, tn), lambda i, j: (0, jnp.minimum(j, n_rope - 1))),
            pl.BlockSpec((tm, LANES), lambda i, j: (i % pos_blocks, 0)),
            pl.BlockSpec((tm, LANES), lambda i, j: (i % pos_blocks, 0)),
            pl.BlockSpec(bd.shape, lambda i, j: (0, 0)),
        ],
        out_specs=(
            pl.BlockSpec((tm, tn), lambda i, j: (i, jnp.minimum(j, n_ab - 1))),
            pl.BlockSpec((tm, tn), lambda i, j: (i, jnp.maximum(j - n_ab, 0))),
        ),
        scratch_shapes=[pltpu.VMEM((tm, d), BF16)],
        compiler_params=pltpu.CompilerParams(
            dimension_semantics=("parallel", "arbitrary"), vmem_limit_bytes=VMEM_LIMIT),
        name="inproj",
    )(x2d, g, w, gain, cos, sin, bd)


def _compress_kernel(x_ref, pos_ref, w1_ref, w2_ref, *rest, is_key):
    if is_key:
        gain_ref, cos_ref, sin_ref, o_ref, xf_ref = rest
    else:
        o_ref, xf_ref = rest
    n_runs = o_ref.shape[1]
    per_chunk = LANES // HEAD_DIM
    for c in range(xf_ref.shape[0]):
        xf_ref[c] = x_ref[:, c * LANES:(c + 1) * LANES].astype(F32)
    lane_group = lax.broadcasted_iota(jnp.int32, (n_runs, LANES), 1) // HEAD_DIM
    for g in range(NSA_GROUPS):
        halves = []
        for half in range(CMP_BLOCK // CMP_STRIDE):
            acc = None
            for tt in range(CMP_STRIDE):
                l = half * CMP_STRIDE + tt
                xt = xf_ref[g // per_chunk, pl.ds(tt, n_runs, stride=CMP_STRIDE), :] + pos_ref[l:l + 1, :]
                part = _dot(jnp.where(lane_group == g % per_chunk, xt, 0.0).astype(BF16), w1_ref[l])
                acc = part if acc is None else acc + part
            halves.append(acc)
        hidden = halves[0] + pltpu.roll(halves[1], n_runs - 1, axis=0)
        out = _dot(jax.nn.gelu(hidden).astype(BF16), w2_ref[...])
        if is_key:
            ms = jnp.mean(out * out, axis=-1, keepdims=True)
            out = _rope(out * lax.rsqrt(ms + EPS) * gain_ref[...], cos_ref[...], sin_ref[...])
        o_ref[g] = out.astype(o_ref.dtype)


def _compress(ab, col0, seq, pos, w1, w2dup, extra, is_key):
    batch = ab.shape[0] // seq
    width = NSA_GROUPS * HEAD_DIM
    n_runs = seq // CMP_STRIDE
    full = lambda arr: pl.BlockSpec(arr.shape, lambda b: (0,) * arr.ndim)
    return pl.pallas_call(
        functools.partial(_compress_kernel, is_key=is_key),
        out_shape=jax.ShapeDtypeStruct((batch, NSA_GROUPS, n_runs, LANES), BF16),
        grid=(batch,),
        in_specs=[pl.BlockSpec((seq, width), lambda b: (b, col0 // width)), full(pos), full(w1), full(w2dup)]
        + [full(e) for e in extra],
        out_specs=pl.BlockSpec((None, NSA_GROUPS, n_runs, LANES), lambda b: (b, 0, 0, 0)),
        scratch_shapes=[pltpu.VMEM((width // LANES, seq, LANES), F32)],
        compiler_params=pltpu.CompilerParams(dimension_semantics=("parallel",), vmem_limit_bytes=VMEM_LIMIT),
        name="compress_k" if is_key else "compress_v",
    )(ab, pos, w1, w2dup, *extra)


def _online_update(s, v_ones, m_ref, acc_ref, idx):
    m_prev = m_ref[idx]
    m_next = jnp.maximum(m_prev, jnp.max(s, axis=1, keepdims=True))
    alpha = jnp.exp2(m_prev - m_next)
    p = jnp.exp2(s - jnp.concatenate([m_next] * (s.shape[1] // LANES), axis=1))
    m_ref[idx] = m_next
    scale = jnp.concatenate([alpha] * (acc_ref.shape[-1] // LANES), axis=1)
    acc_ref[idx] = acc_ref[idx] * scale + _dot(p.astype(BF16), v_ones)


def _init_softmax_state(m_ref, acc_ref):
    m_ref[...] = jnp.full(m_ref.shape, -jnp.inf, F32)
    acc_ref[...] = jnp.zeros(acc_ref.shape, F32)


def _add_per_head(s, bias, heads):
    rows, cols = bias.shape
    return (s.reshape(heads, rows, cols) + bias[None]).reshape(heads * rows, cols)


def _diff_attn_kernel(q_ref, k_ref, v_ref, cb_ref, lq1_ref, lk1_ref, lq2_ref, lk2_ref, sg_ref, o_ref,
                      m_ref, acc_ref, vones_ref, s_ref, *, lambda_init):
    i = pl.program_id(2)
    t = q_ref.shape[0]
    dv = v_ref.shape[1]

    @pl.when(i == 0)
    def _():
        vones_ref[:, :dv] = v_ref[...]
        vones_ref[:, dv:] = jnp.ones((v_ref.shape[0], vones_ref.shape[1] - dv), BF16)

    q = q_ref[...]
    lane = lax.broadcasted_iota(jnp.int32, q.shape, 1)
    zero = jnp.zeros_like(q)
    qc = (jnp.where(lane < HEAD_DIM, q, zero), jnp.where(lane >= HEAD_DIM, q, zero))
    _init_softmax_state(m_ref, acc_ref)

    def scores(j, slot):
        k = k_ref[pl.ds(pl.multiple_of(j * t, t), t), :]
        for c in range(2):
            s_ref[slot, c] = _dot_nt(qc[c], k)

    def update(j, slot, masked):
        v = vones_ref[pl.ds(pl.multiple_of(j * t, t), t), :]
        for c in range(2):
            s = s_ref[slot, c]
            if masked:
                s = s + cb_ref[...]
            _online_update(s, v, m_ref, acc_ref, c)

    scores(0, 0)

    def body(pp, carry):
        scores(2 * pp + 1, 1)
        update(2 * pp, 0, False)
        scores(2 * pp + 2, 0)
        update(2 * pp + 1, 1, False)
        return carry

    lax.fori_loop(0, i // 2, body, 0)

    @pl.when(i % 2 == 0)
    def _():
        update(i, 0, True)

    @pl.when(i % 2 == 1)
    def _():
        scores(i, 1)
        update(i - 1, 0, False)
        update(i, 1, True)

    lam = (jnp.exp(jnp.sum(lq1_ref[...] * lk1_ref[...], axis=-1, keepdims=True))
           - jnp.exp(jnp.sum(lq2_ref[...] * lk2_ref[...], axis=-1, keepdims=True)) + lambda_init)
    o = acc_ref[0, :, :dv] / acc_ref[0, :, dv:] - lam * (acc_ref[1, :, :dv] / acc_ref[1, :, dv:])
    ms = jnp.mean(o * o, axis=-1, keepdims=True)
    o_ref[...] = (o * lax.rsqrt(ms + EPS) * sg_ref[...] * (1.0 - lambda_init)).astype(o_ref.dtype)


def _diff_attention(ab, causal_bias, lams, subln_g, batch, seq, lambda_init, col):
    t = causal_bias.shape[0]
    nq = seq // t
    dv = 2 * HEAD_DIM
    small = lambda arr: pl.BlockSpec(arr.shape, lambda b, h, i: (0, 0))
    return pl.pallas_call(
        functools.partial(_diff_attn_kernel, lambda_init=lambda_init),
        out_shape=jax.ShapeDtypeStruct((batch * seq, DIFF_HEADS * dv), BF16),
        grid=(batch, DIFF_HEADS, nq),
        in_specs=[
            pl.BlockSpec((t, LANES), lambda b, h, i: (b * nq + i, col["dq"] // LANES + h)),
            pl.BlockSpec((seq, LANES), lambda b, h, i: (b, col["dk"] // LANES + h)),
            pl.BlockSpec((seq, dv), lambda b, h, i: (b, col["dv"] // dv + h)),
            small(causal_bias),
            small(lams[0]), small(lams[1]), small(lams[2]), small(lams[3]), small(subln_g),
        ],
        out_specs=pl.BlockSpec((t, dv), lambda b, h, i: (b * nq + i, h)),
        scratch_shapes=[
            pltpu.VMEM((2, t, LANES), F32),
            pltpu.VMEM((2, t, 2 * dv), F32),
            pltpu.VMEM((seq, 2 * dv), BF16),
            pltpu.VMEM((2, 2, t, t), F32),
        ],
        compiler_params=pltpu.CompilerParams(
            dimension_semantics=("parallel", "parallel", "arbitrary"), vmem_limit_bytes=VMEM_LIMIT),
        name="diff_attention",
    )(ab, ab, ab, causal_bias, *lams, subln_g)


def _nsa_kernel(q_ref, ksl_ref, vsl_ref, kwn_ref, vwn_ref, kc_ref, vc_ref, gate_ref, ovt_ref, cmpb_ref, cb_ref,
                wb_ref, o_ref, qaug_ref, m_ref, acc_ref, accw_ref, imp_ref, kaug_ref, vsl1_ref, vwn1_ref,
                q4_ref, part_ref, rank_ref, s_ref):
    i = pl.program_id(2)
    t = q_ref.shape[0]
    seq = ksl_ref.shape[0]
    q0 = i * t
    n_sel = seq // SEL_BLOCK
    assert n_sel <= LANES - HEAD_DIM and imp_ref.shape[0] == LANES
    lane = lax.broadcasted_iota(jnp.int32, (t, LANES), 1)
    low = lane < HEAD_DIM

    @pl.when(i == 0)
    def _():
        row = lax.broadcasted_iota(jnp.int32, (seq, LANES), 0)
        ln = lax.broadcasted_iota(jnp.int32, (seq, LANES), 1)
        kaug_ref[...] = ksl_ref[...] + jnp.where(ln - HEAD_DIM == row // SEL_BLOCK, 1.0, 0.0).astype(BF16)
        ones_hi = jnp.where(ln >= HEAD_DIM, 1.0, 0.0).astype(BF16)
        vsl1_ref[...] = vsl_ref[...] + ones_hi
        vwn1_ref[...] = vwn_ref[...] + ones_hi

    q = q_ref[...].astype(F32)
    heads = []
    for pair in range(NSA_HPG // 2):
        qp = q[:, pair * LANES:(pair + 1) * LANES]
        heads += [qp, pltpu.roll(qp, HEAD_DIM, axis=1)]

    for hh, hq in enumerate(heads):
        q4_ref[hh] = jnp.where(low, hq, 0.0).astype(BF16)
    rows = NSA_HPG * t

    tw = wb_ref.shape[1]
    wk = wb_ref.shape[2]
    assert wk == WINDOW + tw and WINDOW % tw == 0 and t % tw == 0
    for sub in range(t // tw):
        qs = q4_ref[:, sub * tw:(sub + 1) * tw, :].reshape(NSA_HPG * tw, LANES)
        sub_idx = i * (t // tw) + sub
        kstart = pl.multiple_of(jnp.maximum(q0 + sub * tw - WINDOW, 0), tw)
        s = _dot_nt(qs, kwn_ref[pl.ds(kstart, wk), :])
        s = _add_per_head(s, wb_ref[jnp.minimum(sub_idx, WINDOW // tw)], NSA_HPG)
        e = jnp.exp2(s - jnp.max(s, axis=1, keepdims=True))
        acc_w =_dot(e.astype(BF16), vwn1_ref[pl.ds(kstart, wk), :])
        accw_ref[:, sub * tw:(sub + 1) * tw, :] = acc_w.reshape(NSA_HPG, tw, LANES)

    s = _add_per_head(_dot_nt(q4_ref[...].reshape(rows, LANES), kc_ref[...]), cmpb_ref[...], NSA_HPG)
    e = jnp.exp2(s - jnp.max(s, axis=1, keepdims=True))
    p = e / jnp.sum(e, axis=1, keepdims=True)
    o_cmp = _dot(p.astype(BF16), vc_ref[...])

    denom = jnp.ones((t, LANES), F32)
    for hh in range(NSA_HPG):
        denom = jnp.where(lane == HEAD_DIM + 3 * hh + 2, accw_ref[hh], denom)
    coef = gate_ref[...] / denom
    has_cmp = q0 + lax.broadcasted_iota(jnp.int32, (t, 1), 0) >= CMP_BLOCK - 1
    for hh in range(NSA_HPG):
        c0 = HEAD_DIM + 3 * hh
        part_ref[pl.ds(hh * t, t), :] = (jnp.where(has_cmp, coef[:, c0:c0 + 1], 0.0) * o_cmp[hh * t:(hh + 1) * t]
                                         + coef[:, c0 + 2:c0 + 3] * accw_ref[hh])

    psum = p[0:t] + p[t:2 * t] + p[2 * t:3 * t] + p[3 * t:4 * t]
    hi, lo = _split_bf16(psum)
    ovt = ovt_ref[...]
    imp = _dot_nt(ovt, hi) + _dot_nt(ovt, lo)
    jrow = lax.broadcasted_iota(jnp.int32, imp.shape, 0) - SEL_BLOCK
    qcol = q0 + lax.broadcasted_iota(jnp.int32, imp.shape, 1)
    cur = qcol // SEL_BLOCK
    forced = (jrow == 0) | (jrow == cur) | (jrow == cur - 1)
    imp = jnp.where(forced, FORCED_SCORE, imp)
    imp = jnp.where((jrow >= 0) & (jrow * SEL_BLOCK <= qcol), imp, -jnp.inf)
    imp_ref[...] = imp
    imp_b = imp[SEL_BLOCK:, :]
    jb = lax.broadcasted_iota(jnp.int32, imp_b.shape, 0)
    per_tile = t // SEL_BLOCK
    rank_ref[...] = jnp.zeros(rank_ref.shape, F32)

    def count_group(g8):
        rank = rank_ref[...]
        for kk in range(g8 * per_tile, (g8 + 1) * per_tile):
            rk = imp_ref[SEL_BLOCK + kk:SEL_BLOCK + kk + 1, :]
            rank = rank + jnp.where(rk > imp_b, 1.0, jnp.where((rk == imp_b) & (jb > kk), 1.0, 0.0))
        rank_ref[...] = rank

    count_group(0)
    for g8 in range(1, n_sel // per_tile):
        pl.when(i >= g8)(functools.partial(count_group, g8))
    chosen = (rank_ref[...] < float(min(SEL_TOP, n_sel))) & (imp_b > -jnp.inf)
    bias_t = jnp.concatenate([jnp.zeros((SEL_BLOCK, t), F32), jnp.where(chosen, 0.0, NEG_BLOCK)], axis=0)
    bias = bias_t.T
    for hh, hq in enumerate(heads):
        qaug_ref[hh] = jnp.where(low, hq, bias).astype(BF16)

    _init_softmax_state(m_ref, acc_ref)

    def scores(j, slot):
        start = pl.multiple_of(j * t, t)
        s_ref[slot] = _dot_nt(qaug_ref[...].reshape(rows, LANES), kaug_ref[pl.ds(start, t), :])

    def update(j, slot, masked):
        s = s_ref[slot]
        if masked:
            s = _add_per_head(s, cb_ref[...], NSA_HPG)
        _online_update(s, vsl1_ref[pl.ds(pl.multiple_of(j * t, t), t), :], m_ref, acc_ref, 0)

    scores(0, 0)

    def slc_body(pp, carry):
        scores(2 * pp + 1, 1)
        update(2 * pp, 0, False)
        scores(2 * pp + 2, 0)
        update(2 * pp + 1, 1, False)
        return carry

    lax.fori_loop(0, i // 2, slc_body, 0)

    @pl.when(i % 2 == 0)
    def _():
        update(i, 0, True)

    @pl.when(i % 2 == 1)
    def _():
        scores(i, 1)
        update(i - 1, 0, False)
        update(i, 1, True)

    denom = jnp.ones((t, LANES), F32)
    for hh in range(NSA_HPG):
        denom = jnp.where(lane == HEAD_DIM + 3 * hh + 1, acc_ref[0, pl.ds(hh * t, t), :], denom)
    coef = gate_ref[...] / denom
    outs = []
    for hh in range(NSA_HPG):
        c1 = HEAD_DIM + 3 * hh + 1
        outs.append(part_ref[pl.ds(hh * t, t), :] + coef[:, c1:c1 + 1] * acc_ref[0, pl.ds(hh * t, t), :])
    for pair in range(NSA_HPG // 2):
        o_ref[:, pair * LANES:(pair + 1) * LANES] = jnp.where(
            low, outs[2 * pair], pltpu.roll(outs[2 * pair + 1], HEAD_DIM, axis=1)).astype(o_ref.dtype)


def _nsa_attention(ab, gates, kcmp, vcmp, ovt, cmp_bias, causal_bias, window_bias, batch, seq, col, col_gate):
    t = causal_bias.shape[0]
    nq = seq // t
    gw = NSA_HPG * HEAD_DIM
    rows = NSA_HPG * t
    n_cmp = kcmp.shape[2]
    full = lambda arr: pl.BlockSpec(arr.shape, lambda b, g, i: (0,) * arr.ndim)
    kv = lambda name: pl.BlockSpec((seq, LANES), lambda b, g, i: (b, col[name] // LANES + g))
    return pl.pallas_call(
        _nsa_kernel,
        out_shape=jax.ShapeDtypeStruct((batch * seq, NSA_HEADS * HEAD_DIM), BF16),
        grid=(batch, NSA_GROUPS, nq),
        in_specs=[
            pl.BlockSpec((t, gw), lambda b, g, i: (b * nq + i, col["nq"] // gw + g)),
            kv("ksl"), kv("vsl"), kv("kwn"), kv("vwn"),
            pl.BlockSpec((None, None, n_cmp, LANES), lambda b, g, i: (b, g, 0, 0)),
            pl.BlockSpec((None, None, n_cmp, LANES), lambda b, g, i: (b, g, 0, 0)),
            pl.BlockSpec((t, LANES), lambda b, g, i: (b * nq + i, col_gate // LANES + g)),
            full(ovt),
            pl.BlockSpec((t, n_cmp), lambda b, g, i: (i, 0)),
            full(causal_bias),
            full(window_bias),
        ],
        out_specs=pl.BlockSpec((t, gw), lambda b, g, i: (b * nq + i, g)),
        scratch_shapes=[
            pltpu.VMEM((NSA_HPG, t, LANES), BF16),
            pltpu.VMEM((1, rows, LANES), F32),
            pltpu.VMEM((1, rows, LANES), F32),
            pltpu.VMEM((NSA_HPG, t, LANES), F32),
            pltpu.VMEM((LANES, t), F32),
            pltpu.VMEM((seq, LANES), BF16),
            pltpu.VMEM((seq, LANES), BF16),
            pltpu.VMEM((seq, LANES), BF16),
            pltpu.VMEM((NSA_HPG, t, LANES), BF16),
            pltpu.VMEM((rows, LANES), F32),
            pltpu.VMEM((SEL_BLOCK, t), F32),
            pltpu.VMEM((2, rows, t), F32),
        ],
        compiler_params=pltpu.CompilerParams(
            dimension_semantics=("parallel", "parallel", "arbitrary"), vmem_limit_bytes=VMEM_LIMIT),
        name="nsa_attention",
    )(ab, ab, ab, ab, ab, kcmp, vcmp, gates, ovt, cmp_bias, causal_bias, window_bias)


def _merge_kernel(yd_ref, yn_ref, g0_ref, g1_ref, x_ref, wd_ref, wn_ref, wo_ref, lg_ref, x1_ref, h_ref):
    mixed = g0_ref[...] * _dot(yd_ref[...], wd_ref[...]) + g1_ref[...] * _dot(yn_ref[...], wn_ref[...])
    x1 = x_ref[...] + _dot(mixed.astype(BF16), wo_ref[...])
    x1_ref[...] = x1
    ms = jnp.mean(x1 * x1, axis=-1, keepdims=True)
    h_ref[...] = (x1 * lax.rsqrt(ms + EPS) * lg_ref[...]).astype(h_ref.dtype)


def _merge(yd, yn, gates, x2d, wd, wn, wo, lg):
    n, d = x2d.shape
    tm = min(MERGE_TM, n)
    tok = lambda c: pl.BlockSpec((tm, d), lambda i: (i, c))
    full = lambda arr: pl.BlockSpec(arr.shape, lambda i: (0, 0))
    return pl.pallas_call(
        _merge_kernel,
        out_shape=(jax.ShapeDtypeStruct((n, d), F32), jax.ShapeDtypeStruct((n, d), BF16)),
        grid=(n // tm,),
        in_specs=[tok(0), tok(0), tok(0), tok(1), tok(0), full(wd), full(wn), full(wo), full(lg)],
        out_specs=(tok(0), tok(0)),
        compiler_params=pltpu.CompilerParams(
            dimension_semantics=("parallel",), vmem_limit_bytes=VMEM_LIMIT),
        name="merge_outproj",
    )(yd, yn, gates, gates, x2d, wd, wn, wo, lg)


def _mlp_kernel(h_ref, x1_ref, wu_ref, wd_ref, o_ref, acc_ref):
    f = pl.program_id(1)

    @pl.when(f == 0)
    def _():
        acc_ref[...] = x1_ref[...]

    up = jnp.maximum(_dot(h_ref[...], wu_ref[...]), 0.0)
    acc_ref[...] += _dot((up * up).astype(BF16), wd_ref[...])

    @pl.when(f == pl.num_programs(1) - 1)
    def _():
        o_ref[...] = acc_ref[...]


def _mlp(h, x1, wu, wd):
    n, d = x1.shape
    dff = wu.shape[1]
    tm, tf = min(MLP_TM, n), MLP_TF
    return pl.pallas_call(
        _mlp_kernel,
        out_shape=jax.ShapeDtypeStruct((n, d), F32),
        grid=(n // tm, dff // tf),
        in_specs=[
            pl.BlockSpec((tm, d), lambda i, f: (i, 0)),
            pl.BlockSpec((tm, d), lambda i, f: (i, 0)),
            pl.BlockSpec((d, tf), lambda i, f: (0, f)),
            pl.BlockSpec((tf, d), lambda i, f: (f, 0)),
        ],
        out_specs=pl.BlockSpec((tm, d), lambda i, f: (i, 0)),
        scratch_shapes=[pltpu.VMEM((tm, d), F32)],
        compiler_params=pltpu.CompilerParams(
            dimension_semantics=("parallel", "arbitrary"), vmem_limit_bytes=VMEM_LIMIT),
        name="mlp",
    )(h, x1, wu, wd)


def _rope_tables(pos):
    inv_freq = ROPE_THETA ** (-jnp.arange(HALF, dtype=F32) / HALF)
    ang = pos.astype(F32)[:, None] * inv_freq[None, :]
    c, s = jnp.cos(ang), jnp.sin(ang)
    return jnp.concatenate([c, c, c, c], axis=-1), jnp.concatenate([-s, s, -s, s], axis=-1)


def _pad_groups(w, width, offset=0):
    d = w.shape[0]
    w = w.reshape(d, NSA_GROUPS, width)
    return jnp.pad(w, ((0, 0), (0, 0), (offset, LANES - width - offset))).reshape(d, NSA_GROUPS * LANES)


def _additive_mask(valid):
    return jnp.asarray(np.where(valid, 0.0, NEG_TOKEN), F32)


def _layer(x2d, batch, seq, layer, ln_mix_g, w_in, diff_q_norm_g, diff_k_norm_g, diff_lambda_q1,
           diff_lambda_k1, diff_lambda_q2, diff_lambda_k2, diff_subln_g, nsa_q_norm_g, nsa_k_norm_g,
           cmp_pos_k, cmp_pos_v, cmp_k_w1, cmp_k_w2, cmp_v_w1, cmp_v_w2, w_proj_diff, w_proj_nsa,
           w_out, ln_mlp_g, w_mlp_up, w_mlp_down):
    d = x2d.shape[1]
    diff_qk = DIFF_HEADS * 2 * HEAD_DIM
    diff_v = DIFF_HEADS * 2 * HEAD_DIM
    nsa_q = NSA_HEADS * HEAD_DIM
    nsa_kv = NSA_GROUPS * HEAD_DIM
    splits = np.cumsum([diff_qk, diff_qk, diff_v, nsa_q] + [nsa_kv] * 6 + [NSA_HEADS * 3, 2 * d])[:-1]
    (w_dq, w_dk, w_dv, w_nq, w_kc, w_vc, w_ksl, w_vsl, w_kwn, w_vwn, w_ng, w_mg) = jnp.split(
        w_in, [int(c) for c in splits], axis=1)
    scale = HEAD_DIM ** -0.5 * math.log2(math.e)
    pad_kv = lambda w: _pad_groups(w, HEAD_DIM)

    fam_rope = [("dq", w_dq), ("dk", w_dk), ("nq", w_nq), ("ksl", pad_kv(w_ksl)), ("kwn", pad_kv(w_kwn))]
    fam_plain = [("dv", w_dv), ("vsl", pad_kv(w_vsl)), ("vwn", pad_kv(w_vwn)), ("kc", w_kc), ("vc", w_vc)]
    fam_gate = [("mg", w_mg), ("ng", _pad_groups(w_ng, NSA_HPG * 3, HEAD_DIM))]
    col, blocks, tiles = {}, [], []
    for fam in (fam_rope, fam_plain, fam_gate):
        start = sum(b.shape[1] for b in blocks)
        for name, w in fam:
            col[name] = sum(b.shape[1] for b in blocks)
            blocks.append(w)
        width = sum(b.shape[1] for b in blocks) - start
        pad = -width % PROJ_TN
        if pad:
            blocks.append(jnp.zeros((d, pad), w_in.dtype))
        tiles.append((width + pad) // PROJ_TN)
    w_all = jnp.concatenate(blocks, axis=1).astype(BF16)
    gate_base = (tiles[0] + tiles[1]) * PROJ_TN

    tile_g = lambda g, reps: jnp.tile(g.astype(F32), reps)
    gain = jnp.concatenate([
        tile_g(diff_q_norm_g, 2 * DIFF_HEADS) * scale,
        tile_g(diff_k_norm_g, 2 * DIFF_HEADS),
        tile_g(nsa_q_norm_g, NSA_HEADS) * scale,
        tile_g(nsa_k_norm_g[1], 2 * NSA_GROUPS),
        tile_g(nsa_k_norm_g[2], 2 * NSA_GROUPS),
    ])[None, :]
    assert gain.shape[1] == tiles[0] * PROJ_TN
    cos, sin = _rope_tables(jnp.arange(seq))
    mean_heads = np.kron(np.eye(MXU_WIDTH // HEAD_DIM), np.full((HEAD_DIM, HEAD_DIM), 1.0 / HEAD_DIM))
    ab, gates = _inproj(x2d, ln_mix_g.astype(F32)[None, :], w_all, gain, cos, sin,
                        jnp.asarray(mean_heads, BF16), seq, *tiles)

    n_runs = seq // CMP_STRIDE
    dup = lambda w: jnp.concatenate([w, w], axis=1).astype(BF16)
    reps = LANES // HEAD_DIM
    per_group = lambda pos: jnp.tile(pos.astype(F32), (1, reps))
    rows_per_group = lambda w1: jnp.tile(
        w1.reshape(CMP_BLOCK, HEAD_DIM, w1.shape[1]), (1, reps, 1)).astype(BF16)
    cmp_center = jnp.arange(n_runs) * CMP_STRIDE + (CMP_BLOCK - 1) / 2.0
    cos_c, sin_c = _rope_tables(cmp_center)
    gain_c = tile_g(nsa_k_norm_g[0], 2)[None, :]
    kcmp = _compress(ab, col["kc"], seq, per_group(cmp_pos_k), rows_per_group(cmp_k_w1), dup(cmp_k_w2),
                     (gain_c, cos_c, sin_c), True)
    vcmp = _compress(ab, col["vc"], seq, per_group(cmp_pos_v), rows_per_group(cmp_v_w1), dup(cmp_v_w2),
                     (), False)

    lambda_init = 0.8 - 0.6 * math.exp(-0.3 * layer)
    lams = [v.astype(F32)[None, :] for v in (diff_lambda_q1, diff_lambda_k1, diff_lambda_q2, diff_lambda_k2)]
    t_att = min(ATT_T, seq)
    tri = np.arange(t_att)
    y_diff = _diff_attention(ab, _additive_mask(tri[None, :] <= tri[:, None]), lams,
                             diff_subln_g.astype(F32)[None, :], batch, seq, lambda_init, col)

    n_sel = seq // SEL_BLOCK
    cmp_start = np.arange(n_runs) * CMP_STRIDE
    sel_start = np.arange(n_sel) * SEL_BLOCK
    overlap = ((cmp_start[:, None] < sel_start[None, :] + SEL_BLOCK)
               & (cmp_start[:, None] + CMP_BLOCK - 1 >= sel_start[None, :]))
    ovt = np.zeros((LANES, n_runs))
    ovt[HEAD_DIM:HEAD_DIM + n_sel] = overlap.T
    cmp_bias = _additive_mask(cmp_start[None, :] + CMP_BLOCK - 1 <= np.arange(seq)[:, None])
    t_nsa = min(NSA_T, seq)
    tri = np.arange(t_nsa)
    t_win = min(WIN_T, t_nsa)
    r = np.arange(t_win)[:, None]
    c = np.arange(WINDOW + t_win)[None, :]
    window_bias = jnp.stack([_additive_mask((c <= r + off) & (r + off - c < WINDOW))
                             for off in range(0, WINDOW + 1, t_win)])
    y_nsa = _nsa_attention(ab, gates, kcmp, vcmp, jnp.asarray(ovt, BF16), cmp_bias,
                           _additive_mask(tri[None, :] <= tri[:, None]), window_bias, batch, seq, col,
                           col["ng"] - gate_base)

    assert col["mg"] == gate_base
    x1, h2 = _merge(y_diff, y_nsa, gates, x2d, w_proj_diff.astype(BF16), w_proj_nsa.astype(BF16),
                    w_out.astype(BF16), ln_mlp_g.astype(F32)[None, :])
    return _mlp(h2, x1, w_mlp_up.astype(BF16), w_mlp_down.astype(BF16))


def kernel(x, ln_mix_g, w_in, diff_q_norm_g, diff_k_norm_g, diff_lambda_q1, diff_lambda_k1, diff_lambda_q2, diff_lambda_k2, diff_subln_g, nsa_q_norm_g, nsa_k_norm_g, cmp_pos_k, cmp_pos_v, cmp_k_w1, cmp_k_w2, cmp_v_w1, cmp_v_w2, w_proj_diff, w_proj_nsa, w_out, ln_mlp_g, w_mlp_up, w_mlp_down):
    batch, seq, d = x.shape
    params = (ln_mix_g, w_in, diff_q_norm_g, diff_k_norm_g, diff_lambda_q1, diff_lambda_k1, diff_lambda_q2,
              diff_lambda_k2, diff_subln_g, nsa_q_norm_g, nsa_k_norm_g, cmp_pos_k, cmp_pos_v, cmp_k_w1,
              cmp_k_w2, cmp_v_w1, cmp_v_w2, w_proj_diff, w_proj_nsa, w_out, ln_mlp_g, w_mlp_up, w_mlp_down)
    x2d = x.reshape(batch * seq, d)
    for layer in range(ln_mix_g.shape[0]):
        x2d = _layer(x2d, batch, seq, layer, *[prm[layer] for prm in params])
    return x2d.reshape(batch, seq, d)
```

```python
import functools
import math

import numpy as np
import jax
import jax.numpy as jnp
from jax import lax
from jax.experimental import pallas as pl
from jax.experimental.pallas import tpu as pltpu

F32 = jnp.float32
BF16 = jnp.bfloat16

LANES = 128
MXU_WIDTH = 256
HEAD_DIM = 64
HALF = HEAD_DIM // 2
DIFF_HEADS = 8
NSA_HEADS = 16
NSA_GROUPS = 4
NSA_HPG = NSA_HEADS // NSA_GROUPS
CMP_BLOCK = 32
CMP_STRIDE = 16
SEL_BLOCK = 64
SEL_TOP = 16
WINDOW = 512
FORCED_SCORE = 1e4
ROPE_THETA = 10000.0
EPS = 1e-6
NEG_BLOCK = -1e9
NEG_TOKEN = -1e30

PROJ_TM = 2048
PROJ_TN = 512
PROJ_ROWS = 256
ATT_T = 512
DIFF_HEADS_PER_STEP = 2
NSA_T = 512
WIN_T = 256
MERGE_TM = 512
MLP_TM = 1024
MLP_TF = 1024
VMEM_LIMIT = 56 * 1024 * 1024

_NT = (((1,), (1,)), ((), ()))


def _dot(a, b):
    return jnp.dot(a, b, preferred_element_type=F32)


def _dot_nt(a, b):
    return lax.dot_general(a, b, _NT, preferred_element_type=F32)


def _split_bf16(x):
    hi = x.astype(BF16)
    lo = (x - hi.astype(F32)).astype(BF16)
    return hi, lo


def _swap_halves_within_heads(y):
    lane = lax.broadcasted_iota(jnp.int32, y.shape, 1)
    first_half = (lane % HEAD_DIM) < HALF
    return jnp.where(first_half, pltpu.roll(y, LANES - HALF, axis=1), pltpu.roll(y, HALF, axis=1))


def _rope(y, cos, sin_signed):
    return y * cos + _swap_halves_within_heads(y) * sin_signed


def _inproj_kernel(x_ref, g_ref, w_ref, gain_ref, cos_ref, sin_ref, bd_ref, ab_ref, gt_ref, h_scr,
                   *, n_rope, n_plain):
    j = pl.program_id(1)

    @pl.when(j == 0)
    def _():
        x = x_ref[...]
        ms = jnp.mean(x * x, axis=-1, keepdims=True)
        h_scr[...] = (x * lax.rsqrt(ms + EPS) * g_ref[...]).astype(BF16)

    wide = bd_ref.shape[0]
    row_chunks = [slice(r, r + PROJ_ROWS) for r in range(0, h_scr.shape[0], PROJ_ROWS)]

    @pl.when(j < n_rope)
    def _():
        for rs in row_chunks:
            acc = _dot(h_scr[rs, :], w_ref[...])
            cos = cos_ref[rs, :]
            sin = sin_ref[rs, :]
            for c in range(acc.shape[1] // wide):
                sl = slice(c * wide, (c + 1) * wide)
                xc = acc[:, sl]
                ms = _dot((xc * xc).astype(BF16), bd_ref[...])
                y = xc * lax.rsqrt(ms + EPS) * gain_ref[:, sl]
                for k in range(wide // LANES):
                    lo = sl.start + k * LANES
                    ab_ref[rs, lo:lo + LANES] = _rope(
                        y[:, k * LANES:(k + 1) * LANES], cos, sin).astype(ab_ref.dtype)

    @pl.when((j >= n_rope) & (j < n_rope + n_plain))
    def _():
        for rs in row_chunks:
            ab_ref[rs, :] = _dot(h_scr[rs, :], w_ref[...]).astype(ab_ref.dtype)

    @pl.when(j >= n_rope + n_plain)
    def _():
        for rs in row_chunks:
            gt_ref[rs, :] = jax.nn.sigmoid(_dot(h_scr[rs, :], w_ref[...]))


def _inproj(x2d, g, w, gain, cos, sin, bd, seq, n_rope, n_plain, n_gate):
    n, d = x2d.shape
    tm, tn = min(PROJ_TM, seq), PROJ_TN
    assert n % tm == 0 and seq % tm == 0 and w.shape[1] == (n_rope + n_plain + n_gate) * tn
    pos_blocks = seq // tm
    n_ab = n_rope + n_plain
    return pl.pallas_call(
        functools.partial(_inproj_kernel, n_rope=n_rope, n_plain=n_plain),
        out_shape=(jax.ShapeDtypeStruct((n, n_ab * tn), BF16), jax.ShapeDtypeStruct((n, n_gate * tn), F32)),
        grid=(n // tm, n_ab + n_gate),
        in_specs=[
            pl.BlockSpec((tm, d), lambda i, j: (i, 0)),
            pl.BlockSpec((1, d), lambda i, j: (0, 0)),
            pl.BlockSpec((d, tn), lambda i, j: (0, j)),
            pl.BlockSpec((1, tn), lambda i, j: (0, jnp.minimum(j, n_rope - 1))),
            pl.BlockSpec((tm, LANES), lambda i, j: (i % pos_blocks, 0)),
            pl.BlockSpec((tm, LANES), lambda i, j: (i % pos_blocks, 0)),
            pl.BlockSpec(bd.shape, lambda i, j: (0, 0)),
        ],
        out_specs=(
            pl.BlockSpec((tm, tn), lambda i, j: (i, jnp.minimum(j, n_ab - 1))),
            pl.BlockSpec((tm, tn), lambda i, j: (i, jnp.maximum(j - n_ab, 0))),
        ),
        scratch_shapes=[pltpu.VMEM((tm, d), BF16)],
        compiler_params=pltpu.CompilerParams(
            dimension_semantics=("parallel", "arbitrary"), vmem_limit_bytes=VMEM_LIMIT),
        name="inproj",
    )(x2d, g, w, gain, cos, sin, bd)


def _compress_kernel(x_ref, pos_ref, w1_ref, w2_ref, *rest, is_key):
    if is_key:
        gain_ref, cos_ref, sin_ref, o_ref, xf_ref = rest
    else:
        o_ref, xf_ref = rest
    n_runs = o_ref.shape[1]
    per_chunk = LANES // HEAD_DIM
    for c in range(xf_ref.shape[0]):
        xf_ref[c] = x_ref[:, c * LANES:(c + 1) * LANES].astype(F32)
    lane_group = lax.broadcasted_iota(jnp.int32, (n_runs, LANES), 1) // HEAD_DIM
    for g in range(NSA_GROUPS):
        halves = []
        for half in range(CMP_BLOCK // CMP_STRIDE):
            acc = None
            for tt in range(CMP_STRIDE):
                l = half * CMP_STRIDE + tt
                xt = xf_ref[g // per_chunk, pl.ds(tt, n_runs, stride=CMP_STRIDE), :] + pos_ref[l:l + 1, :]
                part = _dot(jnp.where(lane_group == g % per_chunk, xt, 0.0).astype(BF16), w1_ref[l])
                acc = part if acc is None else acc + part
            halves.append(acc)
        hidden = halves[0] + pltpu.roll(halves[1], n_runs - 1, axis=0)
        out = _dot(jax.nn.gelu(hidden).astype(BF16), w2_ref[...])
        if is_key:
            ms = jnp.mean(out * out, axis=-1, keepdims=True)
            out = _rope(out * lax.rsqrt(ms + EPS) * gain_ref[...], cos_ref[...], sin_ref[...])
        o_ref[g] = out.astype(o_ref.dtype)


def _compress(ab, col0, seq, pos, w1, w2dup, extra, is_key):
    batch = ab.shape[0] // seq
    width = NSA_GROUPS * HEAD_DIM
    n_runs = seq // CMP_STRIDE
    full = lambda arr: pl.BlockSpec(arr.shape, lambda b: (0,) * arr.ndim)
    return pl.pallas_call(
        functools.partial(_compress_kernel, is_key=is_key),
        out_shape=jax.ShapeDtypeStruct((batch, NSA_GROUPS, n_runs, LANES), BF16),
        grid=(batch,),
        in_specs=[pl.BlockSpec((seq, width), lambda b: (b, col0 // width)), full(pos), full(w1), full(w2dup)]
        + [full(e) for e in extra],
        out_specs=pl.BlockSpec((None, NSA_GROUPS, n_runs, LANES), lambda b: (b, 0, 0, 0)),
        scratch_shapes=[pltpu.VMEM((width // LANES, seq, LANES), F32)],
        compiler_params=pltpu.CompilerParams(dimension_semantics=("parallel",), vmem_limit_bytes=VMEM_LIMIT),
        name="compress_k" if is_key else "compress_v",
    )(ab, pos, w1, w2dup, *extra)


def _online_update(s, v_ones, m_ref, acc_ref, idx):
    m_prev = m_ref[idx]
    m_next = jnp.maximum(m_prev, jnp.max(s, axis=1, keepdims=True))
    alpha = jnp.exp2(m_prev - m_next)
    p = jnp.exp2(s - jnp.concatenate([m_next] * (s.shape[1] // LANES), axis=1))
    m_ref[idx] = m_next
    scale = jnp.concatenate([alpha] * (acc_ref.shape[-1] // LANES), axis=1)
    acc_ref[idx] = acc_ref[idx] * scale + _dot(p.astype(BF16), v_ones)


def _init_softmax_state(m_ref, acc_ref):
    m_ref[...] = jnp.full(m_ref.shape, -jnp.inf, F32)
    acc_ref[...] = jnp.zeros(acc_ref.shape, F32)


def _add_per_head(s, bias, heads):
    rows, cols = bias.shape
    return (s.reshape(heads, rows, cols) + bias[None]).reshape(heads * rows, cols)


def _diff_attn_kernel(q_ref, k_ref, v_ref, cb_ref, lq1_ref, lk1_ref, lq2_ref, lk2_ref, sg_ref, o_ref,
                      m_ref, acc_ref, vones_ref, s_ref, *, lambda_init):
    i = pl.program_id(2)
    t = q_ref.shape[0]
    dv = 2 * HEAD_DIM
    n_heads = q_ref.shape[1] // LANES
    chunk = lambda hd: slice(hd * LANES, (hd + 1) * LANES)

    @pl.when(i == 0)
    def _():
        for hd in range(n_heads):
            vones_ref[hd, :, :dv] = v_ref[:, chunk(hd)]
            vones_ref[hd, :, dv:] = jnp.ones((v_ref.shape[0], vones_ref.shape[2] - dv), BF16)

    lane = lax.broadcasted_iota(jnp.int32, (t, LANES), 1)
    zero = jnp.zeros((t, LANES), q_ref.dtype)
    qc = []
    for hd in range(n_heads):
        q = q_ref[:, chunk(hd)]
        qc += [jnp.where(lane < HEAD_DIM, q, zero), jnp.where(lane >= HEAD_DIM, q, zero)]
    _init_softmax_state(m_ref, acc_ref)

    def scores(j, slot):
        rows = pl.ds(pl.multiple_of(j * t, t), t)
        for hd in range(n_heads):
            k = k_ref[rows, chunk(hd)]
            for c in range(2):
                s_ref[slot, 2 * hd + c] = _dot_nt(qc[2 * hd + c], k)

    def update(j, slot, masked):
        rows = pl.ds(pl.multiple_of(j * t, t), t)
        for hd in range(n_heads):
            v = vones_ref[hd, rows, :]
            for c in range(2):
                s = s_ref[slot, 2 * hd + c]
                if masked:
                    s = s + cb_ref[...]
                _online_update(s, v, m_ref, acc_ref, 2 * hd + c)

    scores(0, 0)

    def body(pp, carry):
        scores(2 * pp + 1, 1)
        update(2 * pp, 0, False)
        scores(2 * pp + 2, 0)
        update(2 * pp + 1, 1, False)
        return carry

    lax.fori_loop(0, i // 2, body, 0)

    @pl.when(i % 2 == 0)
    def _():
        update(i, 0, True)

    @pl.when(i % 2 == 1)
    def _():
        scores(i, 1)
        update(i - 1, 0, False)
        update(i, 1, True)

    lam = (jnp.exp(jnp.sum(lq1_ref[...] * lk1_ref[...], axis=-1, keepdims=True))
           - jnp.exp(jnp.sum(lq2_ref[...] * lk2_ref[...], axis=-1, keepdims=True)) + lambda_init)
    for hd in range(n_heads):
        c0, c1 = 2 * hd, 2 * hd + 1
        o = acc_ref[c0, :, :dv] / acc_ref[c0, :, dv:] - lam * (acc_ref[c1, :, :dv] / acc_ref[c1, :, dv:])
        ms = jnp.mean(o * o, axis=-1, keepdims=True)
        o_ref[:, chunk(hd)] = (o * lax.rsqrt(ms + EPS) * sg_ref[...] * (1.0 - lambda_init)).astype(o_ref.dtype)


def _diff_attention(ab, causal_bias, lams, subln_g, batch, seq, lambda_init, col):
    t = causal_bias.shape[0]
    nq = seq // t
    dv = 2 * HEAD_DIM
    hps = DIFF_HEADS_PER_STEP
    width = hps * LANES
    chains = 2 * hps
    assert dv == LANES and DIFF_HEADS % hps == 0
    small = lambda arr: pl.BlockSpec(arr.shape, lambda b, h, i: (0, 0))
    return pl.pallas_call(
        functools.partial(_diff_attn_kernel, lambda_init=lambda_init),
        out_shape=jax.ShapeDtypeStruct((batch * seq, DIFF_HEADS * dv), BF16),
        grid=(batch, DIFF_HEADS // hps, nq),
        in_specs=[
            pl.BlockSpec((t, width), lambda b, h, i: (b * nq + i, col["dq"] // width + h)),
            pl.BlockSpec((seq, width), lambda b, h, i: (b, col["dk"] // width + h)),
            pl.BlockSpec((seq, width), lambda b, h, i: (b, col["dv"] // width + h)),
            small(causal_bias),
            small(lams[0]), small(lams[1]), small(lams[2]), small(lams[3]), small(subln_g),
        ],
        out_specs=pl.BlockSpec((t, width), lambda b, h, i: (b * nq + i, h)),
        scratch_shapes=[
            pltpu.VMEM((chains, t, LANES), F32),
            pltpu.VMEM((chains, t, 2 * dv), F32),
            pltpu.VMEM((hps, seq, 2 * dv), BF16),
            pltpu.VMEM((2, chains, t, t), F32),
        ],
        compiler_params=pltpu.CompilerParams(
            dimension_semantics=("parallel", "parallel", "arbitrary"), vmem_limit_bytes=VMEM_LIMIT),
        name="diff_attention",
    )(ab, ab, ab, causal_bias, *lams, subln_g)


def _nsa_kernel(q_ref, ksl_ref, vsl_ref, kwn_ref, vwn_ref, kc_ref, vc_ref, gate_ref, ovt_ref, cmpb_ref, cb_ref,
                wb_ref, o_ref, qaug_ref, m_ref, acc_ref, accw_ref, imp_ref, kaug_ref, vsl1_ref, vwn1_ref,
                q4_ref, part_ref, rank_ref):
    i = pl.program_id(2)
    t = q_ref.shape[0]
    seq = ksl_ref.shape[0]
    q0 = i * t
    n_sel = seq // SEL_BLOCK
    assert n_sel <= LANES - HEAD_DIM and imp_ref.shape[0] == LANES
    lane = lax.broadcasted_iota(jnp.int32, (t, LANES), 1)
    low = lane < HEAD_DIM

    @pl.when(i == 0)
    def _():
        row = lax.broadcasted_iota(jnp.int32, (seq, LANES), 0)
        ln = lax.broadcasted_iota(jnp.int32, (seq, LANES), 1)
        kaug_ref[...] = ksl_ref[...] + jnp.where(ln - HEAD_DIM == row // SEL_BLOCK, 1.0, 0.0).astype(BF16)
        ones_hi = jnp.where(ln >= HEAD_DIM, 1.0, 0.0).astype(BF16)
        vsl1_ref[...] = vsl_ref[...] + ones_hi
        vwn1_ref[...] = vwn_ref[...] + ones_hi

    q = q_ref[...].astype(F32)
    heads = []
    for pair in range(NSA_HPG // 2):
        qp = q[:, pair * LANES:(pair + 1) * LANES]
        heads += [qp, pltpu.roll(qp, HEAD_DIM, axis=1)]

    for hh, hq in enumerate(heads):
        q4_ref[hh] = jnp.where(low, hq, 0.0).astype(BF16)
    rows = NSA_HPG * t

    tw = wb_ref.shape[1]
    wk = wb_ref.shape[2]
    assert wk == WINDOW + tw and WINDOW % tw == 0 and t % tw == 0
    for sub in range(t // tw):
        qs = q4_ref[:, sub * tw:(sub + 1) * tw, :].reshape(NSA_HPG * tw, LANES)
        sub_idx = i * (t // tw) + sub
        kstart = pl.multiple_of(jnp.maximum(q0 + sub * tw - WINDOW, 0), tw)
        s = _dot_nt(qs, kwn_ref[pl.ds(kstart, wk), :])
        s = _add_per_head(s, wb_ref[jnp.minimum(sub_idx, WINDOW // tw)], NSA_HPG)
        e = jnp.exp2(s - jnp.max(s, axis=1, keepdims=True))
        acc_w =_dot(e.astype(BF16), vwn1_ref[pl.ds(kstart, wk), :])
        accw_ref[:, sub * tw:(sub + 1) * tw, :] = acc_w.reshape(NSA_HPG, tw, LANES)

    s = _add_per_head(_dot_nt(q4_ref[...].reshape(rows, LANES), kc_ref[...]), cmpb_ref[...], NSA_HPG)
    e = jnp.exp2(s - jnp.max(s, axis=1, keepdims=True))
    p = e / jnp.sum(e, axis=1, keepdims=True)
    o_cmp = _dot(p.astype(BF16), vc_ref[...])

    denom = jnp.ones((t, LANES), F32)
    for hh in range(NSA_HPG):
        denom = jnp.where(lane == HEAD_DIM + 3 * hh + 2, accw_ref[hh], denom)
    coef = gate_ref[...] / denom
    has_cmp = q0 + lax.broadcasted_iota(jnp.int32, (t, 1), 0) >= CMP_BLOCK - 1
    for hh in range(NSA_HPG):
        c0 = HEAD_DIM + 3 * hh
        part_ref[pl.ds(hh * t, t), :] = (jnp.where(has_cmp, coef[:, c0:c0 + 1], 0.0) * o_cmp[hh * t:(hh + 1) * t]
                                         + coef[:, c0 + 2:c0 + 3] * accw_ref[hh])

    psum = p[0:t] + p[t:2 * t] + p[2 * t:3 * t] + p[3 * t:4 * t]
    hi, lo = _split_bf16(psum)
    ovt = ovt_ref[...]
    imp = _dot_nt(ovt, hi) + _dot_nt(ovt, lo)
    jrow = lax.broadcasted_iota(jnp.int32, imp.shape, 0) - SEL_BLOCK
    qcol = q0 + lax.broadcasted_iota(jnp.int32, imp.shape, 1)
    cur = qcol // SEL_BLOCK
    forced = (jrow == 0) | (jrow == cur) | (jrow == cur - 1)
    imp = jnp.where(forced, FORCED_SCORE, imp)
    imp = jnp.where((jrow >= 0) & (jrow * SEL_BLOCK <= qcol), imp, -jnp.inf)
    imp_ref[...] = imp
    imp_b = imp[SEL_BLOCK:, :]
    jb = lax.broadcasted_iota(jnp.int32, imp_b.shape, 0)
    per_tile = t // SEL_BLOCK
    rank_ref[...] = jnp.zeros(rank_ref.shape, F32)

    def count_group(g8):
        rank = rank_ref[...]
        for kk in range(g8 * per_tile, (g8 + 1) * per_tile):
            rk = imp_ref[SEL_BLOCK + kk:SEL_BLOCK + kk + 1, :]
            rank = rank + jnp.where(rk > imp_b, 1.0, jnp.where((rk == imp_b) & (jb > kk), 1.0, 0.0))
        rank_ref[...] = rank

    count_group(0)
    for g8 in range(1, n_sel // per_tile):
        pl.when(i >= g8)(functools.partial(count_group, g8))
    chosen = (rank_ref[...] < float(min(SEL_TOP, n_sel))) & (imp_b > -jnp.inf)
    bias_t = jnp.concatenate([jnp.zeros((SEL_BLOCK, t), F32), jnp.where(chosen, 0.0, NEG_BLOCK)], axis=0)
    bias = bias_t.T
    for hh, hq in enumerate(heads):
        qaug_ref[hh] = jnp.where(low, hq, bias).astype(BF16)

    _init_softmax_state(m_ref, acc_ref)

    def slc_step(j, masked):
        start = pl.multiple_of(j * t, t)
        s = _dot_nt(qaug_ref[...].reshape(rows, LANES), kaug_ref[pl.ds(start, t), :])
        if masked:
            s = _add_per_head(s, cb_ref[...], NSA_HPG)
        _online_update(s, vsl1_ref[pl.ds(start, t), :], m_ref, acc_ref, 0)

    def slc_body(j, carry):
        slc_step(j, False)
        return carry

    lax.fori_loop(0, i, slc_body, 0)
    slc_step(i, True)

    denom = jnp.ones((t, LANES), F32)
    for hh in range(NSA_HPG):
        denom = jnp.where(lane == HEAD_DIM + 3 * hh + 1, acc_ref[0, pl.ds(hh * t, t), :], denom)
    coef = gate_ref[...] / denom
    outs = []
    for hh in range(NSA_HPG):
        c1 = HEAD_DIM + 3 * hh + 1
        outs.append(part_ref[pl.ds(hh * t, t), :] + coef[:, c1:c1 + 1] * acc_ref[0, pl.ds(hh * t, t), :])
    for pair in range(NSA_HPG // 2):
        o_ref[:, pair * LANES:(pair + 1) * LANES] = jnp.where(
            low, outs[2 * pair], pltpu.roll(outs[2 * pair + 1], HEAD_DIM, axis=1)).astype(o_ref.dtype)


def _nsa_attention(ab, gates, kcmp, vcmp, ovt, cmp_bias, causal_bias, window_bias, batch, seq, col, col_gate):
    t = causal_bias.shape[0]
    nq = seq // t
    gw = NSA_HPG * HEAD_DIM
    rows = NSA_HPG * t
    n_cmp = kcmp.shape[2]
    full = lambda arr: pl.BlockSpec(arr.shape, lambda b, g, i: (0,) * arr.ndim)
    kv = lambda name: pl.BlockSpec((seq, LANES), lambda b, g, i: (b, col[name] // LANES + g))
    return pl.pallas_call(
        _nsa_kernel,
        out_shape=jax.ShapeDtypeStruct((batch * seq, NSA_HEADS * HEAD_DIM), BF16),
        grid=(batch, NSA_GROUPS, nq),
        in_specs=[
            pl.BlockSpec((t, gw), lambda b, g, i: (b * nq + i, col["nq"] // gw + g)),
            kv("ksl"), kv("vsl"), kv("kwn"), kv("vwn"),
            pl.BlockSpec((None, None, n_cmp, LANES), lambda b, g, i: (b, g, 0, 0)),
            pl.BlockSpec((None, None, n_cmp, LANES), lambda b, g, i: (b, g, 0, 0)),
            pl.BlockSpec((t, LANES), lambda b, g, i: (b * nq + i, col_gate // LANES + g)),
            full(ovt),
            pl.BlockSpec((t, n_cmp), lambda b, g, i: (i, 0)),
            full(causal_bias),
            full(window_bias),
        ],
        out_specs=pl.BlockSpec((t, gw), lambda b, g, i: (b * nq + i, g)),
        scratch_shapes=[
            pltpu.VMEM((NSA_HPG, t, LANES), BF16),
            pltpu.VMEM((1, rows, LANES), F32),
            pltpu.VMEM((1, rows, LANES), F32),
            pltpu.VMEM((NSA_HPG, t, LANES), F32),
            pltpu.VMEM((LANES, t), F32),
            pltpu.VMEM((seq, LANES), BF16),
            pltpu.VMEM((seq, LANES), BF16),
            pltpu.VMEM((seq, LANES), BF16),
            pltpu.VMEM((NSA_HPG, t, LANES), BF16),
            pltpu.VMEM((rows, LANES), F32),
            pltpu.VMEM((SEL_BLOCK, t), F32),
        ],
        compiler_params=pltpu.CompilerParams(
            dimension_semantics=("parallel", "parallel", "arbitrary"), vmem_limit_bytes=VMEM_LIMIT),
        name="nsa_attention",
    )(ab, ab, ab, ab, ab, kcmp, vcmp, gates, ovt, cmp_bias, causal_bias, window_bias)


def _merge_kernel(yd_ref, yn_ref, g0_ref, g1_ref, x_ref, wd_ref, wn_ref, wo_ref, lg_ref, x1_ref, h_ref):
    mixed = g0_ref[...] * _dot(yd_ref[...], wd_ref[...]) + g1_ref[...] * _dot(yn_ref[...], wn_ref[...])
    x1 = x_ref[...] + _dot(mixed.astype(BF16), wo_ref[...])
    x1_ref[...] = x1
    ms = jnp.mean(x1 * x1, axis=-1, keepdims=True)
    h_ref[...] = (x1 * lax.rsqrt(ms + EPS) * lg_ref[...]).astype(h_ref.dtype)


def _merge(yd, yn, gates, x2d, wd, wn, wo, lg):
    n, d = x2d.shape
    tm = min(MERGE_TM, n)
    tok = lambda c: pl.BlockSpec((tm, d), lambda i: (i, c))
    full = lambda arr: pl.BlockSpec(arr.shape, lambda i: (0, 0))
    return pl.pallas_call(
        _merge_kernel,
        out_shape=(jax.ShapeDtypeStruct((n, d), F32), jax.ShapeDtypeStruct((n, d), BF16)),
        grid=(n // tm,),
        in_specs=[tok(0), tok(0), tok(0), tok(1), tok(0), full(wd), full(wn), full(wo), full(lg)],
        out_specs=(tok(0), tok(0)),
        compiler_params=pltpu.CompilerParams(
            dimension_semantics=("parallel",), vmem_limit_bytes=VMEM_LIMIT),
        name="merge_outproj",
    )(yd, yn, gates, gates, x2d, wd, wn, wo, lg)


def _mlp_kernel(h_ref, x1_ref, wu_ref, wd_ref, o_ref, acc_ref):
    f = pl.program_id(1)

    @pl.when(f == 0)
    def _():
        acc_ref[...] = x1_ref[...]

    up = jnp.maximum(_dot(h_ref[...], wu_ref[...]), 0.0)
    acc_ref[...] += _dot((up * up).astype(BF16), wd_ref[...])

    @pl.when(f == pl.num_programs(1) - 1)
    def _():
        o_ref[...] = acc_ref[...]


def _mlp(h, x1, wu, wd):
    n, d = x1.shape
    dff = wu.shape[1]
    tm, tf = min(MLP_TM, n), MLP_TF
    return pl.pallas_call(
        _mlp_kernel,
        out_shape=jax.ShapeDtypeStruct((n, d), F32),
        grid=(n // tm, dff // tf),
        in_specs=[
            pl.BlockSpec((tm, d), lambda i, f: (i, 0)),
            pl.BlockSpec((tm, d), lambda i, f: (i, 0)),
            pl.BlockSpec((d, tf), lambda i, f: (0, f)),
            pl.BlockSpec((tf, d), lambda i, f: (f, 0)),
        ],
        out_specs=pl.BlockSpec((tm, d), lambda i, f: (i, 0)),
        scratch_shapes=[pltpu.VMEM((tm, d), F32)],
        compiler_params=pltpu.CompilerParams(
            dimension_semantics=("parallel", "arbitrary"), vmem_limit_bytes=VMEM_LIMIT),
        name="mlp",
    )(h, x1, wu, wd)


def _rope_tables(pos):
    inv_freq = ROPE_THETA ** (-jnp.arange(HALF, dtype=F32) / HALF)
    ang = pos.astype(F32)[:, None] * inv_freq[None, :]
    c, s = jnp.cos(ang), jnp.sin(ang)
    return jnp.concatenate([c, c, c, c], axis=-1), jnp.concatenate([-s, s, -s, s], axis=-1)


def _pad_groups(w, width, offset=0):
    d = w.shape[0]
    w = w.reshape(d, NSA_GROUPS, width)
    return jnp.pad(w, ((0, 0), (0, 0), (offset, LANES - width - offset))).reshape(d, NSA_GROUPS * LANES)


def _additive_mask(valid):
    return jnp.asarray(np.where(valid, 0.0, NEG_TOKEN), F32)


def _layer(x2d, batch, seq, layer, ln_mix_g, w_in, diff_q_norm_g, diff_k_norm_g, diff_lambda_q1,
           diff_lambda_k1, diff_lambda_q2, diff_lambda_k2, diff_subln_g, nsa_q_norm_g, nsa_k_norm_g,
           cmp_pos_k, cmp_pos_v, cmp_k_w1, cmp_k_w2, cmp_v_w1, cmp_v_w2, w_proj_diff, w_proj_nsa,
           w_out, ln_mlp_g, w_mlp_up, w_mlp_down):
    d = x2d.shape[1]
    diff_qk = DIFF_HEADS * 2 * HEAD_DIM
    diff_v = DIFF_HEADS * 2 * HEAD_DIM
    nsa_q = NSA_HEADS * HEAD_DIM
    nsa_kv = NSA_GROUPS * HEAD_DIM
    splits = np.cumsum([diff_qk, diff_qk, diff_v, nsa_q] + [nsa_kv] * 6 + [NSA_HEADS * 3, 2 * d])[:-1]
    (w_dq, w_dk, w_dv, w_nq, w_kc, w_vc, w_ksl, w_vsl, w_kwn, w_vwn, w_ng, w_mg) = jnp.split(
        w_in, [int(c) for c in splits], axis=1)
    scale = HEAD_DIM ** -0.5 * math.log2(math.e)
    pad_kv = lambda w: _pad_groups(w, HEAD_DIM)

    fam_rope = [("dq", w_dq), ("dk", w_dk), ("nq", w_nq), ("ksl", pad_kv(w_ksl)), ("kwn", pad_kv(w_kwn))]
    fam_plain = [("dv", w_dv), ("vsl", pad_kv(w_vsl)), ("vwn", pad_kv(w_vwn)), ("kc", w_kc), ("vc", w_vc)]
    fam_gate = [("mg", w_mg), ("ng", _pad_groups(w_ng, NSA_HPG * 3, HEAD_DIM))]
    col, blocks, tiles = {}, [], []
    for fam in (fam_rope, fam_plain, fam_gate):
        start = sum(b.shape[1] for b in blocks)
        for name, w in fam:
            col[name] = sum(b.shape[1] for b in blocks)
            blocks.append(w)
        width = sum(b.shape[1] for b in blocks) - start
        pad = -width % PROJ_TN
        if pad:
            blocks.append(jnp.zeros((d, pad), w_in.dtype))
        tiles.append((width + pad) // PROJ_TN)
    w_all = jnp.concatenate(blocks, axis=1).astype(BF16)
    gate_base = (tiles[0] + tiles[1]) * PROJ_TN

    tile_g = lambda g, reps: jnp.tile(g.astype(F32), reps)
    gain = jnp.concatenate([
        tile_g(diff_q_norm_g, 2 * DIFF_HEADS) * scale,
        tile_g(diff_k_norm_g, 2 * DIFF_HEADS),
        tile_g(nsa_q_norm_g, NSA_HEADS) * scale,
        tile_g(nsa_k_norm_g[1], 2 * NSA_GROUPS),
        tile_g(nsa_k_norm_g[2], 2 * NSA_GROUPS),
    ])[None, :]
    assert gain.shape[1] == tiles[0] * PROJ_TN
    cos, sin = _rope_tables(jnp.arange(seq))
    mean_heads = np.kron(np.eye(MXU_WIDTH // HEAD_DIM), np.full((HEAD_DIM, HEAD_DIM), 1.0 / HEAD_DIM))
    ab, gates = _inproj(x2d, ln_mix_g.astype(F32)[None, :], w_all, gain, cos, sin,
                        jnp.asarray(mean_heads, BF16), seq, *tiles)

    n_runs = seq // CMP_STRIDE
    dup = lambda w: jnp.concatenate([w, w], axis=1).astype(BF16)
    reps = LANES // HEAD_DIM
    per_group = lambda pos: jnp.tile(pos.astype(F32), (1, reps))
    rows_per_group = lambda w1: jnp.tile(
        w1.reshape(CMP_BLOCK, HEAD_DIM, w1.shape[1]), (1, reps, 1)).astype(BF16)
    cmp_center = jnp.arange(n_runs) * CMP_STRIDE + (CMP_BLOCK - 1) / 2.0
    cos_c, sin_c = _rope_tables(cmp_center)
    gain_c = tile_g(nsa_k_norm_g[0], 2)[None, :]
    kcmp = _compress(ab, col["kc"], seq, per_group(cmp_pos_k), rows_per_group(cmp_k_w1), dup(cmp_k_w2),
                     (gain_c, cos_c, sin_c), True)
    vcmp = _compress(ab, col["vc"], seq, per_group(cmp_pos_v), rows_per_group(cmp_v_w1), dup(cmp_v_w2),
                     (), False)

    lambda_init = 0.8 - 0.6 * math.exp(-0.3 * layer)
    lams = [v.astype(F32)[None, :] for v in (diff_lambda_q1, diff_lambda_k1, diff_lambda_q2, diff_lambda_k2)]
    t_att = min(ATT_T, seq)
    tri = np.arange(t_att)
    y_diff = _diff_attention(ab, _additive_mask(tri[None, :] <= tri[:, None]), lams,
                             diff_subln_g.astype(F32)[None, :], batch, seq, lambda_init, col)

    n_sel = seq // SEL_BLOCK
    cmp_start = np.arange(n_runs) * CMP_STRIDE
    sel_start = np.arange(n_sel) * SEL_BLOCK
    overlap = ((cmp_start[:, None] < sel_start[None, :] + SEL_BLOCK)
               & (cmp_start[:, None] + CMP_BLOCK - 1 >= sel_start[None, :]))
    ovt = np.zeros((LANES, n_runs))
    ovt[HEAD_DIM:HEAD_DIM + n_sel] = overlap.T
    cmp_bias = _additive_mask(cmp_start[None, :] + CMP_BLOCK - 1 <= np.arange(seq)[:, None])
    t_nsa = min(NSA_T, seq)
    tri = np.arange(t_nsa)
    t_win = min(WIN_T, t_nsa)
    r = np.arange(t_win)[:, None]
    c = np.arange(WINDOW + t_win)[None, :]
    window_bias = jnp.stack([_additive_mask((c <= r + off) & (r + off - c < WINDOW))
                             for off in range(0, WINDOW + 1, t_win)])
    y_nsa = _nsa_attention(ab, gates, kcmp, vcmp, jnp.asarray(ovt, BF16), cmp_bias,
                           _additive_mask(tri[None, :] <= tri[:, None]), window_bias, batch, seq, col,
                           col["ng"] - gate_base)

    assert col["mg"] == gate_base
    x1, h2 = _merge(y_diff, y_nsa, gates, x2d, w_proj_diff.astype(BF16), w_proj_nsa.astype(BF16),
                    w_out.astype(BF16), ln_mlp_g.astype(F32)[None, :])
    return _mlp(h2, x1, w_mlp_up.astype(BF16), w_mlp_down.astype(BF16))


def kernel(x, ln_mix_g, w_in, diff_q_norm_g, diff_k_norm_g, diff_lambda_q1, diff_lambda_k1, diff_lambda_q2, diff_lambda_k2, diff_subln_g, nsa_q_norm_g, nsa_k_norm_g, cmp_pos_k, cmp_pos_v, cmp_k_w1, cmp_k_w2, cmp_v_w1, cmp_v_w2, w_proj_diff, w_proj_nsa, w_out, ln_mlp_g, w_mlp_up, w_mlp_down):
    batch, seq, d = x.shape
    params = (ln_mix_g, w_in, diff_q_norm_g, diff_k_norm_g, diff_lambda_q1, diff_lambda_k1, diff_lambda_q2,
              diff_lambda_k2, diff_subln_g, nsa_q_norm_g, nsa_k_norm_g, cmp_pos_k, cmp_pos_v, cmp_k_w1,
              cmp_k_w2, cmp_v_w1, cmp_v_w2, w_proj_diff, w_proj_nsa, w_out, ln_mlp_g, w_mlp_up, w_mlp_down)
    x2d = x.reshape(batch * seq, d)
    for layer in range(ln_mix_g.shape[0]):
        x2d = _layer(x2d, batch, seq, layer, *[prm[layer] for prm in params])
    return x2d.reshape(batch, seq, d)
```

```python
import functools
import math

import numpy as np
import jax
import jax.numpy as jnp
from jax import lax
from jax.experimental import pallas as pl
from jax.experimental.pallas import tpu as pltpu

F32 = jnp.float32
BF16 = jnp.bfloat16

LANES = 128
MXU_WIDTH = 256
HEAD_DIM = 64
HALF = HEAD_DIM // 2
DIFF_HEADS = 8
NSA_HEADS = 16
NSA_GROUPS = 4
NSA_HPG = NSA_HEADS // NSA_GROUPS
CMP_BLOCK = 32
CMP_STRIDE = 16
SEL_BLOCK = 64
SEL_TOP = 16
WINDOW = 512
FORCED_SCORE = 1e4
ROPE_THETA = 10000.0
EPS = 1e-6
NEG_BLOCK = -1e9
NEG_TOKEN = -1e30

PROJ_TM = 2048
PROJ_TN = 512
PROJ_ROWS = 256
ATT_T = 512
DIFF_HEADS_PER_STEP = 2
NSA_T = 512
WIN_T = 256
SLC_CHAINS = 2
MERGE_TM = 512
MLP_TM = 1024
MLP_TF = 1024
VMEM_LIMIT = 56 * 1024 * 1024

_NT = (((1,), (1,)), ((), ()))


def _dot(a, b):
    return jnp.dot(a, b, preferred_element_type=F32)


def _dot_nt(a, b):
    return lax.dot_general(a, b, _NT, preferred_element_type=F32)


def _split_bf16(x):
    hi = x.astype(BF16)
    lo = (x - hi.astype(F32)).astype(BF16)
    return hi, lo


def _swap_halves_within_heads(y):
    lane = lax.broadcasted_iota(jnp.int32, y.shape, 1)
    first_half = (lane % HEAD_DIM) < HALF
    return jnp.where(first_half, pltpu.roll(y, LANES - HALF, axis=1), pltpu.roll(y, HALF, axis=1))


def _rope(y, cos, sin_signed):
    return y * cos + _swap_halves_within_heads(y) * sin_signed


def _inproj_kernel(x_ref, g_ref, w_ref, gain_ref, cos_ref, sin_ref, bd_ref, ab_ref, gt_ref, h_scr,
                   *, n_rope, n_plain):
    j = pl.program_id(1)

    @pl.when(j == 0)
    def _():
        x = x_ref[...]
        ms = jnp.mean(x * x, axis=-1, keepdims=True)
        h_scr[...] = (x * lax.rsqrt(ms + EPS) * g_ref[...]).astype(BF16)

    wide = bd_ref.shape[0]
    row_chunks = [slice(r, r + PROJ_ROWS) for r in range(0, h_scr.shape[0], PROJ_ROWS)]

    @pl.when(j < n_rope)
    def _():
        for rs in row_chunks:
            acc = _dot(h_scr[rs, :], w_ref[...])
            cos = cos_ref[rs, :]
            sin = sin_ref[rs, :]
            for c in range(acc.shape[1] // wide):
                sl = slice(c * wide, (c + 1) * wide)
                xc = acc[:, sl]
                ms = _dot((xc * xc).astype(BF16), bd_ref[...])
                y = xc * lax.rsqrt(ms + EPS) * gain_ref[:, sl]
                for k in range(wide // LANES):
                    lo = sl.start + k * LANES
                    ab_ref[rs, lo:lo + LANES] = _rope(
                        y[:, k * LANES:(k + 1) * LANES], cos, sin).astype(ab_ref.dtype)

    @pl.when((j >= n_rope) & (j < n_rope + n_plain))
    def _():
        for rs in row_chunks:
            ab_ref[rs, :] = _dot(h_scr[rs, :], w_ref[...]).astype(ab_ref.dtype)

    @pl.when(j >= n_rope + n_plain)
    def _():
        for rs in row_chunks:
            gt_ref[rs, :] = jax.nn.sigmoid(_dot(h_scr[rs, :], w_ref[...]))


def _inproj(x2d, g, w, gain, cos, sin, bd, seq, n_rope, n_plain, n_gate):
    n, d = x2d.shape
    tm, tn = min(PROJ_TM, seq), PROJ_TN
    assert n % tm == 0 and seq % tm == 0 and w.shape[1] == (n_rope + n_plain + n_gate) * tn
    pos_blocks = seq // tm
    n_ab = n_rope + n_plain
    return pl.pallas_call(
        functools.partial(_inproj_kernel, n_rope=n_rope, n_plain=n_plain),
        out_shape=(jax.ShapeDtypeStruct((n, n_ab * tn), BF16), jax.ShapeDtypeStruct((n, n_gate * tn), F32)),
        grid=(n // tm, n_ab + n_gate),
        in_specs=[
            pl.BlockSpec((tm, d), lambda i, j: (i, 0)),
            pl.BlockSpec((1, d), lambda i, j: (0, 0)),
            pl.BlockSpec((d, tn), lambda i, j: (0, j)),
            pl.BlockSpec((1, tn), lambda i, j: (0, jnp.minimum(j, n_rope - 1))),
            pl.BlockSpec((tm, LANES), lambda i, j: (i % pos_blocks, 0)),
            pl.BlockSpec((tm, LANES), lambda i, j: (i % pos_blocks, 0)),
            pl.BlockSpec(bd.shape, lambda i, j: (0, 0)),
        ],
        out_specs=(
            pl.BlockSpec((tm, tn), lambda i, j: (i, jnp.minimum(j, n_ab - 1))),
            pl.BlockSpec((tm, tn), lambda i, j: (i, jnp.maximum(j - n_ab, 0))),
        ),
        scratch_shapes=[pltpu.VMEM((tm, d), BF16)],
        compiler_params=pltpu.CompilerParams(
            dimension_semantics=("parallel", "arbitrary"), vmem_limit_bytes=VMEM_LIMIT),
        name="inproj",
    )(x2d, g, w, gain, cos, sin, bd)


def _compress_kernel(x_ref, pos_ref, w1_ref, w2_ref, *rest, is_key):
    if is_key:
        gain_ref, cos_ref, sin_ref, o_ref, xf_ref = rest
    else:
        o_ref, xf_ref = rest
    n_runs = o_ref.shape[1]
    per_chunk = LANES // HEAD_DIM
    for c in range(xf_ref.shape[0]):
        xf_ref[c] = x_ref[:, c * LANES:(c + 1) * LANES].astype(F32)
    lane_group = lax.broadcasted_iota(jnp.int32, (n_runs, LANES), 1) // HEAD_DIM
    for g in range(NSA_GROUPS):
        halves = []
        for half in range(CMP_BLOCK // CMP_STRIDE):
            acc = None
            for tt in range(CMP_STRIDE):
                l = half * CMP_STRIDE + tt
                xt = xf_ref[g // per_chunk, pl.ds(tt, n_runs, stride=CMP_STRIDE), :] + pos_ref[l:l + 1, :]
                part = _dot(jnp.where(lane_group == g % per_chunk, xt, 0.0).astype(BF16), w1_ref[l])
                acc = part if acc is None else acc + part
            halves.append(acc)
        hidden = halves[0] + pltpu.roll(halves[1], n_runs - 1, axis=0)
        out = _dot(jax.nn.gelu(hidden).astype(BF16), w2_ref[...])
        if is_key:
            ms = jnp.mean(out * out, axis=-1, keepdims=True)
            out = _rope(out * lax.rsqrt(ms + EPS) * gain_ref[...], cos_ref[...], sin_ref[...])
        o_ref[g] = out.astype(o_ref.dtype)


def _compress(ab, col0, seq, pos, w1, w2dup, extra, is_key):
    batch = ab.shape[0] // seq
    width = NSA_GROUPS * HEAD_DIM
    n_runs = seq // CMP_STRIDE
    full = lambda arr: pl.BlockSpec(arr.shape, lambda b: (0,) * arr.ndim)
    return pl.pallas_call(
        functools.partial(_compress_kernel, is_key=is_key),
        out_shape=jax.ShapeDtypeStruct((batch, NSA_GROUPS, n_runs, LANES), BF16),
        grid=(batch,),
        in_specs=[pl.BlockSpec((seq, width), lambda b: (b, col0 // width)), full(pos), full(w1), full(w2dup)]
        + [full(e) for e in extra],
        out_specs=pl.BlockSpec((None, NSA_GROUPS, n_runs, LANES), lambda b: (b, 0, 0, 0)),
        scratch_shapes=[pltpu.VMEM((width // LANES, seq, LANES), F32)],
        compiler_params=pltpu.CompilerParams(dimension_semantics=("parallel",), vmem_limit_bytes=VMEM_LIMIT),
        name="compress_k" if is_key else "compress_v",
    )(ab, pos, w1, w2dup, *extra)


def _online_update(s, v_ones, m_ref, acc_ref, idx):
    m_prev = m_ref[idx]
    m_next = jnp.maximum(m_prev, jnp.max(s, axis=1, keepdims=True))
    alpha = jnp.exp2(m_prev - m_next)
    p = jnp.exp2(s - jnp.concatenate([m_next] * (s.shape[1] // LANES), axis=1))
    m_ref[idx] = m_next
    scale = jnp.concatenate([alpha] * (acc_ref.shape[-1] // LANES), axis=1)
    acc_ref[idx] = acc_ref[idx] * scale + _dot(p.astype(BF16), v_ones)


def _init_softmax_state(m_ref, acc_ref):
    m_ref[...] = jnp.full(m_ref.shape, -jnp.inf, F32)
    acc_ref[...] = jnp.zeros(acc_ref.shape, F32)


def _add_per_head(s, bias, heads):
    rows, cols = bias.shape
    return (s.reshape(heads, rows, cols) + bias[None]).reshape(heads * rows, cols)


def _diff_attn_kernel(q_ref, k_ref, v_ref, cb_ref, lq1_ref, lk1_ref, lq2_ref, lk2_ref, sg_ref, o_ref,
                      m_ref, acc_ref, vones_ref, s_ref, *, lambda_init):
    i = pl.program_id(2)
    t = q_ref.shape[0]
    dv = 2 * HEAD_DIM
    n_heads = q_ref.shape[1] // LANES
    chunk = lambda hd: slice(hd * LANES, (hd + 1) * LANES)

    @pl.when(i == 0)
    def _():
        for hd in range(n_heads):
            vones_ref[hd, :, :dv] = v_ref[:, chunk(hd)]
            vones_ref[hd, :, dv:] = jnp.ones((v_ref.shape[0], vones_ref.shape[2] - dv), BF16)

    lane = lax.broadcasted_iota(jnp.int32, (t, LANES), 1)
    zero = jnp.zeros((t, LANES), q_ref.dtype)
    qc = []
    for hd in range(n_heads):
        q = q_ref[:, chunk(hd)]
        qc += [jnp.where(lane < HEAD_DIM, q, zero), jnp.where(lane >= HEAD_DIM, q, zero)]
    _init_softmax_state(m_ref, acc_ref)

    def scores(j, slot):
        rows = pl.ds(pl.multiple_of(j * t, t), t)
        for hd in range(n_heads):
            k = k_ref[rows, chunk(hd)]
            for c in range(2):
                s_ref[slot, 2 * hd + c] = _dot_nt(qc[2 * hd + c], k)

    def update(j, slot, masked):
        rows = pl.ds(pl.multiple_of(j * t, t), t)
        for hd in range(n_heads):
            v = vones_ref[hd, rows, :]
            for c in range(2):
                s = s_ref[slot, 2 * hd + c]
                if masked:
                    s = s + cb_ref[...]
                _online_update(s, v, m_ref, acc_ref, 2 * hd + c)

    scores(0, 0)

    def body(pp, carry):
        scores(2 * pp + 1, 1)
        update(2 * pp, 0, False)
        scores(2 * pp + 2, 0)
        update(2 * pp + 1, 1, False)
        return carry

    lax.fori_loop(0, i // 2, body, 0)

    @pl.when(i % 2 == 0)
    def _():
        update(i, 0, True)

    @pl.when(i % 2 == 1)
    def _():
        scores(i, 1)
        update(i - 1, 0, False)
        update(i, 1, True)

    lam = (jnp.exp(jnp.sum(lq1_ref[...] * lk1_ref[...], axis=-1, keepdims=True))
           - jnp.exp(jnp.sum(lq2_ref[...] * lk2_ref[...], axis=-1, keepdims=True)) + lambda_init)
    for hd in range(n_heads):
        c0, c1 = 2 * hd, 2 * hd + 1
        o = acc_ref[c0, :, :dv] / acc_ref[c0, :, dv:] - lam * (acc_ref[c1, :, :dv] / acc_ref[c1, :, dv:])
        ms = jnp.mean(o * o, axis=-1, keepdims=True)
        o_ref[:, chunk(hd)] = (o * lax.rsqrt(ms + EPS) * sg_ref[...] * (1.0 - lambda_init)).astype(o_ref.dtype)


def _diff_attention(ab, causal_bias, lams, subln_g, batch, seq, lambda_init, col):
    t = causal_bias.shape[0]
    nq = seq // t
    dv = 2 * HEAD_DIM
    hps = DIFF_HEADS_PER_STEP
    width = hps * LANES
    chains = 2 * hps
    assert dv == LANES and DIFF_HEADS % hps == 0
    small = lambda arr: pl.BlockSpec(arr.shape, lambda b, h, i: (0, 0))
    return pl.pallas_call(
        functools.partial(_diff_attn_kernel, lambda_init=lambda_init),
        out_shape=jax.ShapeDtypeStruct((batch * seq, DIFF_HEADS * dv), BF16),
        grid=(batch, DIFF_HEADS // hps, nq),
        in_specs=[
            pl.BlockSpec((t, width), lambda b, h, i: (b * nq + i, col["dq"] // width + h)),
            pl.BlockSpec((seq, width), lambda b, h, i: (b, col["dk"] // width + h)),
            pl.BlockSpec((seq, width), lambda b, h, i: (b, col["dv"] // width + h)),
            small(causal_bias),
            small(lams[0]), small(lams[1]), small(lams[2]), small(lams[3]), small(subln_g),
        ],
        out_specs=pl.BlockSpec((t, width), lambda b, h, i: (b * nq + i, h)),
        scratch_shapes=[
            pltpu.VMEM((chains, t, LANES), F32),
            pltpu.VMEM((chains, t, 2 * dv), F32),
            pltpu.VMEM((hps, seq, 2 * dv), BF16),
            pltpu.VMEM((2, chains, t, t), F32),
        ],
        compiler_params=pltpu.CompilerParams(
            dimension_semantics=("parallel", "parallel", "arbitrary"), vmem_limit_bytes=VMEM_LIMIT),
        name="diff_attention",
    )(ab, ab, ab, causal_bias, *lams, subln_g)


def _nsa_kernel(q_ref, ksl_ref, vsl_ref, kwn_ref, vwn_ref, kc_ref, vc_ref, gate_ref, ovt_ref, cmpb_ref, cb_ref,
                wb_ref, o_ref, qaug_ref, m_ref, acc_ref, accw_ref, imp_ref, kaug_ref, vsl1_ref, vwn1_ref,
                q4_ref, part_ref, rank_ref):
    i = pl.program_id(2)
    t = q_ref.shape[0]
    seq = ksl_ref.shape[0]
    q0 = i * t
    n_sel = seq // SEL_BLOCK
    assert n_sel <= LANES - HEAD_DIM and imp_ref.shape[0] == LANES
    lane = lax.broadcasted_iota(jnp.int32, (t, LANES), 1)
    low = lane < HEAD_DIM

    @pl.when(i == 0)
    def _():
        row = lax.broadcasted_iota(jnp.int32, (seq, LANES), 0)
        ln = lax.broadcasted_iota(jnp.int32, (seq, LANES), 1)
        kaug_ref[...] = ksl_ref[...] + jnp.where(ln - HEAD_DIM == row // SEL_BLOCK, 1.0, 0.0).astype(BF16)
        ones_hi = jnp.where(ln >= HEAD_DIM, 1.0, 0.0).astype(BF16)
        vsl1_ref[...] = vsl_ref[...] + ones_hi
        vwn1_ref[...] = vwn_ref[...] + ones_hi

    q = q_ref[...].astype(F32)
    heads = []
    for pair in range(NSA_HPG // 2):
        qp = q[:, pair * LANES:(pair + 1) * LANES]
        heads += [qp, pltpu.roll(qp, HEAD_DIM, axis=1)]

    for hh, hq in enumerate(heads):
        q4_ref[hh] = jnp.where(low, hq, 0.0).astype(BF16)
    rows = NSA_HPG * t

    tw = wb_ref.shape[1]
    wk = wb_ref.shape[2]
    assert wk == WINDOW + tw and WINDOW % tw == 0 and t % tw == 0
    for sub in range(t // tw):
        qs = q4_ref[:, sub * tw:(sub + 1) * tw, :].reshape(NSA_HPG * tw, LANES)
        sub_idx = i * (t // tw) + sub
        kstart = pl.multiple_of(jnp.maximum(q0 + sub * tw - WINDOW, 0), tw)
        s = _dot_nt(qs, kwn_ref[pl.ds(kstart, wk), :])
        s = _add_per_head(s, wb_ref[jnp.minimum(sub_idx, WINDOW // tw)], NSA_HPG)
        e = jnp.exp2(s - jnp.max(s, axis=1, keepdims=True))
        acc_w =_dot(e.astype(BF16), vwn1_ref[pl.ds(kstart, wk), :])
        accw_ref[:, sub * tw:(sub + 1) * tw, :] = acc_w.reshape(NSA_HPG, tw, LANES)

    s = _add_per_head(_dot_nt(q4_ref[...].reshape(rows, LANES), kc_ref[...]), cmpb_ref[...], NSA_HPG)
    e = jnp.exp2(s - jnp.max(s, axis=1, keepdims=True))
    p = e / jnp.sum(e, axis=1, keepdims=True)
    o_cmp = _dot(p.astype(BF16), vc_ref[...])

    denom = jnp.ones((t, LANES), F32)
    for hh in range(NSA_HPG):
        denom = jnp.where(lane == HEAD_DIM + 3 * hh + 2, accw_ref[hh], denom)
    coef = gate_ref[...] / denom
    has_cmp = q0 + lax.broadcasted_iota(jnp.int32, (t, 1), 0) >= CMP_BLOCK - 1
    for hh in range(NSA_HPG):
        c0 = HEAD_DIM + 3 * hh
        part_ref[pl.ds(hh * t, t), :] = (jnp.where(has_cmp, coef[:, c0:c0 + 1], 0.0) * o_cmp[hh * t:(hh + 1) * t]
                                         + coef[:, c0 + 2:c0 + 3] * accw_ref[hh])

    psum = p[0:t] + p[t:2 * t] + p[2 * t:3 * t] + p[3 * t:4 * t]
    hi, lo = _split_bf16(psum)
    ovt = ovt_ref[...]
    imp = _dot_nt(ovt, hi) + _dot_nt(ovt, lo)
    jrow = lax.broadcasted_iota(jnp.int32, imp.shape, 0) - SEL_BLOCK
    qcol = q0 + lax.broadcasted_iota(jnp.int32, imp.shape, 1)
    cur = qcol // SEL_BLOCK
    forced = (jrow == 0) | (jrow == cur) | (jrow == cur - 1)
    imp = jnp.where(forced, FORCED_SCORE, imp)
    imp = jnp.where((jrow >= 0) & (jrow * SEL_BLOCK <= qcol), imp, -jnp.inf)
    imp_ref[...] = imp
    imp_b = imp[SEL_BLOCK:, :]
    jb = lax.broadcasted_iota(jnp.int32, imp_b.shape, 0)
    per_tile = t // SEL_BLOCK
    rank_ref[...] = jnp.zeros(rank_ref.shape, F32)

    def count_group(g8):
        rank = rank_ref[...]
        for kk in range(g8 * per_tile, (g8 + 1) * per_tile):
            rk = imp_ref[SEL_BLOCK + kk:SEL_BLOCK + kk + 1, :]
            rank = rank + jnp.where(rk > imp_b, 1.0, jnp.where((rk == imp_b) & (jb > kk), 1.0, 0.0))
        rank_ref[...] = rank

    count_group(0)
    for g8 in range(1, n_sel // per_tile):
        pl.when(i >= g8)(functools.partial(count_group, g8))
    chosen = (rank_ref[...] < float(min(SEL_TOP, n_sel))) & (imp_b > -jnp.inf)
    bias_t = jnp.concatenate([jnp.zeros((SEL_BLOCK, t), F32), jnp.where(chosen, 0.0, NEG_BLOCK)], axis=0)
    bias = bias_t.T
    for hh, hq in enumerate(heads):
        qaug_ref[hh] = jnp.where(low, hq, bias).astype(BF16)

    _init_softmax_state(m_ref, acc_ref)

    def slc_step(j, masked):
        start = pl.multiple_of(j * t, t)
        k = kaug_ref[pl.ds(start, t), :]
        v = vsl1_ref[pl.ds(start, t), :]
        chains = SLC_CHAINS if masked else 1
        per_chain = NSA_HPG // chains
        for ch in range(chains):
            qs = qaug_ref[ch * per_chain:(ch + 1) * per_chain].reshape(per_chain * t, LANES)
            s = _dot_nt(qs, k)
            if masked:
                s = _add_per_head(s, cb_ref[...], per_chain)
            _online_update(s, v, m_ref, acc_ref, (0, pl.ds(ch * per_chain * t, per_chain * t)))

    def slc_body(j, carry):
        slc_step(j, False)
        return carry

    lax.fori_loop(0, i, slc_body, 0)
    slc_step(i, True)

    slc_acc = lambda hh: acc_ref[0, pl.ds(hh * t, t), :]
    denom = jnp.ones((t, LANES), F32)
    for hh in range(NSA_HPG):
        denom = jnp.where(lane == HEAD_DIM + 3 * hh + 1, slc_acc(hh), denom)
    coef = gate_ref[...] / denom
    outs = []
    for hh in range(NSA_HPG):
        c1 = HEAD_DIM + 3 * hh + 1
        outs.append(part_ref[pl.ds(hh * t, t), :] + coef[:, c1:c1 + 1] * slc_acc(hh))
    for pair in range(NSA_HPG // 2):
        o_ref[:, pair * LANES:(pair + 1) * LANES] = jnp.where(
            low, outs[2 * pair], pltpu.roll(outs[2 * pair + 1], HEAD_DIM, axis=1)).astype(o_ref.dtype)


def _nsa_attention(ab, gates, kcmp, vcmp, ovt, cmp_bias, causal_bias, window_bias, batch, seq, col, col_gate):
    t = causal_bias.shape[0]
    nq = seq // t
    gw = NSA_HPG * HEAD_DIM
    rows = NSA_HPG * t
    n_cmp = kcmp.shape[2]
    full = lambda arr: pl.BlockSpec(arr.shape, lambda b, g, i: (0,) * arr.ndim)
    kv = lambda name: pl.BlockSpec((seq, LANES), lambda b, g, i: (b, col[name] // LANES + g))
    return pl.pallas_call(
        _nsa_kernel,
        out_shape=jax.ShapeDtypeStruct((batch * seq, NSA_HEADS * HEAD_DIM), BF16),
        grid=(batch, NSA_GROUPS, nq),
        in_specs=[
            pl.BlockSpec((t, gw), lambda b, g, i: (b * nq + i, col["nq"] // gw + g)),
            kv("ksl"), kv("vsl"), kv("kwn"), kv("vwn"),
            pl.BlockSpec((None, None, n_cmp, LANES), lambda b, g, i: (b, g, 0, 0)),
            pl.BlockSpec((None, None, n_cmp, LANES), lambda b, g, i: (b, g, 0, 0)),
            pl.BlockSpec((t, LANES), lambda b, g, i: (b * nq + i, col_gate // LANES + g)),
            full(ovt),
            pl.BlockSpec((t, n_cmp), lambda b, g, i: (i, 0)),
            full(causal_bias),
            full(window_bias),
        ],
        out_specs=pl.BlockSpec((t, gw), lambda b, g, i: (b * nq + i, g)),
        scratch_shapes=[
            pltpu.VMEM((NSA_HPG, t, LANES), BF16),
            pltpu.VMEM((1, rows, LANES), F32),
            pltpu.VMEM((1, rows, LANES), F32),
            pltpu.VMEM((NSA_HPG, t, LANES), F32),
            pltpu.VMEM((LANES, t), F32),
            pltpu.VMEM((seq, LANES), BF16),
            pltpu.VMEM((seq, LANES), BF16),
            pltpu.VMEM((seq, LANES), BF16),
            pltpu.VMEM((NSA_HPG, t, LANES), BF16),
            pltpu.VMEM((rows, LANES), F32),
            pltpu.VMEM((SEL_BLOCK, t), F32),
        ],
        compiler_params=pltpu.CompilerParams(
            dimension_semantics=("parallel", "parallel", "arbitrary"), vmem_limit_bytes=VMEM_LIMIT),
        name="nsa_attention",
    )(ab, ab, ab, ab, ab, kcmp, vcmp, gates, ovt, cmp_bias, causal_bias, window_bias)


def _merge_kernel(yd_ref, yn_ref, g0_ref, g1_ref, x_ref, wd_ref, wn_ref, wo_ref, lg_ref, x1_ref, h_ref):
    mixed = g0_ref[...] * _dot(yd_ref[...], wd_ref[...]) + g1_ref[...] * _dot(yn_ref[...], wn_ref[...])
    x1 = x_ref[...] + _dot(mixed.astype(BF16), wo_ref[...])
    x1_ref[...] = x1
    ms = jnp.mean(x1 * x1, axis=-1, keepdims=True)
    h_ref[...] = (x1 * lax.rsqrt(ms + EPS) * lg_ref[...]).astype(h_ref.dtype)


def _merge(yd, yn, gates, x2d, wd, wn, wo, lg):
    n, d = x2d.shape
    tm = min(MERGE_TM, n)
    tok = lambda c: pl.BlockSpec((tm, d), lambda i: (i, c))
    full = lambda arr: pl.BlockSpec(arr.shape, lambda i: (0, 0))
    return pl.pallas_call(
        _merge_kernel,
        out_shape=(jax.ShapeDtypeStruct((n, d), F32), jax.ShapeDtypeStruct((n, d), BF16)),
        grid=(n // tm,),
        in_specs=[tok(0), tok(0), tok(0), tok(1), tok(0), full(wd), full(wn), full(wo), full(lg)],
        out_specs=(tok(0), tok(0)),
        compiler_params=pltpu.CompilerParams(
            dimension_semantics=("parallel",), vmem_limit_bytes=VMEM_LIMIT),
        name="merge_outproj",
    )(yd, yn, gates, gates, x2d, wd, wn, wo, lg)


def _mlp_kernel(h_ref, x1_ref, wu_ref, wd_ref, o_ref, acc_ref):
    f = pl.program_id(1)

    @pl.when(f == 0)
    def _():
        acc_ref[...] = x1_ref[...]

    up = jnp.maximum(_dot(h_ref[...], wu_ref[...]), 0.0)
    acc_ref[...] += _dot((up * up).astype(BF16), wd_ref[...])

    @pl.when(f == pl.num_programs(1) - 1)
    def _():
        o_ref[...] = acc_ref[...]


def _mlp(h, x1, wu, wd):
    n, d = x1.shape
    dff = wu.shape[1]
    tm, tf = min(MLP_TM, n), MLP_TF
    return pl.pallas_call(
        _mlp_kernel,
        out_shape=jax.ShapeDtypeStruct((n, d), F32),
        grid=(n // tm, dff // tf),
        in_specs=[
            pl.BlockSpec((tm, d), lambda i, f: (i, 0)),
            pl.BlockSpec((tm, d), lambda i, f: (i, 0)),
            pl.BlockSpec((d, tf), lambda i, f: (0, f)),
            pl.BlockSpec((tf, d), lambda i, f: (f, 0)),
        ],
        out_specs=pl.BlockSpec((tm, d), lambda i, f: (i, 0)),
        scratch_shapes=[pltpu.VMEM((tm, d), F32)],
        compiler_params=pltpu.CompilerParams(
            dimension_semantics=("parallel", "arbitrary"), vmem_limit_bytes=VMEM_LIMIT),
        name="mlp",
    )(h, x1, wu, wd)


def _rope_tables(pos):
    inv_freq = ROPE_THETA ** (-jnp.arange(HALF, dtype=F32) / HALF)
    ang = pos.astype(F32)[:, None] * inv_freq[None, :]
    c, s = jnp.cos(ang), jnp.sin(ang)
    return jnp.concatenate([c, c, c, c], axis=-1), jnp.concatenate([-s, s, -s, s], axis=-1)


def _pad_groups(w, width, offset=0):
    d = w.shape[0]
    w = w.reshape(d, NSA_GROUPS, width)
    return jnp.pad(w, ((0, 0), (0, 0), (offset, LANES - width - offset))).reshape(d, NSA_GROUPS * LANES)


def _additive_mask(valid):
    return jnp.asarray(np.where(valid, 0.0, NEG_TOKEN), F32)


def _layer(x2d, batch, seq, layer, ln_mix_g, w_in, diff_q_norm_g, diff_k_norm_g, diff_lambda_q1,
           diff_lambda_k1, diff_lambda_q2, diff_lambda_k2, diff_subln_g, nsa_q_norm_g, nsa_k_norm_g,
           cmp_pos_k, cmp_pos_v, cmp_k_w1, cmp_k_w2, cmp_v_w1, cmp_v_w2, w_proj_diff, w_proj_nsa,
           w_out, ln_mlp_g, w_mlp_up, w_mlp_down):
    d = x2d.shape[1]
    diff_qk = DIFF_HEADS * 2 * HEAD_DIM
    diff_v = DIFF_HEADS * 2 * HEAD_DIM
    nsa_q = NSA_HEADS * HEAD_DIM
    nsa_kv = NSA_GROUPS * HEAD_DIM
    splits = np.cumsum([diff_qk, diff_qk, diff_v, nsa_q] + [nsa_kv] * 6 + [NSA_HEADS * 3, 2 * d])[:-1]
    (w_dq, w_dk, w_dv, w_nq, w_kc, w_vc, w_ksl, w_vsl, w_kwn, w_vwn, w_ng, w_mg) = jnp.split(
        w_in, [int(c) for c in splits], axis=1)
    scale = HEAD_DIM ** -0.5 * math.log2(math.e)
    pad_kv = lambda w: _pad_groups(w, HEAD_DIM)

    fam_rope = [("dq", w_dq), ("dk", w_dk), ("nq", w_nq), ("ksl", pad_kv(w_ksl)), ("kwn", pad_kv(w_kwn))]
    fam_plain = [("dv", w_dv), ("vsl", pad_kv(w_vsl)), ("vwn", pad_kv(w_vwn)), ("kc", w_kc), ("vc", w_vc)]
    fam_gate = [("mg", w_mg), ("ng", _pad_groups(w_ng, NSA_HPG * 3, HEAD_DIM))]
    col, blocks, tiles = {}, [], []
    for fam in (fam_rope, fam_plain, fam_gate):
        start = sum(b.shape[1] for b in blocks)
        for name, w in fam:
            col[name] = sum(b.shape[1] for b in blocks)
            blocks.append(w)
        width = sum(b.shape[1] for b in blocks) - start
        pad = -width % PROJ_TN
        if pad:
            blocks.append(jnp.zeros((d, pad), w_in.dtype))
        tiles.append((width + pad) // PROJ_TN)
    w_all = jnp.concatenate(blocks, axis=1).astype(BF16)
    gate_base = (tiles[0] + tiles[1]) * PROJ_TN

    tile_g = lambda g, reps: jnp.tile(g.astype(F32), reps)
    gain = jnp.concatenate([
        tile_g(diff_q_norm_g, 2 * DIFF_HEADS) * scale,
        tile_g(diff_k_norm_g, 2 * DIFF_HEADS),
        tile_g(nsa_q_norm_g, NSA_HEADS) * scale,
        tile_g(nsa_k_norm_g[1], 2 * NSA_GROUPS),
        tile_g(nsa_k_norm_g[2], 2 * NSA_GROUPS),
    ])[None, :]
    assert gain.shape[1] == tiles[0] * PROJ_TN
    cos, sin = _rope_tables(jnp.arange(seq))
    mean_heads = np.kron(np.eye(MXU_WIDTH // HEAD_DIM), np.full((HEAD_DIM, HEAD_DIM), 1.0 / HEAD_DIM))
    ab, gates = _inproj(x2d, ln_mix_g.astype(F32)[None, :], w_all, gain, cos, sin,
                        jnp.asarray(mean_heads, BF16), seq, *tiles)

    n_runs = seq // CMP_STRIDE
    dup = lambda w: jnp.concatenate([w, w], axis=1).astype(BF16)
    reps = LANES // HEAD_DIM
    per_group = lambda pos: jnp.tile(pos.astype(F32), (1, reps))
    rows_per_group = lambda w1: jnp.tile(
        w1.reshape(CMP_BLOCK, HEAD_DIM, w1.shape[1]), (1, reps, 1)).astype(BF16)
    cmp_center = jnp.arange(n_runs) * CMP_STRIDE + (CMP_BLOCK - 1) / 2.0
    cos_c, sin_c = _rope_tables(cmp_center)
    gain_c = tile_g(nsa_k_norm_g[0], 2)[None, :]
    kcmp = _compress(ab, col["kc"], seq, per_group(cmp_pos_k), rows_per_group(cmp_k_w1), dup(cmp_k_w2),
                     (gain_c, cos_c, sin_c), True)
    vcmp = _compress(ab, col["vc"], seq, per_group(cmp_pos_v), rows_per_group(cmp_v_w1), dup(cmp_v_w2),
                     (), False)

    lambda_init = 0.8 - 0.6 * math.exp(-0.3 * layer)
    lams = [v.astype(F32)[None, :] for v in (diff_lambda_q1, diff_lambda_k1, diff_lambda_q2, diff_lambda_k2)]
    t_att = min(ATT_T, seq)
    tri = np.arange(t_att)
    y_diff = _diff_attention(ab, _additive_mask(tri[None, :] <= tri[:, None]), lams,
                             diff_subln_g.astype(F32)[None, :], batch, seq, lambda_init, col)

    n_sel = seq // SEL_BLOCK
    cmp_start = np.arange(n_runs) * CMP_STRIDE
    sel_start = np.arange(n_sel) * SEL_BLOCK
    overlap = ((cmp_start[:, None] < sel_start[None, :] + SEL_BLOCK)
               & (cmp_start[:, None] + CMP_BLOCK - 1 >= sel_start[None, :]))
    ovt = np.zeros((LANES, n_runs))
    ovt[HEAD_DIM:HEAD_DIM + n_sel] = overlap.T
    cmp_bias = _additive_mask(cmp_start[None, :] + CMP_BLOCK - 1 <= np.arange(seq)[:, None])
    t_nsa = min(NSA_T, seq)
    tri = np.arange(t_nsa)
    t_win = min(WIN_T, t_nsa)
    r = np.arange(t_win)[:, None]
    c = np.arange(WINDOW + t_win)[None, :]
    window_bias = jnp.stack([_additive_mask((c <= r + off) & (r + off - c < WINDOW))
                             for off in range(0, WINDOW + 1, t_win)])
    y_nsa = _nsa_attention(ab, gates, kcmp, vcmp, jnp.asarray(ovt, BF16), cmp_bias,
                           _additive_mask(tri[None, :] <= tri[:, None]), window_bias, batch, seq, col,
                           col["ng"] - gate_base)

    assert col["mg"] == gate_base
    x1, h2 = _merge(y_diff, y_nsa, gates, x2d, w_proj_diff.astype(BF16), w_proj_nsa.astype(BF16),
                    w_out.astype(BF16), ln_mlp_g.astype(F32)[None, :])
    return _mlp(h2, x1, w_mlp_up.astype(BF16), w_mlp_down.astype(BF16))


def kernel(x, ln_mix_g, w_in, diff_q_norm_g, diff_k_norm_g, diff_lambda_q1, diff_lambda_k1, diff_lambda_q2, diff_lambda_k2, diff_subln_g, nsa_q_norm_g, nsa_k_norm_g, cmp_pos_k, cmp_pos_v, cmp_k_w1, cmp_k_w2, cmp_v_w1, cmp_v_w2, w_proj_diff, w_proj_nsa, w_out, ln_mlp_g, w_mlp_up, w_mlp_down):
    batch, seq, d = x.shape
    params = (ln_mix_g, w_in, diff_q_norm_g, diff_k_norm_g, diff_lambda_q1, diff_lambda_k1, diff_lambda_q2,
              diff_lambda_k2, diff_subln_g, nsa_q_norm_g, nsa_k_norm_g, cmp_pos_k, cmp_pos_v, cmp_k_w1,
              cmp_k_w2, cmp_v_w1, cmp_v_w2, w_proj_diff, w_proj_nsa, w_out, ln_mlp_g, w_mlp_up, w_mlp_down)
    x2d = x.reshape(batch * seq, d)
    for layer in range(ln_mix_g.shape[0]):
        x2d = _layer(x2d, batch, seq, layer, *[prm[layer] for prm in params])
    return x2d.reshape(batch, seq, d)
```

```python
import functools
import math

import numpy as np
import jax
import jax.numpy as jnp
from jax import lax
from jax.experimental import pallas as pl
from jax.experimental.pallas import tpu as pltpu

F32 = jnp.float32
BF16 = jnp.bfloat16

LANES = 128
MXU_WIDTH = 256
HEAD_DIM = 64
HALF = HEAD_DIM // 2
DIFF_HEADS = 8
NSA_HEADS = 16
NSA_GROUPS = 4
NSA_HPG = NSA_HEADS // NSA_GROUPS
CMP_BLOCK = 32
CMP_STRIDE = 16
SEL_BLOCK = 64
SEL_TOP = 16
WINDOW = 512
FORCED_SCORE = 1e4
ROPE_THETA = 10000.0
EPS = 1e-6
NEG_BLOCK = -1e9
NEG_TOKEN = -1e30

PROJ_TM = 2048
PROJ_TN = 512
PROJ_ROWS = 512
ATT_T = 512
DIFF_HEADS_PER_STEP = 2
NSA_T = 512
WIN_T = 256
SLC_CHAINS = 2
MERGE_TM = 512
MLP_TM = 1024
MLP_TF = 1024
VMEM_LIMIT = 56 * 1024 * 1024

_NT = (((1,), (1,)), ((), ()))


def _dot(a, b):
    return jnp.dot(a, b, preferred_element_type=F32)


def _dot_nt(a, b):
    return lax.dot_general(a, b, _NT, preferred_element_type=F32)


def _split_bf16(x):
    hi = x.astype(BF16)
    lo = (x - hi.astype(F32)).astype(BF16)
    return hi, lo


def _swap_halves_within_heads(y):
    lane = lax.broadcasted_iota(jnp.int32, y.shape, 1)
    first_half = (lane % HEAD_DIM) < HALF
    return jnp.where(first_half, pltpu.roll(y, LANES - HALF, axis=1), pltpu.roll(y, HALF, axis=1))


def _rope(y, cos, sin_signed):
    return y * cos + _swap_halves_within_heads(y) * sin_signed


def _inproj_kernel(x_ref, g_ref, w_ref, gain_ref, cos_ref, sin_ref, bd_ref, ab_ref, gt_ref, h_scr,
                   *, n_rope, n_plain):
    j = pl.program_id(1)

    @pl.when(j == 0)
    def _():
        x = x_ref[...]
        ms = jnp.mean(x * x, axis=-1, keepdims=True)
        h_scr[...] = (x * lax.rsqrt(ms + EPS) * g_ref[...]).astype(BF16)

    wide = bd_ref.shape[0]
    row_chunks = [slice(r, r + PROJ_ROWS) for r in range(0, h_scr.shape[0], PROJ_ROWS)]

    @pl.when(j < n_rope)
    def _():
        for rs in row_chunks:
            acc = _dot(h_scr[rs, :], w_ref[...])
            cos = cos_ref[rs, :]
            sin = sin_ref[rs, :]
            for c in range(acc.shape[1] // wide):
                sl = slice(c * wide, (c + 1) * wide)
                xc = acc[:, sl]
                ms = _dot((xc * xc).astype(BF16), bd_ref[...])
                y = xc * lax.rsqrt(ms + EPS) * gain_ref[:, sl]
                for k in range(wide // LANES):
                    lo = sl.start + k * LANES
                    ab_ref[rs, lo:lo + LANES] = _rope(
                        y[:, k * LANES:(k + 1) * LANES], cos, sin).astype(ab_ref.dtype)

    @pl.when((j >= n_rope) & (j < n_rope + n_plain))
    def _():
        for rs in row_chunks:
            ab_ref[rs, :] = _dot(h_scr[rs, :], w_ref[...]).astype(ab_ref.dtype)

    @pl.when(j >= n_rope + n_plain)
    def _():
        for rs in row_chunks:
            gt_ref[rs, :] = jax.nn.sigmoid(_dot(h_scr[rs, :], w_ref[...]))


def _inproj(x2d, g, w, gain, cos, sin, bd, seq, n_rope, n_plain, n_gate):
    n, d = x2d.shape
    tm, tn = min(PROJ_TM, seq), PROJ_TN
    assert n % tm == 0 and seq % tm == 0 and w.shape[1] == (n_rope + n_plain + n_gate) * tn
    pos_blocks = seq // tm
    n_ab = n_rope + n_plain
    return pl.pallas_call(
        functools.partial(_inproj_kernel, n_rope=n_rope, n_plain=n_plain),
        out_shape=(jax.ShapeDtypeStruct((n, n_ab * tn), BF16), jax.ShapeDtypeStruct((n, n_gate * tn), F32)),
        grid=(n // tm, n_ab + n_gate),
        in_specs=[
            pl.BlockSpec((tm, d), lambda i, j: (i, 0)),
            pl.BlockSpec((1, d), lambda i, j: (0, 0)),
            pl.BlockSpec((d, tn), lambda i, j: (0, j)),
            pl.BlockSpec((1, tn), lambda i, j: (0, jnp.minimum(j, n_rope - 1))),
            pl.BlockSpec((tm, LANES), lambda i, j: (i % pos_blocks, 0)),
            pl.BlockSpec((tm, LANES), lambda i, j: (i % pos_blocks, 0)),
            pl.BlockSpec(bd.shape, lambda i, j: (0, 0)),
        ],
        out_specs=(
            pl.BlockSpec((tm, tn), lambda i, j: (i, jnp.minimum(j, n_ab - 1))),
            pl.BlockSpec((tm, tn), lambda i, j: (i, jnp.maximum(j - n_ab, 0))),
        ),
        scratch_shapes=[pltpu.VMEM((tm, d), BF16)],
        compiler_params=pltpu.CompilerParams(
            dimension_semantics=("parallel", "arbitrary"), vmem_limit_bytes=VMEM_LIMIT),
        name="inproj",
    )(x2d, g, w, gain, cos, sin, bd)


def _compress_kernel(x_ref, pos_ref, w1_ref, w2_ref, *rest, is_key):
    if is_key:
        gain_ref, cos_ref, sin_ref, o_ref, xf_ref = rest
    else:
        o_ref, xf_ref = rest
    n_runs = o_ref.shape[1]
    per_chunk = LANES // HEAD_DIM
    for c in range(xf_ref.shape[0]):
        xf_ref[c] = x_ref[:, c * LANES:(c + 1) * LANES].astype(F32)
    lane_group = lax.broadcasted_iota(jnp.int32, (n_runs, LANES), 1) // HEAD_DIM
    for g in range(NSA_GROUPS):
        halves = []
        for half in range(CMP_BLOCK // CMP_STRIDE):
            acc = None
            for tt in range(CMP_STRIDE):
                l = half * CMP_STRIDE + tt
                xt = xf_ref[g // per_chunk, pl.ds(tt, n_runs, stride=CMP_STRIDE), :] + pos_ref[l:l + 1, :]
                part = _dot(jnp.where(lane_group == g % per_chunk, xt, 0.0).astype(BF16), w1_ref[l])
                acc = part if acc is None else acc + part
            halves.append(acc)
        hidden = halves[0] + pltpu.roll(halves[1], n_runs - 1, axis=0)
        out = _dot(jax.nn.gelu(hidden).astype(BF16), w2_ref[...])
        if is_key:
            ms = jnp.mean(out * out, axis=-1, keepdims=True)
            out = _rope(out * lax.rsqrt(ms + EPS) * gain_ref[...], cos_ref[...], sin_ref[...])
        o_ref[g] = out.astype(o_ref.dtype)


def _compress(ab, col0, seq, pos, w1, w2dup, extra, is_key):
    batch = ab.shape[0] // seq
    width = NSA_GROUPS * HEAD_DIM
    n_runs = seq // CMP_STRIDE
    full = lambda arr: pl.BlockSpec(arr.shape, lambda b: (0,) * arr.ndim)
    return pl.pallas_call(
        functools.partial(_compress_kernel, is_key=is_key),
        out_shape=jax.ShapeDtypeStruct((batch, NSA_GROUPS, n_runs, LANES), BF16),
        grid=(batch,),
        in_specs=[pl.BlockSpec((seq, width), lambda b: (b, col0 // width)), full(pos), full(w1), full(w2dup)]
        + [full(e) for e in extra],
        out_specs=pl.BlockSpec((None, NSA_GROUPS, n_runs, LANES), lambda b: (b, 0, 0, 0)),
        scratch_shapes=[pltpu.VMEM((width // LANES, seq, LANES), F32)],
        compiler_params=pltpu.CompilerParams(dimension_semantics=("parallel",), vmem_limit_bytes=VMEM_LIMIT),
        name="compress_k" if is_key else "compress_v",
    )(ab, pos, w1, w2dup, *extra)


def _online_update(s, v_ones, m_ref, acc_ref, idx):
    m_prev = m_ref[idx]
    m_next = jnp.maximum(m_prev, jnp.max(s, axis=1, keepdims=True))
    alpha = jnp.exp2(m_prev - m_next)
    p = jnp.exp2(s - jnp.concatenate([m_next] * (s.shape[1] // LANES), axis=1))
    m_ref[idx] = m_next
    scale = jnp.concatenate([alpha] * (acc_ref.shape[-1] // LANES), axis=1)
    acc_ref[idx] = acc_ref[idx] * scale + _dot(p.astype(BF16), v_ones)


def _init_softmax_state(m_ref, acc_ref):
    m_ref[...] = jnp.full(m_ref.shape, -jnp.inf, F32)
    acc_ref[...] = jnp.zeros(acc_ref.shape, F32)


def _add_per_head(s, bias, heads):
    rows, cols = bias.shape
    return (s.reshape(heads, rows, cols) + bias[None]).reshape(heads * rows, cols)


def _diff_attn_kernel(q_ref, k_ref, v_ref, cb_ref, lq1_ref, lk1_ref, lq2_ref, lk2_ref, sg_ref, o_ref,
                      m_ref, acc_ref, vones_ref, s_ref, *, lambda_init):
    i = pl.program_id(2)
    t = q_ref.shape[0]
    dv = 2 * HEAD_DIM
    n_heads = q_ref.shape[1] // LANES
    chunk = lambda hd: slice(hd * LANES, (hd + 1) * LANES)

    @pl.when(i == 0)
    def _():
        for hd in range(n_heads):
            vones_ref[hd, :, :dv] = v_ref[:, chunk(hd)]
            vones_ref[hd, :, dv:] = jnp.ones((v_ref.shape[0], vones_ref.shape[2] - dv), BF16)

    lane = lax.broadcasted_iota(jnp.int32, (t, LANES), 1)
    zero = jnp.zeros((t, LANES), q_ref.dtype)
    qc = []
    for hd in range(n_heads):
        q = q_ref[:, chunk(hd)]
        qc += [jnp.where(lane < HEAD_DIM, q, zero), jnp.where(lane >= HEAD_DIM, q, zero)]
    _init_softmax_state(m_ref, acc_ref)

    def scores(j, slot):
        rows = pl.ds(pl.multiple_of(j * t, t), t)
        for hd in range(n_heads):
            k = k_ref[rows, chunk(hd)]
            for c in range(2):
                s_ref[slot, 2 * hd + c] = _dot_nt(qc[2 * hd + c], k)

    def update(j, slot, masked):
        rows = pl.ds(pl.multiple_of(j * t, t), t)
        for hd in range(n_heads):
            v = vones_ref[hd, rows, :]
            for c in range(2):
                s = s_ref[slot, 2 * hd + c]
                if masked:
                    s = s + cb_ref[...]
                _online_update(s, v, m_ref, acc_ref, 2 * hd + c)

    scores(0, 0)

    def body(pp, carry):
        scores(2 * pp + 1, 1)
        update(2 * pp, 0, False)
        scores(2 * pp + 2, 0)
        update(2 * pp + 1, 1, False)
        return carry

    lax.fori_loop(0, i // 2, body, 0)

    @pl.when(i % 2 == 0)
    def _():
        update(i, 0, True)

    @pl.when(i % 2 == 1)
    def _():
        scores(i, 1)
        update(i - 1, 0, False)
        update(i, 1, True)

    lam = (jnp.exp(jnp.sum(lq1_ref[...] * lk1_ref[...], axis=-1, keepdims=True))
           - jnp.exp(jnp.sum(lq2_ref[...] * lk2_ref[...], axis=-1, keepdims=True)) + lambda_init)
    for hd in range(n_heads):
        c0, c1 = 2 * hd, 2 * hd + 1
        o = acc_ref[c0, :, :dv] / acc_ref[c0, :, dv:] - lam * (acc_ref[c1, :, :dv] / acc_ref[c1, :, dv:])
        ms = jnp.mean(o * o, axis=-1, keepdims=True)
        o_ref[:, chunk(hd)] = (o * lax.rsqrt(ms + EPS) * sg_ref[...] * (1.0 - lambda_init)).astype(o_ref.dtype)


def _diff_attention(ab, causal_bias, lams, subln_g, batch, seq, lambda_init, col):
    t = causal_bias.shape[0]
    nq = seq // t
    dv = 2 * HEAD_DIM
    hps = DIFF_HEADS_PER_STEP
    width = hps * LANES
    chains = 2 * hps
    assert dv == LANES and DIFF_HEADS % hps == 0
    small = lambda arr: pl.BlockSpec(arr.shape, lambda b, h, i: (0, 0))
    return pl.pallas_call(
        functools.partial(_diff_attn_kernel, lambda_init=lambda_init),
        out_shape=jax.ShapeDtypeStruct((batch * seq, DIFF_HEADS * dv), BF16),
        grid=(batch, DIFF_HEADS // hps, nq),
        in_specs=[
            pl.BlockSpec((t, width), lambda b, h, i: (b * nq + i, col["dq"] // width + h)),
            pl.BlockSpec((seq, width), lambda b, h, i: (b, col["dk"] // width + h)),
            pl.BlockSpec((seq, width), lambda b, h, i: (b, col["dv"] // width + h)),
            small(causal_bias),
            small(lams[0]), small(lams[1]), small(lams[2]), small(lams[3]), small(subln_g),
        ],
        out_specs=pl.BlockSpec((t, width), lambda b, h, i: (b * nq + i, h)),
        scratch_shapes=[
            pltpu.VMEM((chains, t, LANES), F32),
            pltpu.VMEM((chains, t, 2 * dv), F32),
            pltpu.VMEM((hps, seq, 2 * dv), BF16),
            pltpu.VMEM((2, chains, t, t), F32),
        ],
        compiler_params=pltpu.CompilerParams(
            dimension_semantics=("parallel", "parallel", "arbitrary"), vmem_limit_bytes=VMEM_LIMIT),
        name="diff_attention",
    )(ab, ab, ab, causal_bias, *lams, subln_g)


def _nsa_kernel(q_ref, ksl_ref, vsl_ref, kwn_ref, vwn_ref, kc_ref, vc_ref, gate_ref, ovt_ref, cmpb_ref, cb_ref,
                wb_ref, o_ref, qaug_ref, m_ref, acc_ref, accw_ref, imp_ref, kaug_ref, vsl1_ref, vwn1_ref,
                q4_ref, part_ref, rank_ref):
    i = pl.program_id(2)
    t = q_ref.shape[0]
    seq = ksl_ref.shape[0]
    q0 = i * t
    n_sel = seq // SEL_BLOCK
    assert n_sel <= LANES - HEAD_DIM and imp_ref.shape[0] == LANES
    lane = lax.broadcasted_iota(jnp.int32, (t, LANES), 1)
    low = lane < HEAD_DIM

    @pl.when(i == 0)
    def _():
        row = lax.broadcasted_iota(jnp.int32, (seq, LANES), 0)
        ln = lax.broadcasted_iota(jnp.int32, (seq, LANES), 1)
        kaug_ref[...] = ksl_ref[...] + jnp.where(ln - HEAD_DIM == row // SEL_BLOCK, 1.0, 0.0).astype(BF16)
        ones_hi = jnp.where(ln >= HEAD_DIM, 1.0, 0.0).astype(BF16)
        vsl1_ref[...] = vsl_ref[...] + ones_hi
        vwn1_ref[...] = vwn_ref[...] + ones_hi

    q = q_ref[...].astype(F32)
    heads = []
    for pair in range(NSA_HPG // 2):
        qp = q[:, pair * LANES:(pair + 1) * LANES]
        heads += [qp, pltpu.roll(qp, HEAD_DIM, axis=1)]

    for hh, hq in enumerate(heads):
        q4_ref[hh] = jnp.where(low, hq, 0.0).astype(BF16)
    rows = NSA_HPG * t

    tw = wb_ref.shape[1]
    wk = wb_ref.shape[2]
    assert wk == WINDOW + tw and WINDOW % tw == 0 and t % tw == 0
    for sub in range(t // tw):
        qs = q4_ref[:, sub * tw:(sub + 1) * tw, :].reshape(NSA_HPG * tw, LANES)
        sub_idx = i * (t // tw) + sub
        kstart = pl.multiple_of(jnp.maximum(q0 + sub * tw - WINDOW, 0), tw)
        s = _dot_nt(qs, kwn_ref[pl.ds(kstart, wk), :])
        s = _add_per_head(s, wb_ref[jnp.minimum(sub_idx, WINDOW // tw)], NSA_HPG)
        e = jnp.exp2(s - jnp.max(s, axis=1, keepdims=True))
        acc_w =_dot(e.astype(BF16), vwn1_ref[pl.ds(kstart, wk), :])
        accw_ref[:, sub * tw:(sub + 1) * tw, :] = acc_w.reshape(NSA_HPG, tw, LANES)

    s = _add_per_head(_dot_nt(q4_ref[...].reshape(rows, LANES), kc_ref[...]), cmpb_ref[...], NSA_HPG)
    e = jnp.exp2(s - jnp.max(s, axis=1, keepdims=True))
    p = e / jnp.sum(e, axis=1, keepdims=True)
    o_cmp = _dot(p.astype(BF16), vc_ref[...])

    denom = jnp.ones((t, LANES), F32)
    for hh in range(NSA_HPG):
        denom = jnp.where(lane == HEAD_DIM + 3 * hh + 2, accw_ref[hh], denom)
    coef = gate_ref[...] / denom
    has_cmp = q0 + lax.broadcasted_iota(jnp.int32, (t, 1), 0) >= CMP_BLOCK - 1
    for hh in range(NSA_HPG):
        c0 = HEAD_DIM + 3 * hh
        part_ref[pl.ds(hh * t, t), :] = (jnp.where(has_cmp, coef[:, c0:c0 + 1], 0.0) * o_cmp[hh * t:(hh + 1) * t]
                                         + coef[:, c0 + 2:c0 + 3] * accw_ref[hh])

    psum = p[0:t] + p[t:2 * t] + p[2 * t:3 * t] + p[3 * t:4 * t]
    hi, lo = _split_bf16(psum)
    ovt = ovt_ref[...]
    imp = _dot_nt(ovt, hi) + _dot_nt(ovt, lo)
    jrow = lax.broadcasted_iota(jnp.int32, imp.shape, 0) - SEL_BLOCK
    qcol = q0 + lax.broadcasted_iota(jnp.int32, imp.shape, 1)
    cur = qcol // SEL_BLOCK
    forced = (jrow == 0) | (jrow == cur) | (jrow == cur - 1)
    imp = jnp.where(forced, FORCED_SCORE, imp)
    imp = jnp.where((jrow >= 0) & (jrow * SEL_BLOCK <= qcol), imp, -jnp.inf)
    imp_ref[...] = imp
    imp_b = imp[SEL_BLOCK:, :]
    jb = lax.broadcasted_iota(jnp.int32, imp_b.shape, 0)
    per_tile = t // SEL_BLOCK
    rank_ref[...] = jnp.zeros(rank_ref.shape, F32)

    def count_group(g8):
        rank = rank_ref[...]
        for kk in range(g8 * per_tile, (g8 + 1) * per_tile):
            rk = imp_ref[SEL_BLOCK + kk:SEL_BLOCK + kk + 1, :]
            rank = rank + jnp.where(rk > imp_b, 1.0, jnp.where((rk == imp_b) & (jb > kk), 1.0, 0.0))
        rank_ref[...] = rank

    count_group(0)
    for g8 in range(1, n_sel // per_tile):
        pl.when(i >= g8)(functools.partial(count_group, g8))
    chosen = (rank_ref[...] < float(min(SEL_TOP, n_sel))) & (imp_b > -jnp.inf)
    bias_t = jnp.concatenate([jnp.zeros((SEL_BLOCK, t), F32), jnp.where(chosen, 0.0, NEG_BLOCK)], axis=0)
    bias = bias_t.T
    for hh, hq in enumerate(heads):
        qaug_ref[hh] = jnp.where(low, hq, bias).astype(BF16)

    _init_softmax_state(m_ref, acc_ref)

    def slc_step(j, masked):
        start = pl.multiple_of(j * t, t)
        k = kaug_ref[pl.ds(start, t), :]
        v = vsl1_ref[pl.ds(start, t), :]
        chains = SLC_CHAINS if masked else 1
        per_chain = NSA_HPG // chains
        for ch in range(chains):
            qs = qaug_ref[ch * per_chain:(ch + 1) * per_chain].reshape(per_chain * t, LANES)
            s = _dot_nt(qs, k)
            if masked:
                s = _add_per_head(s, cb_ref[...], per_chain)
            _online_update(s, v, m_ref, acc_ref, (0, pl.ds(ch * per_chain * t, per_chain * t)))

    def slc_body(j, carry):
        slc_step(j, False)
        return carry

    lax.fori_loop(0, i, slc_body, 0)
    slc_step(i, True)

    slc_acc = lambda hh: acc_ref[0, pl.ds(hh * t, t), :]
    denom = jnp.ones((t, LANES), F32)
    for hh in range(NSA_HPG):
        denom = jnp.where(lane == HEAD_DIM + 3 * hh + 1, slc_acc(hh), denom)
    coef = gate_ref[...] / denom
    outs = []
    for hh in range(NSA_HPG):
        c1 = HEAD_DIM + 3 * hh + 1
        outs.append(part_ref[pl.ds(hh * t, t), :] + coef[:, c1:c1 + 1] * slc_acc(hh))
    for pair in range(NSA_HPG // 2):
        o_ref[:, pair * LANES:(pair + 1) * LANES] = jnp.where(
            low, outs[2 * pair], pltpu.roll(outs[2 * pair + 1], HEAD_DIM, axis=1)).astype(o_ref.dtype)


def _nsa_attention(ab, gates, kcmp, vcmp, ovt, cmp_bias, causal_bias, window_bias, batch, seq, col, col_gate):
    t = causal_bias.shape[0]
    nq = seq // t
    gw = NSA_HPG * HEAD_DIM
    rows = NSA_HPG * t
    n_cmp = kcmp.shape[2]
    full = lambda arr: pl.BlockSpec(arr.shape, lambda b, g, i: (0,) * arr.ndim)
    kv = lambda name: pl.BlockSpec((seq, LANES), lambda b, g, i: (b, col[name] // LANES + g))
    return pl.pallas_call(
        _nsa_kernel,
        out_shape=jax.ShapeDtypeStruct((batch * seq, NSA_HEADS * HEAD_DIM), BF16),
        grid=(batch, NSA_GROUPS, nq),
        in_specs=[
            pl.BlockSpec((t, gw), lambda b, g, i: (b * nq + i, col["nq"] // gw + g)),
            kv("ksl"), kv("vsl"), kv("kwn"), kv("vwn"),
            pl.BlockSpec((None, None, n_cmp, LANES), lambda b, g, i: (b, g, 0, 0)),
            pl.BlockSpec((None, None, n_cmp, LANES), lambda b, g, i: (b, g, 0, 0)),
            pl.BlockSpec((t, LANES), lambda b, g, i: (b * nq + i, col_gate // LANES + g)),
            full(ovt),
            pl.BlockSpec((t, n_cmp), lambda b, g, i: (i, 0)),
            full(causal_bias),
            full(window_bias),
        ],
        out_specs=pl.BlockSpec((t, gw), lambda b, g, i: (b * nq + i, g)),
        scratch_shapes=[
            pltpu.VMEM((NSA_HPG, t, LANES), BF16),
            pltpu.VMEM((1, rows, LANES), F32),
            pltpu.VMEM((1, rows, LANES), F32),
            pltpu.VMEM((NSA_HPG, t, LANES), F32),
            pltpu.VMEM((LANES, t), F32),
            pltpu.VMEM((seq, LANES), BF16),
            pltpu.VMEM((seq, LANES), BF16),
            pltpu.VMEM((seq, LANES), BF16),
            pltpu.VMEM((NSA_HPG, t, LANES), BF16),
            pltpu.VMEM((rows, LANES), F32),
            pltpu.VMEM((SEL_BLOCK, t), F32),
        ],
        compiler_params=pltpu.CompilerParams(
            dimension_semantics=("parallel", "parallel", "arbitrary"), vmem_limit_bytes=VMEM_LIMIT),
        name="nsa_attention",
    )(ab, ab, ab, ab, ab, kcmp, vcmp, gates, ovt, cmp_bias, causal_bias, window_bias)


def _merge_kernel(yd_ref, yn_ref, g0_ref, g1_ref, x_ref, wd_ref, wn_ref, wo_ref, lg_ref, x1_ref, h_ref):
    mixed = g0_ref[...] * _dot(yd_ref[...], wd_ref[...]) + g1_ref[...] * _dot(yn_ref[...], wn_ref[...])
    x1 = x_ref[...] + _dot(mixed.astype(BF16), wo_ref[...])
    x1_ref[...] = x1
    ms = jnp.mean(x1 * x1, axis=-1, keepdims=True)
    h_ref[...] = (x1 * lax.rsqrt(ms + EPS) * lg_ref[...]).astype(h_ref.dtype)


def _merge(yd, yn, gates, x2d, wd, wn, wo, lg):
    n, d = x2d.shape
    tm = min(MERGE_TM, n)
    tok = lambda c: pl.BlockSpec((tm, d), lambda i: (i, c))
    full = lambda arr: pl.BlockSpec(arr.shape, lambda i: (0, 0))
    return pl.pallas_call(
        _merge_kernel,
        out_shape=(jax.ShapeDtypeStruct((n, d), F32), jax.ShapeDtypeStruct((n, d), BF16)),
        grid=(n // tm,),
        in_specs=[tok(0), tok(0), tok(0), tok(1), tok(0), full(wd), full(wn), full(wo), full(lg)],
        out_specs=(tok(0), tok(0)),
        compiler_params=pltpu.CompilerParams(
            dimension_semantics=("parallel",), vmem_limit_bytes=VMEM_LIMIT),
        name="merge_outproj",
    )(yd, yn, gates, gates, x2d, wd, wn, wo, lg)


def _mlp_kernel(h_ref, x1_ref, wu_ref, wd_ref, o_ref, acc_ref):
    f = pl.program_id(1)

    @pl.when(f == 0)
    def _():
        acc_ref[...] = x1_ref[...]

    up = jnp.maximum(_dot(h_ref[...], wu_ref[...]), 0.0)
    acc_ref[...] += _dot((up * up).astype(BF16), wd_ref[...])

    @pl.when(f == pl.num_programs(1) - 1)
    def _():
        o_ref[...] = acc_ref[...]


def _mlp(h, x1, wu, wd):
    n, d = x1.shape
    dff = wu.shape[1]
    tm, tf = min(MLP_TM, n), MLP_TF
    return pl.pallas_call(
        _mlp_kernel,
        out_shape=jax.ShapeDtypeStruct((n, d), F32),
        grid=(n // tm, dff // tf),
        in_specs=[
            pl.BlockSpec((tm, d), lambda i, f: (i, 0)),
            pl.BlockSpec((tm, d), lambda i, f: (i, 0)),
            pl.BlockSpec((d, tf), lambda i, f: (0, f)),
            pl.BlockSpec((tf, d), lambda i, f: (f, 0)),
        ],
        out_specs=pl.BlockSpec((tm, d), lambda i, f: (i, 0)),
        scratch_shapes=[pltpu.VMEM((tm, d), F32)],
        compiler_params=pltpu.CompilerParams(
            dimension_semantics=("parallel", "arbitrary"), vmem_limit_bytes=VMEM_LIMIT),
        name="mlp",
    )(h, x1, wu, wd)


def _rope_tables(pos):
    inv_freq = ROPE_THETA ** (-jnp.arange(HALF, dtype=F32) / HALF)
    ang = pos.astype(F32)[:, None] * inv_freq[None, :]
    c, s = jnp.cos(ang), jnp.sin(ang)
    return jnp.concatenate([c, c, c, c], axis=-1), jnp.concatenate([-s, s, -s, s], axis=-1)


def _pad_groups(w, width, offset=0):
    d = w.shape[0]
    w = w.reshape(d, NSA_GROUPS, width)
    return jnp.pad(w, ((0, 0), (0, 0), (offset, LANES - width - offset))).reshape(d, NSA_GROUPS * LANES)


def _additive_mask(valid):
    return jnp.asarray(np.where(valid, 0.0, NEG_TOKEN), F32)


def _layer(x2d, batch, seq, layer, ln_mix_g, w_in, diff_q_norm_g, diff_k_norm_g, diff_lambda_q1,
           diff_lambda_k1, diff_lambda_q2, diff_lambda_k2, diff_subln_g, nsa_q_norm_g, nsa_k_norm_g,
           cmp_pos_k, cmp_pos_v, cmp_k_w1, cmp_k_w2, cmp_v_w1, cmp_v_w2, w_proj_diff, w_proj_nsa,
           w_out, ln_mlp_g, w_mlp_up, w_mlp_down):
    d = x2d.shape[1]
    diff_qk = DIFF_HEADS * 2 * HEAD_DIM
    diff_v = DIFF_HEADS * 2 * HEAD_DIM
    nsa_q = NSA_HEADS * HEAD_DIM
    nsa_kv = NSA_GROUPS * HEAD_DIM
    splits = np.cumsum([diff_qk, diff_qk, diff_v, nsa_q] + [nsa_kv] * 6 + [NSA_HEADS * 3, 2 * d])[:-1]
    (w_dq, w_dk, w_dv, w_nq, w_kc, w_vc, w_ksl, w_vsl, w_kwn, w_vwn, w_ng, w_mg) = jnp.split(
        w_in, [int(c) for c in splits], axis=1)
    scale = HEAD_DIM ** -0.5 * math.log2(math.e)
    pad_kv = lambda w: _pad_groups(w, HEAD_DIM)

    fam_rope = [("dq", w_dq), ("dk", w_dk), ("nq", w_nq), ("ksl", pad_kv(w_ksl)), ("kwn", pad_kv(w_kwn))]
    fam_plain = [("dv", w_dv), ("vsl", pad_kv(w_vsl)), ("vwn", pad_kv(w_vwn)), ("kc", w_kc), ("vc", w_vc)]
    fam_gate = [("mg", w_mg), ("ng", _pad_groups(w_ng, NSA_HPG * 3, HEAD_DIM))]
    col, blocks, tiles = {}, [], []
    for fam in (fam_rope, fam_plain, fam_gate):
        start = sum(b.shape[1] for b in blocks)
        for name, w in fam:
            col[name] = sum(b.shape[1] for b in blocks)
            blocks.append(w)
        width = sum(b.shape[1] for b in blocks) - start
        pad = -width % PROJ_TN
        if pad:
            blocks.append(jnp.zeros((d, pad), w_in.dtype))
        tiles.append((width + pad) // PROJ_TN)
    w_all = jnp.concatenate(blocks, axis=1).astype(BF16)
    gate_base = (tiles[0] + tiles[1]) * PROJ_TN

    tile_g = lambda g, reps: jnp.tile(g.astype(F32), reps)
    gain = jnp.concatenate([
        tile_g(diff_q_norm_g, 2 * DIFF_HEADS) * scale,
        tile_g(diff_k_norm_g, 2 * DIFF_HEADS),
        tile_g(nsa_q_norm_g, NSA_HEADS) * scale,
        tile_g(nsa_k_norm_g[1], 2 * NSA_GROUPS),
        tile_g(nsa_k_norm_g[2], 2 * NSA_GROUPS),
    ])[None, :]
    assert gain.shape[1] == tiles[0] * PROJ_TN
    cos, sin = _rope_tables(jnp.arange(seq))
    mean_heads = np.kron(np.eye(MXU_WIDTH // HEAD_DIM), np.full((HEAD_DIM, HEAD_DIM), 1.0 / HEAD_DIM))
    ab, gates = _inproj(x2d, ln_mix_g.astype(F32)[None, :], w_all, gain, cos, sin,
                        jnp.asarray(mean_heads, BF16), seq, *tiles)

    n_runs = seq // CMP_STRIDE
    dup = lambda w: jnp.concatenate([w, w], axis=1).astype(BF16)
    reps = LANES // HEAD_DIM
    per_group = lambda pos: jnp.tile(pos.astype(F32), (1, reps))
    rows_per_group = lambda w1: jnp.tile(
        w1.reshape(CMP_BLOCK, HEAD_DIM, w1.shape[1]), (1, reps, 1)).astype(BF16)
    cmp_center = jnp.arange(n_runs) * CMP_STRIDE + (CMP_BLOCK - 1) / 2.0
    cos_c, sin_c = _rope_tables(cmp_center)
    gain_c = tile_g(nsa_k_norm_g[0], 2)[None, :]
    kcmp = _compress(ab, col["kc"], seq, per_group(cmp_pos_k), rows_per_group(cmp_k_w1), dup(cmp_k_w2),
                     (gain_c, cos_c, sin_c), True)
    vcmp = _compress(ab, col["vc"], seq, per_group(cmp_pos_v), rows_per_group(cmp_v_w1), dup(cmp_v_w2),
                     (), False)

    lambda_init = 0.8 - 0.6 * math.exp(-0.3 * layer)
    lams = [v.astype(F32)[None, :] for v in (diff_lambda_q1, diff_lambda_k1, diff_lambda_q2, diff_lambda_k2)]
    t_att = min(ATT_T, seq)
    tri = np.arange(t_att)
    y_diff = _diff_attention(ab, _additive_mask(tri[None, :] <= tri[:, None]), lams,
                             diff_subln_g.astype(F32)[None, :], batch, seq, lambda_init, col)

    n_sel = seq // SEL_BLOCK
    cmp_start = np.arange(n_runs) * CMP_STRIDE
    sel_start = np.arange(n_sel) * SEL_BLOCK
    overlap = ((cmp_start[:, None] < sel_start[None, :] + SEL_BLOCK)
               & (cmp_start[:, None] + CMP_BLOCK - 1 >= sel_start[None, :]))
    ovt = np.zeros((LANES, n_runs))
    ovt[HEAD_DIM:HEAD_DIM + n_sel] = overlap.T
    cmp_bias = _additive_mask(cmp_start[None, :] + CMP_BLOCK - 1 <= np.arange(seq)[:, None])
    t_nsa = min(NSA_T, seq)
    tri = np.arange(t_nsa)
    t_win = min(WIN_T, t_nsa)
    r = np.arange(t_win)[:, None]
    c = np.arange(WINDOW + t_win)[None, :]
    window_bias = jnp.stack([_additive_mask((c <= r + off) & (r + off - c < WINDOW))
                             for off in range(0, WINDOW + 1, t_win)])
    y_nsa = _nsa_attention(ab, gates, kcmp, vcmp, jnp.asarray(ovt, BF16), cmp_bias,
                           _additive_mask(tri[None, :] <= tri[:, None]), window_bias, batch, seq, col,
                           col["ng"] - gate_base)

    assert col["mg"] == gate_base
    x1, h2 = _merge(y_diff, y_nsa, gates, x2d, w_proj_diff.astype(BF16), w_proj_nsa.astype(BF16),
                    w_out.astype(BF16), ln_mlp_g.astype(F32)[None, :])
    return _mlp(h2, x1, w_mlp_up.astype(BF16), w_mlp_down.astype(BF16))


def kernel(x, ln_mix_g, w_in, diff_q_norm_g, diff_k_norm_g, diff_lambda_q1, diff_lambda_k1, diff_lambda_q2, diff_lambda_k2, diff_subln_g, nsa_q_norm_g, nsa_k_norm_g, cmp_pos_k, cmp_pos_v, cmp_k_w1, cmp_k_w2, cmp_v_w1, cmp_v_w2, w_proj_diff, w_proj_nsa, w_out, ln_mlp_g, w_mlp_up, w_mlp_down):
    batch, seq, d = x.shape
    params = (ln_mix_g, w_in, diff_q_norm_g, diff_k_norm_g, diff_lambda_q1, diff_lambda_k1, diff_lambda_q2,
              diff_lambda_k2, diff_subln_g, nsa_q_norm_g, nsa_k_norm_g, cmp_pos_k, cmp_pos_v, cmp_k_w1,
              cmp_k_w2, cmp_v_w1, cmp_v_w2, w_proj_diff, w_proj_nsa, w_out, ln_mlp_g, w_mlp_up, w_mlp_down)
    x2d = x.reshape(batch * seq, d)
    for layer in range(ln_mix_g.shape[0]):
        x2d = _layer(x2d, batch, seq, layer, *[prm[layer] for prm in params])
    return x2d.reshape(batch, seq, d)
```

```python
import functools
import math

import numpy as np
import jax
import jax.numpy as jnp
from jax import lax
from jax.experimental import pallas as pl
from jax.experimental.pallas import tpu as pltpu

F32 = jnp.float32
BF16 = jnp.bfloat16

LANES = 128
MXU_WIDTH = 256
HEAD_DIM = 64
HALF = HEAD_DIM // 2
DIFF_HEADS = 8
NSA_HEADS = 16
NSA_GROUPS = 4
NSA_HPG = NSA_HEADS // NSA_GROUPS
CMP_BLOCK = 32
CMP_STRIDE = 16
SEL_BLOCK = 64
SEL_TOP = 16
WINDOW = 512
FORCED_SCORE = 1e4
ROPE_THETA = 10000.0
EPS = 1e-6
NEG_BLOCK = -1e9
NEG_TOKEN = -1e30

PROJ_TM = 2048
PROJ_TN = 512
PROJ_ROWS = 512
ATT_T = 512
DIFF_HEADS_PER_STEP = 2
NSA_T = 512
WIN_T = 256
SLC_CHAINS = 2
MERGE_TM = 512
MERGE_ROWS = 256
MLP_TM = 1024
MLP_TF = 1024
VMEM_LIMIT = 56 * 1024 * 1024

_NT = (((1,), (1,)), ((), ()))


def _dot(a, b):
    return jnp.dot(a, b, preferred_element_type=F32)


def _dot_nt(a, b):
    return lax.dot_general(a, b, _NT, preferred_element_type=F32)


def _split_bf16(x):
    hi = x.astype(BF16)
    lo = (x - hi.astype(F32)).astype(BF16)
    return hi, lo


def _swap_halves_within_heads(y):
    lane = lax.broadcasted_iota(jnp.int32, y.shape, 1)
    first_half = (lane % HEAD_DIM) < HALF
    return jnp.where(first_half, pltpu.roll(y, LANES - HALF, axis=1), pltpu.roll(y, HALF, axis=1))


def _rope(y, cos, sin_signed):
    return y * cos + _swap_halves_within_heads(y) * sin_signed


def _inproj_kernel(x_ref, g_ref, w_ref, gain_ref, cos_ref, sin_ref, bd_ref, ab_ref, gt_ref, h_scr,
                   *, n_rope, n_plain):
    j = pl.program_id(1)

    @pl.when(j == 0)
    def _():
        x = x_ref[...]
        ms = jnp.mean(x * x, axis=-1, keepdims=True)
        h_scr[...] = (x * lax.rsqrt(ms + EPS) * g_ref[...]).astype(BF16)

    wide = bd_ref.shape[0]
    row_chunks = [slice(r, r + PROJ_ROWS) for r in range(0, h_scr.shape[0], PROJ_ROWS)]

    @pl.when(j < n_rope)
    def _():
        for rs in row_chunks:
            acc = _dot(h_scr[rs, :], w_ref[...])
            cos = cos_ref[rs, :]
            sin = sin_ref[rs, :]
            for c in range(acc.shape[1] // wide):
                sl = slice(c * wide, (c + 1) * wide)
                xc = acc[:, sl]
                ms = _dot((xc * xc).astype(BF16), bd_ref[...])
                y = xc * lax.rsqrt(ms + EPS) * gain_ref[:, sl]
                for k in range(wide // LANES):
                    lo = sl.start + k * LANES
                    ab_ref[rs, lo:lo + LANES] = _rope(
                        y[:, k * LANES:(k + 1) * LANES], cos, sin).astype(ab_ref.dtype)

    @pl.when((j >= n_rope) & (j < n_rope + n_plain))
    def _():
        for rs in row_chunks:
            ab_ref[rs, :] = _dot(h_scr[rs, :], w_ref[...]).astype(ab_ref.dtype)

    @pl.when(j >= n_rope + n_plain)
    def _():
        for rs in row_chunks:
            gt_ref[rs, :] = jax.nn.sigmoid(_dot(h_scr[rs, :], w_ref[...]))


def _inproj(x2d, g, w, gain, cos, sin, bd, seq, n_rope, n_plain, n_gate):
    n, d = x2d.shape
    tm, tn = min(PROJ_TM, seq), PROJ_TN
    assert n % tm == 0 and seq % tm == 0 and w.shape[1] == (n_rope + n_plain + n_gate) * tn
    pos_blocks = seq // tm
    n_ab = n_rope + n_plain
    return pl.pallas_call(
        functools.partial(_inproj_kernel, n_rope=n_rope, n_plain=n_plain),
        out_shape=(jax.ShapeDtypeStruct((n, n_ab * tn), BF16), jax.ShapeDtypeStruct((n, n_gate * tn), F32)),
        grid=(n // tm, n_ab + n_gate),
        in_specs=[
            pl.BlockSpec((tm, d), lambda i, j: (i, 0)),
            pl.BlockSpec((1, d), lambda i, j: (0, 0)),
            pl.BlockSpec((d, tn), lambda i, j: (0, j)),
            pl.BlockSpec((1, tn), lambda i, j: (0, jnp.minimum(j, n_rope - 1))),
            pl.BlockSpec((tm, LANES), lambda i, j: (i % pos_blocks, 0)),
            pl.BlockSpec((tm, LANES), lambda i, j: (i % pos_blocks, 0)),
            pl.BlockSpec(bd.shape, lambda i, j: (0, 0)),
        ],
        out_specs=(
            pl.BlockSpec((tm, tn), lambda i, j: (i, jnp.minimum(j, n_ab - 1))),
            pl.BlockSpec((tm, tn), lambda i, j: (i, jnp.maximum(j - n_ab, 0))),
        ),
        scratch_shapes=[pltpu.VMEM((tm, d), BF16)],
        compiler_params=pltpu.CompilerParams(
            dimension_semantics=("parallel", "arbitrary"), vmem_limit_bytes=VMEM_LIMIT),
        name="inproj",
    )(x2d, g, w, gain, cos, sin, bd)


def _compress_kernel(x_ref, pos_ref, w1_ref, w2_ref, *rest, is_key):
    if is_key:
        gain_ref, cos_ref, sin_ref, o_ref, xf_ref = rest
    else:
        o_ref, xf_ref = rest
    n_runs = o_ref.shape[1]
    per_chunk = LANES // HEAD_DIM
    for c in range(xf_ref.shape[0]):
        xf_ref[c] = x_ref[:, c * LANES:(c + 1) * LANES].astype(F32)
    lane_group = lax.broadcasted_iota(jnp.int32, (n_runs, LANES), 1) // HEAD_DIM
    for g in range(NSA_GROUPS):
        halves = []
        for half in range(CMP_BLOCK // CMP_STRIDE):
            acc = None
            for tt in range(CMP_STRIDE):
                l = half * CMP_STRIDE + tt
                xt = xf_ref[g // per_chunk, pl.ds(tt, n_runs, stride=CMP_STRIDE), :] + pos_ref[l:l + 1, :]
                part = _dot(jnp.where(lane_group == g % per_chunk, xt, 0.0).astype(BF16), w1_ref[l])
                acc = part if acc is None else acc + part
            halves.append(acc)
        hidden = halves[0] + pltpu.roll(halves[1], n_runs - 1, axis=0)
        out = _dot(jax.nn.gelu(hidden).astype(BF16), w2_ref[...])
        if is_key:
            ms = jnp.mean(out * out, axis=-1, keepdims=True)
            out = _rope(out * lax.rsqrt(ms + EPS) * gain_ref[...], cos_ref[...], sin_ref[...])
        o_ref[g] = out.astype(o_ref.dtype)


def _compress(ab, col0, seq, pos, w1, w2dup, extra, is_key):
    batch = ab.shape[0] // seq
    width = NSA_GROUPS * HEAD_DIM
    n_runs = seq // CMP_STRIDE
    full = lambda arr: pl.BlockSpec(arr.shape, lambda b: (0,) * arr.ndim)
    return pl.pallas_call(
        functools.partial(_compress_kernel, is_key=is_key),
        out_shape=jax.ShapeDtypeStruct((batch, NSA_GROUPS, n_runs, LANES), BF16),
        grid=(batch,),
        in_specs=[pl.BlockSpec((seq, width), lambda b: (b, col0 // width)), full(pos), full(w1), full(w2dup)]
        + [full(e) for e in extra],
        out_specs=pl.BlockSpec((None, NSA_GROUPS, n_runs, LANES), lambda b: (b, 0, 0, 0)),
        scratch_shapes=[pltpu.VMEM((width // LANES, seq, LANES), F32)],
        compiler_params=pltpu.CompilerParams(dimension_semantics=("parallel",), vmem_limit_bytes=VMEM_LIMIT),
        name="compress_k" if is_key else "compress_v",
    )(ab, pos, w1, w2dup, *extra)


def _online_update(s, v_ones, m_ref, acc_ref, idx):
    m_prev = m_ref[idx]
    m_next = jnp.maximum(m_prev, jnp.max(s, axis=1, keepdims=True))
    alpha = jnp.exp2(m_prev - m_next)
    p = jnp.exp2(s - jnp.concatenate([m_next] * (s.shape[1] // LANES), axis=1))
    m_ref[idx] = m_next
    scale = jnp.concatenate([alpha] * (acc_ref.shape[-1] // LANES), axis=1)
    acc_ref[idx] = acc_ref[idx] * scale + _dot(p.astype(BF16), v_ones)


def _init_softmax_state(m_ref, acc_ref):
    m_ref[...] = jnp.full(m_ref.shape, -jnp.inf, F32)
    acc_ref[...] = jnp.zeros(acc_ref.shape, F32)


def _add_per_head(s, bias, heads):
    rows, cols = bias.shape
    return (s.reshape(heads, rows, cols) + bias[None]).reshape(heads * rows, cols)


def _diff_attn_kernel(q_ref, k_ref, v_ref, cb_ref, lq1_ref, lk1_ref, lq2_ref, lk2_ref, sg_ref, o_ref,
                      m_ref, acc_ref, vones_ref, s_ref, *, lambda_init):
    i = pl.program_id(2)
    t = q_ref.shape[0]
    dv = 2 * HEAD_DIM
    n_heads = q_ref.shape[1] // LANES
    chunk = lambda hd: slice(hd * LANES, (hd + 1) * LANES)

    @pl.when(i == 0)
    def _():
        for hd in range(n_heads):
            vones_ref[hd, :, :dv] = v_ref[:, chunk(hd)]
            vones_ref[hd, :, dv:] = jnp.ones((v_ref.shape[0], vones_ref.shape[2] - dv), BF16)

    lane = lax.broadcasted_iota(jnp.int32, (t, LANES), 1)
    zero = jnp.zeros((t, LANES), q_ref.dtype)
    qc = []
    for hd in range(n_heads):
        q = q_ref[:, chunk(hd)]
        qc += [jnp.where(lane < HEAD_DIM, q, zero), jnp.where(lane >= HEAD_DIM, q, zero)]
    _init_softmax_state(m_ref, acc_ref)

    def scores(j, slot):
        rows = pl.ds(pl.multiple_of(j * t, t), t)
        for hd in range(n_heads):
            k = k_ref[rows, chunk(hd)]
            for c in range(2):
                s_ref[slot, 2 * hd + c] = _dot_nt(qc[2 * hd + c], k)

    def update(j, slot, masked):
        rows = pl.ds(pl.multiple_of(j * t, t), t)
        for hd in range(n_heads):
            v = vones_ref[hd, rows, :]
            for c in range(2):
                s = s_ref[slot, 2 * hd + c]
                if masked:
                    s = s + cb_ref[...]
                _online_update(s, v, m_ref, acc_ref, 2 * hd + c)

    scores(0, 0)

    def body(pp, carry):
        scores(2 * pp + 1, 1)
        update(2 * pp, 0, False)
        scores(2 * pp + 2, 0)
        update(2 * pp + 1, 1, False)
        return carry

    lax.fori_loop(0, i // 2, body, 0)

    @pl.when(i % 2 == 0)
    def _():
        update(i, 0, True)

    @pl.when(i % 2 == 1)
    def _():
        scores(i, 1)
        update(i - 1, 0, False)
        update(i, 1, True)

    lam = (jnp.exp(jnp.sum(lq1_ref[...] * lk1_ref[...], axis=-1, keepdims=True))
           - jnp.exp(jnp.sum(lq2_ref[...] * lk2_ref[...], axis=-1, keepdims=True)) + lambda_init)
    for hd in range(n_heads):
        c0, c1 = 2 * hd, 2 * hd + 1
        o = acc_ref[c0, :, :dv] / acc_ref[c0, :, dv:] - lam * (acc_ref[c1, :, :dv] / acc_ref[c1, :, dv:])
        ms = jnp.mean(o * o, axis=-1, keepdims=True)
        o_ref[:, chunk(hd)] = (o * lax.rsqrt(ms + EPS) * sg_ref[...] * (1.0 - lambda_init)).astype(o_ref.dtype)


def _diff_attention(ab, causal_bias, lams, subln_g, batch, seq, lambda_init, col):
    t = causal_bias.shape[0]
    nq = seq // t
    dv = 2 * HEAD_DIM
    hps = DIFF_HEADS_PER_STEP
    width = hps * LANES
    chains = 2 * hps
    assert dv == LANES and DIFF_HEADS % hps == 0
    small = lambda arr: pl.BlockSpec(arr.shape, lambda b, h, i: (0, 0))
    return pl.pallas_call(
        functools.partial(_diff_attn_kernel, lambda_init=lambda_init),
        out_shape=jax.ShapeDtypeStruct((batch * seq, DIFF_HEADS * dv), BF16),
        grid=(batch, DIFF_HEADS // hps, nq),
        in_specs=[
            pl.BlockSpec((t, width), lambda b, h, i: (b * nq + i, col["dq"] // width + h)),
            pl.BlockSpec((seq, width), lambda b, h, i: (b, col["dk"] // width + h)),
            pl.BlockSpec((seq, width), lambda b, h, i: (b, col["dv"] // width + h)),
            small(causal_bias),
            small(lams[0]), small(lams[1]), small(lams[2]), small(lams[3]), small(subln_g),
        ],
        out_specs=pl.BlockSpec((t, width), lambda b, h, i: (b * nq + i, h)),
        scratch_shapes=[
            pltpu.VMEM((chains, t, LANES), F32),
            pltpu.VMEM((chains, t, 2 * dv), F32),
            pltpu.VMEM((hps, seq, 2 * dv), BF16),
            pltpu.VMEM((2, chains, t, t), F32),
        ],
        compiler_params=pltpu.CompilerParams(
            dimension_semantics=("parallel", "parallel", "arbitrary"), vmem_limit_bytes=VMEM_LIMIT),
        name="diff_attention",
    )(ab, ab, ab, causal_bias, *lams, subln_g)


def _nsa_kernel(q_ref, ksl_ref, vsl_ref, kwn_ref, vwn_ref, kc_ref, vc_ref, gate_ref, ovt_ref, cmpb_ref, cb_ref,
                wb_ref, o_ref, qaug_ref, m_ref, acc_ref, accw_ref, imp_ref, kaug_ref, vsl1_ref, vwn1_ref,
                q4_ref, part_ref, rank_ref):
    i = pl.program_id(2)
    t = q_ref.shape[0]
    seq = ksl_ref.shape[0]
    q0 = i * t
    n_sel = seq // SEL_BLOCK
    assert n_sel <= LANES - HEAD_DIM and imp_ref.shape[0] == LANES
    lane = lax.broadcasted_iota(jnp.int32, (t, LANES), 1)
    low = lane < HEAD_DIM

    @pl.when(i == 0)
    def _():
        row = lax.broadcasted_iota(jnp.int32, (seq, LANES), 0)
        ln = lax.broadcasted_iota(jnp.int32, (seq, LANES), 1)
        kaug_ref[...] = ksl_ref[...] + jnp.where(ln - HEAD_DIM == row // SEL_BLOCK, 1.0, 0.0).astype(BF16)
        ones_hi = jnp.where(ln >= HEAD_DIM, 1.0, 0.0).astype(BF16)
        vsl1_ref[...] = vsl_ref[...] + ones_hi
        vwn1_ref[...] = vwn_ref[...] + ones_hi

    q = q_ref[...].astype(F32)
    heads = []
    for pair in range(NSA_HPG // 2):
        qp = q[:, pair * LANES:(pair + 1) * LANES]
        heads += [qp, pltpu.roll(qp, HEAD_DIM, axis=1)]

    for hh, hq in enumerate(heads):
        q4_ref[hh] = jnp.where(low, hq, 0.0).astype(BF16)
    rows = NSA_HPG * t

    tw = wb_ref.shape[1]
    wk = wb_ref.shape[2]
    assert wk == WINDOW + tw and WINDOW % tw == 0 and t % tw == 0
    for sub in range(t // tw):
        qs = q4_ref[:, sub * tw:(sub + 1) * tw, :].reshape(NSA_HPG * tw, LANES)
        sub_idx = i * (t // tw) + sub
        kstart = pl.multiple_of(jnp.maximum(q0 + sub * tw - WINDOW, 0), tw)
        s = _dot_nt(qs, kwn_ref[pl.ds(kstart, wk), :])
        s = _add_per_head(s, wb_ref[jnp.minimum(sub_idx, WINDOW // tw)], NSA_HPG)
        e = jnp.exp2(s - jnp.max(s, axis=1, keepdims=True))
        acc_w =_dot(e.astype(BF16), vwn1_ref[pl.ds(kstart, wk), :])
        accw_ref[:, sub * tw:(sub + 1) * tw, :] = acc_w.reshape(NSA_HPG, tw, LANES)

    s = _add_per_head(_dot_nt(q4_ref[...].reshape(rows, LANES), kc_ref[...]), cmpb_ref[...], NSA_HPG)
    e = jnp.exp2(s - jnp.max(s, axis=1, keepdims=True))
    p = e / jnp.sum(e, axis=1, keepdims=True)
    o_cmp = _dot(p.astype(BF16), vc_ref[...])

    denom = jnp.ones((t, LANES), F32)
    for hh in range(NSA_HPG):
        denom = jnp.where(lane == HEAD_DIM + 3 * hh + 2, accw_ref[hh], denom)
    coef = gate_ref[...] / denom
    has_cmp = q0 + lax.broadcasted_iota(jnp.int32, (t, 1), 0) >= CMP_BLOCK - 1
    for hh in range(NSA_HPG):
        c0 = HEAD_DIM + 3 * hh
        part_ref[pl.ds(hh * t, t), :] = (jnp.where(has_cmp, coef[:, c0:c0 + 1], 0.0) * o_cmp[hh * t:(hh + 1) * t]
                                         + coef[:, c0 + 2:c0 + 3] * accw_ref[hh])

    psum = p[0:t] + p[t:2 * t] + p[2 * t:3 * t] + p[3 * t:4 * t]
    hi, lo = _split_bf16(psum)
    ovt = ovt_ref[...]
    imp = _dot_nt(ovt, hi) + _dot_nt(ovt, lo)
    jrow = lax.broadcasted_iota(jnp.int32, imp.shape, 0) - SEL_BLOCK
    qcol = q0 + lax.broadcasted_iota(jnp.int32, imp.shape, 1)
    cur = qcol // SEL_BLOCK
    forced = (jrow == 0) | (jrow == cur) | (jrow == cur - 1)
    imp = jnp.where(forced, FORCED_SCORE, imp)
    imp = jnp.where((jrow >= 0) & (jrow * SEL_BLOCK <= qcol), imp, -jnp.inf)
    imp_ref[...] = imp
    imp_b = imp[SEL_BLOCK:, :]
    jb = lax.broadcasted_iota(jnp.int32, imp_b.shape, 0)
    per_tile = t // SEL_BLOCK
    rank_ref[...] = jnp.zeros(rank_ref.shape, F32)

    def count_group(g8):
        rank = rank_ref[...]
        for kk in range(g8 * per_tile, (g8 + 1) * per_tile):
            rk = imp_ref[SEL_BLOCK + kk:SEL_BLOCK + kk + 1, :]
            rank = rank + jnp.where(rk > imp_b, 1.0, jnp.where((rk == imp_b) & (jb > kk), 1.0, 0.0))
        rank_ref[...] = rank

    count_group(0)
    for g8 in range(1, n_sel // per_tile):
        pl.when(i >= g8)(functools.partial(count_group, g8))
    chosen = (rank_ref[...] < float(min(SEL_TOP, n_sel))) & (imp_b > -jnp.inf)
    bias_t = jnp.concatenate([jnp.zeros((SEL_BLOCK, t), F32), jnp.where(chosen, 0.0, NEG_BLOCK)], axis=0)
    bias = bias_t.T
    for hh, hq in enumerate(heads):
        qaug_ref[hh] = jnp.where(low, hq, bias).astype(BF16)

    _init_softmax_state(m_ref, acc_ref)

    def slc_step(j, masked):
        start = pl.multiple_of(j * t, t)
        k = kaug_ref[pl.ds(start, t), :]
        v = vsl1_ref[pl.ds(start, t), :]
        chains = SLC_CHAINS if masked else 1
        per_chain = NSA_HPG // chains
        for ch in range(chains):
            qs = qaug_ref[ch * per_chain:(ch + 1) * per_chain].reshape(per_chain * t, LANES)
            s = _dot_nt(qs, k)
            if masked:
                s = _add_per_head(s, cb_ref[...], per_chain)
            _online_update(s, v, m_ref, acc_ref, (0, pl.ds(ch * per_chain * t, per_chain * t)))

    def slc_body(j, carry):
        slc_step(j, False)
        return carry

    lax.fori_loop(0, i, slc_body, 0)
    slc_step(i, True)

    slc_acc = lambda hh: acc_ref[0, pl.ds(hh * t, t), :]
    denom = jnp.ones((t, LANES), F32)
    for hh in range(NSA_HPG):
        denom = jnp.where(lane == HEAD_DIM + 3 * hh + 1, slc_acc(hh), denom)
    coef = gate_ref[...] / denom
    outs = []
    for hh in range(NSA_HPG):
        c1 = HEAD_DIM + 3 * hh + 1
        outs.append(part_ref[pl.ds(hh * t, t), :] + coef[:, c1:c1 + 1] * slc_acc(hh))
    for pair in range(NSA_HPG // 2):
        o_ref[:, pair * LANES:(pair + 1) * LANES] = jnp.where(
            low, outs[2 * pair], pltpu.roll(outs[2 * pair + 1], HEAD_DIM, axis=1)).astype(o_ref.dtype)


def _nsa_attention(ab, gates, kcmp, vcmp, ovt, cmp_bias, causal_bias, window_bias, batch, seq, col, col_gate):
    t = causal_bias.shape[0]
    nq = seq // t
    gw = NSA_HPG * HEAD_DIM
    rows = NSA_HPG * t
    n_cmp = kcmp.shape[2]
    full = lambda arr: pl.BlockSpec(arr.shape, lambda b, g, i: (0,) * arr.ndim)
    kv = lambda name: pl.BlockSpec((seq, LANES), lambda b, g, i: (b, col[name] // LANES + g))
    return pl.pallas_call(
        _nsa_kernel,
        out_shape=jax.ShapeDtypeStruct((batch * seq, NSA_HEADS * HEAD_DIM), BF16),
        grid=(batch, NSA_GROUPS, nq),
        in_specs=[
            pl.BlockSpec((t, gw), lambda b, g, i: (b * nq + i, col["nq"] // gw + g)),
            kv("ksl"), kv("vsl"), kv("kwn"), kv("vwn"),
            pl.BlockSpec((None, None, n_cmp, LANES), lambda b, g, i: (b, g, 0, 0)),
            pl.BlockSpec((None, None, n_cmp, LANES), lambda b, g, i: (b, g, 0, 0)),
            pl.BlockSpec((t, LANES), lambda b, g, i: (b * nq + i, col_gate // LANES + g)),
            full(ovt),
            pl.BlockSpec((t, n_cmp), lambda b, g, i: (i, 0)),
            full(causal_bias),
            full(window_bias),
        ],
        out_specs=pl.BlockSpec((t, gw), lambda b, g, i: (b * nq + i, g)),
        scratch_shapes=[
            pltpu.VMEM((NSA_HPG, t, LANES), BF16),
            pltpu.VMEM((1, rows, LANES), F32),
            pltpu.VMEM((1, rows, LANES), F32),
            pltpu.VMEM((NSA_HPG, t, LANES), F32),
            pltpu.VMEM((LANES, t), F32),
            pltpu.VMEM((seq, LANES), BF16),
            pltpu.VMEM((seq, LANES), BF16),
            pltpu.VMEM((seq, LANES), BF16),
            pltpu.VMEM((NSA_HPG, t, LANES), BF16),
            pltpu.VMEM((rows, LANES), F32),
            pltpu.VMEM((SEL_BLOCK, t), F32),
        ],
        compiler_params=pltpu.CompilerParams(
            dimension_semantics=("parallel", "parallel", "arbitrary"), vmem_limit_bytes=VMEM_LIMIT),
        name="nsa_attention",
    )(ab, ab, ab, ab, ab, kcmp, vcmp, gates, ovt, cmp_bias, causal_bias, window_bias)


def _merge_kernel(yd_ref, yn_ref, g0_ref, g1_ref, x_ref, wd_ref, wn_ref, wo_ref, lg_ref, x1_ref, h_ref):
    for r in range(0, x_ref.shape[0], MERGE_ROWS):
        rs = slice(r, r + MERGE_ROWS)
        mixed = (g0_ref[rs, :] * _dot(yd_ref[rs, :], wd_ref[...])
                 + g1_ref[rs, :] * _dot(yn_ref[rs, :], wn_ref[...]))
        x1 = x_ref[rs, :] + _dot(mixed.astype(BF16), wo_ref[...])
        x1_ref[rs, :] = x1
        ms = jnp.mean(x1 * x1, axis=-1, keepdims=True)
        h_ref[rs, :] = (x1 * lax.rsqrt(ms + EPS) * lg_ref[...]).astype(h_ref.dtype)


def _merge(yd, yn, gates, x2d, wd, wn, wo, lg):
    n, d = x2d.shape
    tm = min(MERGE_TM, n)
    tok = lambda c: pl.BlockSpec((tm, d), lambda i: (i, c))
    full = lambda arr: pl.BlockSpec(arr.shape, lambda i: (0, 0))
    return pl.pallas_call(
        _merge_kernel,
        out_shape=(jax.ShapeDtypeStruct((n, d), F32), jax.ShapeDtypeStruct((n, d), BF16)),
        grid=(n // tm,),
        in_specs=[tok(0), tok(0), tok(0), tok(1), tok(0), full(wd), full(wn), full(wo), full(lg)],
        out_specs=(tok(0), tok(0)),
        compiler_params=pltpu.CompilerParams(
            dimension_semantics=("parallel",), vmem_limit_bytes=VMEM_LIMIT),
        name="merge_outproj",
    )(yd, yn, gates, gates, x2d, wd, wn, wo, lg)


def _mlp_kernel(h_ref, x1_ref, wu_ref, wd_ref, o_ref, acc_ref):
    f = pl.program_id(1)

    @pl.when(f == 0)
    def _():
        acc_ref[...] = x1_ref[...]

    up = jnp.maximum(_dot(h_ref[...], wu_ref[...]), 0.0)
    acc_ref[...] += _dot((up * up).astype(BF16), wd_ref[...])

    @pl.when(f == pl.num_programs(1) - 1)
    def _():
        o_ref[...] = acc_ref[...]


def _mlp(h, x1, wu, wd):
    n, d = x1.shape
    dff = wu.shape[1]
    tm, tf = min(MLP_TM, n), MLP_TF
    return pl.pallas_call(
        _mlp_kernel,
        out_shape=jax.ShapeDtypeStruct((n, d), F32),
        grid=(n // tm, dff // tf),
        in_specs=[
            pl.BlockSpec((tm, d), lambda i, f: (i, 0)),
            pl.BlockSpec((tm, d), lambda i, f: (i, 0)),
            pl.BlockSpec((d, tf), lambda i, f: (0, f)),
            pl.BlockSpec((tf, d), lambda i, f: (f, 0)),
        ],
        out_specs=pl.BlockSpec((tm, d), lambda i, f: (i, 0)),
        scratch_shapes=[pltpu.VMEM((tm, d), F32)],
        compiler_params=pltpu.CompilerParams(
            dimension_semantics=("parallel", "arbitrary"), vmem_limit_bytes=VMEM_LIMIT),
        name="mlp",
    )(h, x1, wu, wd)


def _rope_tables(pos):
    inv_freq = ROPE_THETA ** (-jnp.arange(HALF, dtype=F32) / HALF)
    ang = pos.astype(F32)[:, None] * inv_freq[None, :]
    c, s = jnp.cos(ang), jnp.sin(ang)
    return jnp.concatenate([c, c, c, c], axis=-1), jnp.concatenate([-s, s, -s, s], axis=-1)


def _pad_groups(w, width, offset=0):
    d = w.shape[0]
    w = w.reshape(d, NSA_GROUPS, width)
    return jnp.pad(w, ((0, 0), (0, 0), (offset, LANES - width - offset))).reshape(d, NSA_GROUPS * LANES)


def _additive_mask(valid):
    return jnp.asarray(np.where(valid, 0.0, NEG_TOKEN), F32)


def _layer(x2d, batch, seq, layer, ln_mix_g, w_in, diff_q_norm_g, diff_k_norm_g, diff_lambda_q1,
           diff_lambda_k1, diff_lambda_q2, diff_lambda_k2, diff_subln_g, nsa_q_norm_g, nsa_k_norm_g,
           cmp_pos_k, cmp_pos_v, cmp_k_w1, cmp_k_w2, cmp_v_w1, cmp_v_w2, w_proj_diff, w_proj_nsa,
           w_out, ln_mlp_g, w_mlp_up, w_mlp_down):
    d = x2d.shape[1]
    diff_qk = DIFF_HEADS * 2 * HEAD_DIM
    diff_v = DIFF_HEADS * 2 * HEAD_DIM
    nsa_q = NSA_HEADS * HEAD_DIM
    nsa_kv = NSA_GROUPS * HEAD_DIM
    splits = np.cumsum([diff_qk, diff_qk, diff_v, nsa_q] + [nsa_kv] * 6 + [NSA_HEADS * 3, 2 * d])[:-1]
    (w_dq, w_dk, w_dv, w_nq, w_kc, w_vc, w_ksl, w_vsl, w_kwn, w_vwn, w_ng, w_mg) = jnp.split(
        w_in, [int(c) for c in splits], axis=1)
    scale = HEAD_DIM ** -0.5 * math.log2(math.e)
    pad_kv = lambda w: _pad_groups(w, HEAD_DIM)

    fam_rope = [("dq", w_dq), ("dk", w_dk), ("nq", w_nq), ("ksl", pad_kv(w_ksl)), ("kwn", pad_kv(w_kwn))]
    fam_plain = [("dv", w_dv), ("vsl", pad_kv(w_vsl)), ("vwn", pad_kv(w_vwn)), ("kc", w_kc), ("vc", w_vc)]
    fam_gate = [("mg", w_mg), ("ng", _pad_groups(w_ng, NSA_HPG * 3, HEAD_DIM))]
    col, blocks, tiles = {}, [], []
    for fam in (fam_rope, fam_plain, fam_gate):
        start = sum(b.shape[1] for b in blocks)
        for name, w in fam:
            col[name] = sum(b.shape[1] for b in blocks)
            blocks.append(w)
        width = sum(b.shape[1] for b in blocks) - start
        pad = -width % PROJ_TN
        if pad:
            blocks.append(jnp.zeros((d, pad), w_in.dtype))
        tiles.append((width + pad) // PROJ_TN)
    w_all = jnp.concatenate(blocks, axis=1).astype(BF16)
    gate_base = (tiles[0] + tiles[1]) * PROJ_TN

    tile_g = lambda g, reps: jnp.tile(g.astype(F32), reps)
    gain = jnp.concatenate([
        tile_g(diff_q_norm_g, 2 * DIFF_HEADS) * scale,
        tile_g(diff_k_norm_g, 2 * DIFF_HEADS),
        tile_g(nsa_q_norm_g, NSA_HEADS) * scale,
        tile_g(nsa_k_norm_g[1], 2 * NSA_GROUPS),
        tile_g(nsa_k_norm_g[2], 2 * NSA_GROUPS),
    ])[None, :]
    assert gain.shape[1] == tiles[0] * PROJ_TN
    cos, sin = _rope_tables(jnp.arange(seq))
    mean_heads = np.kron(np.eye(MXU_WIDTH // HEAD_DIM), np.full((HEAD_DIM, HEAD_DIM), 1.0 / HEAD_DIM))
    ab, gates = _inproj(x2d, ln_mix_g.astype(F32)[None, :], w_all, gain, cos, sin,
                        jnp.asarray(mean_heads, BF16), seq, *tiles)

    n_runs = seq // CMP_STRIDE
    dup = lambda w: jnp.concatenate([w, w], axis=1).astype(BF16)
    reps = LANES // HEAD_DIM
    per_group = lambda pos: jnp.tile(pos.astype(F32), (1, reps))
    rows_per_group = lambda w1: jnp.tile(
        w1.reshape(CMP_BLOCK, HEAD_DIM, w1.shape[1]), (1, reps, 1)).astype(BF16)
    cmp_center = jnp.arange(n_runs) * CMP_STRIDE + (CMP_BLOCK - 1) / 2.0
    cos_c, sin_c = _rope_tables(cmp_center)
    gain_c = tile_g(nsa_k_norm_g[0], 2)[None, :]
    kcmp = _compress(ab, col["kc"], seq, per_group(cmp_pos_k), rows_per_group(cmp_k_w1), dup(cmp_k_w2),
                     (gain_c, cos_c, sin_c), True)
    vcmp = _compress(ab, col["vc"], seq, per_group(cmp_pos_v), rows_per_group(cmp_v_w1), dup(cmp_v_w2),
                     (), False)

    lambda_init = 0.8 - 0.6 * math.exp(-0.3 * layer)
    lams = [v.astype(F32)[None, :] for v in (diff_lambda_q1, diff_lambda_k1, diff_lambda_q2, diff_lambda_k2)]
    t_att = min(ATT_T, seq)
    tri = np.arange(t_att)
    y_diff = _diff_attention(ab, _additive_mask(tri[None, :] <= tri[:, None]), lams,
                             diff_subln_g.astype(F32)[None, :], batch, seq, lambda_init, col)

    n_sel = seq // SEL_BLOCK
    cmp_start = np.arange(n_runs) * CMP_STRIDE
    sel_start = np.arange(n_sel) * SEL_BLOCK
    overlap = ((cmp_start[:, None] < sel_start[None, :] + SEL_BLOCK)
               & (cmp_start[:, None] + CMP_BLOCK - 1 >= sel_start[None, :]))
    ovt = np.zeros((LANES, n_runs))
    ovt[HEAD_DIM:HEAD_DIM + n_sel] = overlap.T
    cmp_bias = _additive_mask(cmp_start[None, :] + CMP_BLOCK - 1 <= np.arange(seq)[:, None])
    t_nsa = min(NSA_T, seq)
    tri = np.arange(t_nsa)
    t_win = min(WIN_T, t_nsa)
    r = np.arange(t_win)[:, None]
    c = np.arange(WINDOW + t_win)[None, :]
    window_bias = jnp.stack([_additive_mask((c <= r + off) & (r + off - c < WINDOW))
                             for off in range(0, WINDOW + 1, t_win)])
    y_nsa = _nsa_attention(ab, gates, kcmp, vcmp, jnp.asarray(ovt, BF16), cmp_bias,
                           _additive_mask(tri[None, :] <= tri[:, None]), window_bias, batch, seq, col,
                           col["ng"] - gate_base)

    assert col["mg"] == gate_base
    x1, h2 = _merge(y_diff, y_nsa, gates, x2d, w_proj_diff.astype(BF16), w_proj_nsa.astype(BF16),
                    w_out.astype(BF16), ln_mlp_g.astype(F32)[None, :])
    return _mlp(h2, x1, w_mlp_up.astype(BF16), w_mlp_down.astype(BF16))


def kernel(x, ln_mix_g, w_in, diff_q_norm_g, diff_k_norm_g, diff_lambda_q1, diff_lambda_k1, diff_lambda_q2, diff_lambda_k2, diff_subln_g, nsa_q_norm_g, nsa_k_norm_g, cmp_pos_k, cmp_pos_v, cmp_k_w1, cmp_k_w2, cmp_v_w1, cmp_v_w2, w_proj_diff, w_proj_nsa, w_out, ln_mlp_g, w_mlp_up, w_mlp_down):
    batch, seq, d = x.shape
    params = (ln_mix_g, w_in, diff_q_norm_g, diff_k_norm_g, diff_lambda_q1, diff_lambda_k1, diff_lambda_q2,
              diff_lambda_k2, diff_subln_g, nsa_q_norm_g, nsa_k_norm_g, cmp_pos_k, cmp_pos_v, cmp_k_w1,
              cmp_k_w2, cmp_v_w1, cmp_v_w2, w_proj_diff, w_proj_nsa, w_out, ln_mlp_g, w_mlp_up, w_mlp_down)
    x2d = x.reshape(batch * seq, d)
    for layer in range(ln_mix_g.shape[0]):
        x2d = _layer(x2d, batch, seq, layer, *[prm[layer] for prm in params])
    return x2d.reshape(batch, seq, d)
```

```python
import functools
import math

import numpy as np
import jax
import jax.numpy as jnp
from jax import lax
from jax.experimental import pallas as pl
from jax.experimental.pallas import tpu as pltpu

F32 = jnp.float32
BF16 = jnp.bfloat16

LANES = 128
MXU_WIDTH = 256
HEAD_DIM = 64
HALF = HEAD_DIM // 2
DIFF_HEADS = 8
NSA_HEADS = 16
NSA_GROUPS = 4
NSA_HPG = NSA_HEADS // NSA_GROUPS
CMP_BLOCK = 32
CMP_STRIDE = 16
SEL_BLOCK = 64
SEL_TOP = 16
WINDOW = 512
FORCED_SCORE = 1e4
ROPE_THETA = 10000.0
EPS = 1e-6
NEG_BLOCK = -3e38
NEG_TOKEN = -1e30

PROJ_TM = 2048
PROJ_TN = 512
PROJ_ROWS = 512
ATT_T = 512
DIFF_HEADS_PER_STEP = 2
NSA_T = 512
WIN_T = 256
SLC_CHAINS = 2
MERGE_TM = 512
MLP_TM = 1024
MLP_TF = 1024
VMEM_LIMIT = 56 * 1024 * 1024

_NT = (((1,), (1,)), ((), ()))


def _dot(a, b):
    return jnp.dot(a, b, preferred_element_type=F32)


def _dot_nt(a, b):
    return lax.dot_general(a, b, _NT, preferred_element_type=F32)


def _split_bf16(x):
    hi = x.astype(BF16)
    lo = (x - hi.astype(F32)).astype(BF16)
    return hi, lo


def _swap_halves_within_heads(y):
    lane = lax.broadcasted_iota(jnp.int32, y.shape, 1)
    first_half = (lane % HEAD_DIM) < HALF
    return jnp.where(first_half, pltpu.roll(y, LANES - HALF, axis=1), pltpu.roll(y, HALF, axis=1))


def _rope(y, cos, sin_signed):
    return y * cos + _swap_halves_within_heads(y) * sin_signed


def _inproj_kernel(x_ref, g_ref, w_ref, gain_ref, cos_ref, sin_ref, bd_ref, ab_ref, gt_ref, h_scr,
                   *, n_rope, n_plain):
    j = pl.program_id(1)

    @pl.when(j == 0)
    def _():
        x = x_ref[...]
        ms = jnp.mean(x * x, axis=-1, keepdims=True)
        h_scr[...] = (x * lax.rsqrt(ms + EPS) * g_ref[...]).astype(BF16)

    wide = bd_ref.shape[0]
    row_chunks = [slice(r, r + PROJ_ROWS) for r in range(0, h_scr.shape[0], PROJ_ROWS)]

    @pl.when(j < n_rope)
    def _():
        for rs in row_chunks:
            acc = _dot(h_scr[rs, :], w_ref[...])
            cos = cos_ref[rs, :]
            sin = sin_ref[rs, :]
            for c in range(acc.shape[1] // wide):
                sl = slice(c * wide, (c + 1) * wide)
                xc = acc[:, sl]
                ms = _dot((xc * xc).astype(BF16), bd_ref[...])
                y = xc * lax.rsqrt(ms + EPS) * gain_ref[:, sl]
                for k in range(wide // LANES):
                    lo = sl.start + k * LANES
                    ab_ref[rs, lo:lo + LANES] = _rope(
                        y[:, k * LANES:(k + 1) * LANES], cos, sin).astype(ab_ref.dtype)

    @pl.when((j >= n_rope) & (j < n_rope + n_plain))
    def _():
        for rs in row_chunks:
            ab_ref[rs, :] = _dot(h_scr[rs, :], w_ref[...]).astype(ab_ref.dtype)

    @pl.when(j >= n_rope + n_plain)
    def _():
        for rs in row_chunks:
            gt_ref[rs, :] = jax.nn.sigmoid(_dot(h_scr[rs, :], w_ref[...]))


def _inproj(x2d, g, w, gain, cos, sin, bd, seq, n_rope, n_plain, n_gate):
    n, d = x2d.shape
    tm, tn = min(PROJ_TM, seq), PROJ_TN
    assert n % tm == 0 and seq % tm == 0 and w.shape[1] == (n_rope + n_plain + n_gate) * tn
    pos_blocks = seq // tm
    n_ab = n_rope + n_plain
    return pl.pallas_call(
        functools.partial(_inproj_kernel, n_rope=n_rope, n_plain=n_plain),
        out_shape=(jax.ShapeDtypeStruct((n, n_ab * tn), BF16), jax.ShapeDtypeStruct((n, n_gate * tn), F32)),
        grid=(n // tm, n_ab + n_gate),
        in_specs=[
            pl.BlockSpec((tm, d), lambda i, j: (i, 0)),
            pl.BlockSpec((1, d), lambda i, j: (0, 0)),
            pl.BlockSpec((d, tn), lambda i, j: (0, j)),
            pl.BlockSpec((1, tn), lambda i, j: (0, jnp.minimum(j, n_rope - 1))),
            pl.BlockSpec((tm, LANES), lambda i, j: (i % pos_blocks, 0)),
            pl.BlockSpec((tm, LANES), lambda i, j: (i % pos_blocks, 0)),
            pl.BlockSpec(bd.shape, lambda i, j: (0, 0)),
        ],
        out_specs=(
            pl.BlockSpec((tm, tn), lambda i, j: (i, jnp.minimum(j, n_ab - 1))),
            pl.BlockSpec((tm, tn), lambda i, j: (i, jnp.maximum(j - n_ab, 0))),
        ),
        scratch_shapes=[pltpu.VMEM((tm, d), BF16)],
        compiler_params=pltpu.CompilerParams(
            dimension_semantics=("parallel", "arbitrary"), vmem_limit_bytes=VMEM_LIMIT),
        name="inproj",
    )(x2d, g, w, gain, cos, sin, bd)


def _compress_kernel(x_ref, pos_ref, w1_ref, w2_ref, *rest, is_key):
    if is_key:
        gain_ref, cos_ref, sin_ref, o_ref, xf_ref = rest
    else:
        o_ref, xf_ref = rest
    n_runs = o_ref.shape[1]
    per_chunk = LANES // HEAD_DIM
    for c in range(xf_ref.shape[0]):
        xf_ref[c] = x_ref[:, c * LANES:(c + 1) * LANES].astype(F32)
    lane_group = lax.broadcasted_iota(jnp.int32, (n_runs, LANES), 1) // HEAD_DIM
    for g in range(NSA_GROUPS):
        halves = []
        for half in range(CMP_BLOCK // CMP_STRIDE):
            acc = None
            for tt in range(CMP_STRIDE):
                l = half * CMP_STRIDE + tt
                xt = xf_ref[g // per_chunk, pl.ds(tt, n_runs, stride=CMP_STRIDE), :] + pos_ref[l:l + 1, :]
                part = _dot(jnp.where(lane_group == g % per_chunk, xt, 0.0).astype(BF16), w1_ref[l])
                acc = part if acc is None else acc + part
            halves.append(acc)
        hidden = halves[0] + pltpu.roll(halves[1], n_runs - 1, axis=0)
        out = _dot(jax.nn.gelu(hidden).astype(BF16), w2_ref[...])
        if is_key:
            ms = jnp.mean(out * out, axis=-1, keepdims=True)
            out = _rope(out * lax.rsqrt(ms + EPS) * gain_ref[...], cos_ref[...], sin_ref[...])
        o_ref[g] = out.astype(o_ref.dtype)


def _compress(ab, col0, seq, pos, w1, w2dup, extra, is_key):
    batch = ab.shape[0] // seq
    width = NSA_GROUPS * HEAD_DIM
    n_runs = seq // CMP_STRIDE
    full = lambda arr: pl.BlockSpec(arr.shape, lambda b: (0,) * arr.ndim)
    return pl.pallas_call(
        functools.partial(_compress_kernel, is_key=is_key),
        out_shape=jax.ShapeDtypeStruct((batch, NSA_GROUPS, n_runs, LANES), BF16),
        grid=(batch,),
        in_specs=[pl.BlockSpec((seq, width), lambda b: (b, col0 // width)), full(pos), full(w1), full(w2dup)]
        + [full(e) for e in extra],
        out_specs=pl.BlockSpec((None, NSA_GROUPS, n_runs, LANES), lambda b: (b, 0, 0, 0)),
        scratch_shapes=[pltpu.VMEM((width // LANES, seq, LANES), F32)],
        compiler_params=pltpu.CompilerParams(dimension_semantics=("parallel",), vmem_limit_bytes=VMEM_LIMIT),
        name="compress_k" if is_key else "compress_v",
    )(ab, pos, w1, w2dup, *extra)


def _online_update(s, v_ones, m_ref, acc_ref, idx):
    m_prev = m_ref[idx]
    m_next = jnp.maximum(m_prev, jnp.max(s, axis=1, keepdims=True))
    alpha = jnp.exp2(m_prev - m_next)
    p = jnp.exp2(s - jnp.concatenate([m_next] * (s.shape[1] // LANES), axis=1))
    m_ref[idx] = m_next
    scale = jnp.concatenate([alpha] * (acc_ref.shape[-1] // LANES), axis=1)
    acc_ref[idx] = acc_ref[idx] * scale + _dot(p.astype(BF16), v_ones)


def _init_softmax_state(m_ref, acc_ref):
    m_ref[...] = jnp.full(m_ref.shape, -jnp.inf, F32)
    acc_ref[...] = jnp.zeros(acc_ref.shape, F32)


def _add_per_head(s, bias, heads):
    rows, cols = bias.shape
    return (s.reshape(heads, rows, cols) + bias[None]).reshape(heads * rows, cols)


def _diff_attn_kernel(q_ref, k_ref, v_ref, cb_ref, lq1_ref, lk1_ref, lq2_ref, lk2_ref, sg_ref, o_ref,
                      m_ref, acc_ref, vones_ref, s_ref, *, lambda_init):
    i = pl.program_id(2)
    t = q_ref.shape[0]
    dv = 2 * HEAD_DIM
    n_heads = q_ref.shape[1] // LANES
    chunk = lambda hd: slice(hd * LANES, (hd + 1) * LANES)

    @pl.when(i == 0)
    def _():
        for hd in range(n_heads):
            vones_ref[hd, :, :dv] = v_ref[:, chunk(hd)]
            vones_ref[hd, :, dv:] = jnp.ones((v_ref.shape[0], vones_ref.shape[2] - dv), BF16)

    lane = lax.broadcasted_iota(jnp.int32, (t, LANES), 1)
    zero = jnp.zeros((t, LANES), q_ref.dtype)
    qc = []
    for hd in range(n_heads):
        q = q_ref[:, chunk(hd)]
        qc += [jnp.where(lane < HEAD_DIM, q, zero), jnp.where(lane >= HEAD_DIM, q, zero)]
    _init_softmax_state(m_ref, acc_ref)

    def scores(j, slot):
        rows = pl.ds(pl.multiple_of(j * t, t), t)
        for hd in range(n_heads):
            k = k_ref[rows, chunk(hd)]
            for c in range(2):
                s_ref[slot, 2 * hd + c] = _dot_nt(qc[2 * hd + c], k)

    def update(j, slot, masked):
        rows = pl.ds(pl.multiple_of(j * t, t), t)
        for hd in range(n_heads):
            v = vones_ref[hd, rows, :]
            for c in range(2):
                s = s_ref[slot, 2 * hd + c]
                if masked:
                    s = s + cb_ref[...]
                _online_update(s, v, m_ref, acc_ref, 2 * hd + c)

    scores(0, 0)

    def body(pp, carry):
        scores(2 * pp + 1, 1)
        update(2 * pp, 0, False)
        scores(2 * pp + 2, 0)
        update(2 * pp + 1, 1, False)
        return carry

    lax.fori_loop(0, i // 2, body, 0)

    @pl.when(i % 2 == 0)
    def _():
        update(i, 0, True)

    @pl.when(i % 2 == 1)
    def _():
        scores(i, 1)
        update(i - 1, 0, False)
        update(i, 1, True)

    lam = (jnp.exp(jnp.sum(lq1_ref[...] * lk1_ref[...], axis=-1, keepdims=True))
           - jnp.exp(jnp.sum(lq2_ref[...] * lk2_ref[...], axis=-1, keepdims=True)) + lambda_init)
    for hd in range(n_heads):
        c0, c1 = 2 * hd, 2 * hd + 1
        o = acc_ref[c0, :, :dv] / acc_ref[c0, :, dv:] - lam * (acc_ref[c1, :, :dv] / acc_ref[c1, :, dv:])
        ms = jnp.mean(o * o, axis=-1, keepdims=True)
        o_ref[:, chunk(hd)] = (o * lax.rsqrt(ms + EPS) * sg_ref[...] * (1.0 - lambda_init)).astype(o_ref.dtype)


def _diff_attention(ab, causal_bias, lams, subln_g, batch, seq, lambda_init, col):
    t = causal_bias.shape[0]
    nq = seq // t
    dv = 2 * HEAD_DIM
    hps = DIFF_HEADS_PER_STEP
    width = hps * LANES
    chains = 2 * hps
    assert dv == LANES and DIFF_HEADS % hps == 0
    small = lambda arr: pl.BlockSpec(arr.shape, lambda b, h, i: (0, 0))
    return pl.pallas_call(
        functools.partial(_diff_attn_kernel, lambda_init=lambda_init),
        out_shape=jax.ShapeDtypeStruct((batch * seq, DIFF_HEADS * dv), BF16),
        grid=(batch, DIFF_HEADS // hps, nq),
        in_specs=[
            pl.BlockSpec((t, width), lambda b, h, i: (b * nq + i, col["dq"] // width + h)),
            pl.BlockSpec((seq, width), lambda b, h, i: (b, col["dk"] // width + h)),
            pl.BlockSpec((seq, width), lambda b, h, i: (b, col["dv"] // width + h)),
            small(causal_bias),
            small(lams[0]), small(lams[1]), small(lams[2]), small(lams[3]), small(subln_g),
        ],
        out_specs=pl.BlockSpec((t, width), lambda b, h, i: (b * nq + i, h)),
        scratch_shapes=[
            pltpu.VMEM((chains, t, LANES), F32),
            pltpu.VMEM((chains, t, 2 * dv), F32),
            pltpu.VMEM((hps, seq, 2 * dv), BF16),
            pltpu.VMEM((2, chains, t, t), F32),
        ],
        compiler_params=pltpu.CompilerParams(
            dimension_semantics=("parallel", "parallel", "arbitrary"), vmem_limit_bytes=VMEM_LIMIT),
        name="diff_attention",
    )(ab, ab, ab, causal_bias, *lams, subln_g)


def _nsa_kernel(q_ref, ksl_ref, vsl_ref, kwn_ref, vwn_ref, kc_ref, vc_ref, gate_ref, ovt_ref, cmpb_ref, cb_ref,
                wb_ref, o_ref, qaug_ref, m_ref, acc_ref, accw_ref, imp_ref, kaug_ref, vsl1_ref, vwn1_ref,
                q4_ref, part_ref, rank_ref):
    i = pl.program_id(2)
    t = q_ref.shape[0]
    seq = ksl_ref.shape[0]
    q0 = i * t
    n_sel = seq // SEL_BLOCK
    assert n_sel <= LANES - HEAD_DIM and imp_ref.shape[0] == LANES
    lane = lax.broadcasted_iota(jnp.int32, (t, LANES), 1)
    low = lane < HEAD_DIM

    @pl.when(i == 0)
    def _():
        row = lax.broadcasted_iota(jnp.int32, (seq, LANES), 0)
        ln = lax.broadcasted_iota(jnp.int32, (seq, LANES), 1)
        kaug_ref[...] = ksl_ref[...] + jnp.where(ln - HEAD_DIM == row // SEL_BLOCK, 1.0, 0.0).astype(BF16)
        ones_hi = jnp.where(ln >= HEAD_DIM, 1.0, 0.0).astype(BF16)
        vsl1_ref[...] = vsl_ref[...] + ones_hi
        vwn1_ref[...] = vwn_ref[...] + ones_hi

    q = q_ref[...].astype(F32)
    heads = []
    for pair in range(NSA_HPG // 2):
        qp = q[:, pair * LANES:(pair + 1) * LANES]
        heads += [qp, pltpu.roll(qp, HEAD_DIM, axis=1)]

    for hh, hq in enumerate(heads):
        q4_ref[hh] = jnp.where(low, hq, 0.0).astype(BF16)
    rows = NSA_HPG * t

    tw = wb_ref.shape[1]
    wk = wb_ref.shape[2]
    assert wk == WINDOW + tw and WINDOW % tw == 0 and t % tw == 0
    for sub in range(t // tw):
        qs = q4_ref[:, sub * tw:(sub + 1) * tw, :].reshape(NSA_HPG * tw, LANES)
        sub_idx = i * (t // tw) + sub
        kstart = pl.multiple_of(jnp.maximum(q0 + sub * tw - WINDOW, 0), tw)
        s = _dot_nt(qs, kwn_ref[pl.ds(kstart, wk), :])
        s = _add_per_head(s, wb_ref[jnp.minimum(sub_idx, WINDOW // tw)], NSA_HPG)
        e = jnp.exp2(s - jnp.max(s, axis=1, keepdims=True))
        acc_w =_dot(e.astype(BF16), vwn1_ref[pl.ds(kstart, wk), :])
        accw_ref[:, sub * tw:(sub + 1) * tw, :] = acc_w.reshape(NSA_HPG, tw, LANES)

    s = _add_per_head(_dot_nt(q4_ref[...].reshape(rows, LANES), kc_ref[...]), cmpb_ref[...], NSA_HPG)
    e = jnp.exp2(s - jnp.max(s, axis=1, keepdims=True))
    p = e / jnp.sum(e, axis=1, keepdims=True)
    o_cmp = _dot(p.astype(BF16), vc_ref[...])

    denom = jnp.ones((t, LANES), F32)
    for hh in range(NSA_HPG):
        denom = jnp.where(lane == HEAD_DIM + 3 * hh + 2, accw_ref[hh], denom)
    coef = gate_ref[...] / denom
    has_cmp = q0 + lax.broadcasted_iota(jnp.int32, (t, 1), 0) >= CMP_BLOCK - 1
    for hh in range(NSA_HPG):
        c0 = HEAD_DIM + 3 * hh
        part_ref[pl.ds(hh * t, t), :] = (jnp.where(has_cmp, coef[:, c0:c0 + 1], 0.0) * o_cmp[hh * t:(hh + 1) * t]
                                         + coef[:, c0 + 2:c0 + 3] * accw_ref[hh])

    psum = p[0:t] + p[t:2 * t] + p[2 * t:3 * t] + p[3 * t:4 * t]
    hi, lo = _split_bf16(psum)
    ovt = ovt_ref[...]
    imp = _dot_nt(ovt, hi) + _dot_nt(ovt, lo)
    jrow = lax.broadcasted_iota(jnp.int32, imp.shape, 0) - SEL_BLOCK
    qcol = q0 + lax.broadcasted_iota(jnp.int32, imp.shape, 1)
    cur = qcol // SEL_BLOCK
    forced = (jrow == 0) | (jrow == cur) | (jrow == cur - 1)
    imp = jnp.where(forced, FORCED_SCORE, imp)
    imp = jnp.where((jrow >= 0) & (jrow * SEL_BLOCK <= qcol), imp, -jnp.inf)
    imp_ref[...] = imp
    imp_b = imp[SEL_BLOCK:, :]
    jb = lax.broadcasted_iota(jnp.int32, imp_b.shape, 0)
    per_tile = t // SEL_BLOCK
    rank_ref[...] = jnp.zeros(rank_ref.shape, F32)

    def count_group(g8):
        rank = rank_ref[...]
        for kk in range(g8 * per_tile, (g8 + 1) * per_tile):
            rk = imp_ref[SEL_BLOCK + kk:SEL_BLOCK + kk + 1, :]
            rank = rank + jnp.where(rk > imp_b, 1.0, jnp.where((rk == imp_b) & (jb > kk), 1.0, 0.0))
        rank_ref[...] = rank

    count_group(0)
    for g8 in range(1, n_sel // per_tile):
        pl.when(i >= g8)(functools.partial(count_group, g8))
    chosen = (rank_ref[...] < float(min(SEL_TOP, n_sel))) & (imp_b > -jnp.inf)
    bias_t = jnp.concatenate([jnp.zeros((SEL_BLOCK, t), F32), jnp.where(chosen, 0.0, NEG_BLOCK)], axis=0)
    bias = bias_t.T
    for hh, hq in enumerate(heads):
        qaug_ref[hh] = jnp.where(low, hq, bias).astype(BF16)

    _init_softmax_state(m_ref, acc_ref)

    def slc_step(j, masked):
        start = pl.multiple_of(j * t, t)
        k = kaug_ref[pl.ds(start, t), :]
        v = vsl1_ref[pl.ds(start, t), :]
        chains = SLC_CHAINS if masked else 1
        per_chain = NSA_HPG // chains
        for ch in range(chains):
            qs = qaug_ref[ch * per_chain:(ch + 1) * per_chain].reshape(per_chain * t, LANES)
            s = _dot_nt(qs, k)
            if masked:
                s = _add_per_head(s, cb_ref[...], per_chain)
            _online_update(s, v, m_ref, acc_ref, (0, pl.ds(ch * per_chain * t, per_chain * t)))

    def slc_body(j, carry):
        slc_step(j, False)
        return carry

    lax.fori_loop(0, i, slc_body, 0)
    slc_step(i, True)

    slc_acc = lambda hh: acc_ref[0, pl.ds(hh * t, t), :]
    denom = jnp.ones((t, LANES), F32)
    for hh in range(NSA_HPG):
        denom = jnp.where(lane == HEAD_DIM + 3 * hh + 1, slc_acc(hh), denom)
    coef = gate_ref[...] / denom
    outs = []
    for hh in range(NSA_HPG):
        c1 = HEAD_DIM + 3 * hh + 1
        outs.append(part_ref[pl.ds(hh * t, t), :] + coef[:, c1:c1 + 1] * slc_acc(hh))
    for pair in range(NSA_HPG // 2):
        o_ref[:, pair * LANES:(pair + 1) * LANES] = jnp.where(
            low, outs[2 * pair], pltpu.roll(outs[2 * pair + 1], HEAD_DIM, axis=1)).astype(o_ref.dtype)


def _nsa_attention(ab, gates, kcmp, vcmp, ovt, cmp_bias, causal_bias, window_bias, batch, seq, col, col_gate):
    t = causal_bias.shape[0]
    nq = seq // t
    gw = NSA_HPG * HEAD_DIM
    rows = NSA_HPG * t
    n_cmp = kcmp.shape[2]
    full = lambda arr: pl.BlockSpec(arr.shape, lambda b, g, i: (0,) * arr.ndim)
    kv = lambda name: pl.BlockSpec((seq, LANES), lambda b, g, i: (b, col[name] // LANES + g))
    return pl.pallas_call(
        _nsa_kernel,
        out_shape=jax.ShapeDtypeStruct((batch * seq, NSA_HEADS * HEAD_DIM), BF16),
        grid=(batch, NSA_GROUPS, nq),
        in_specs=[
            pl.BlockSpec((t, gw), lambda b, g, i: (b * nq + i, col["nq"] // gw + g)),
            kv("ksl"), kv("vsl"), kv("kwn"), kv("vwn"),
            pl.BlockSpec((None, None, n_cmp, LANES), lambda b, g, i: (b, g, 0, 0)),
            pl.BlockSpec((None, None, n_cmp, LANES), lambda b, g, i: (b, g, 0, 0)),
            pl.BlockSpec((t, LANES), lambda b, g, i: (b * nq + i, col_gate // LANES + g)),
            full(ovt),
            pl.BlockSpec((t, n_cmp), lambda b, g, i: (i, 0)),
            full(causal_bias),
            full(window_bias),
        ],
        out_specs=pl.BlockSpec((t, gw), lambda b, g, i: (b * nq + i, g)),
        scratch_shapes=[
            pltpu.VMEM((NSA_HPG, t, LANES), BF16),
            pltpu.VMEM((1, rows, LANES), F32),
            pltpu.VMEM((1, rows, LANES), F32),
            pltpu.VMEM((NSA_HPG, t, LANES), F32),
            pltpu.VMEM((LANES, t), F32),
            pltpu.VMEM((seq, LANES), BF16),
            pltpu.VMEM((seq, LANES), BF16),
            pltpu.VMEM((seq, LANES), BF16),
            pltpu.VMEM((NSA_HPG, t, LANES), BF16),
            pltpu.VMEM((rows, LANES), F32),
            pltpu.VMEM((SEL_BLOCK, t), F32),
        ],
        compiler_params=pltpu.CompilerParams(
            dimension_semantics=("parallel", "parallel", "arbitrary"), vmem_limit_bytes=VMEM_LIMIT),
        name="nsa_attention",
    )(ab, ab, ab, ab, ab, kcmp, vcmp, gates, ovt, cmp_bias, causal_bias, window_bias)


def _merge_kernel(yd_ref, yn_ref, g0_ref, g1_ref, x_ref, wd_ref, wn_ref, wo_ref, lg_ref, x1_ref, h_ref):
    mixed = g0_ref[...] * _dot(yd_ref[...], wd_ref[...]) + g1_ref[...] * _dot(yn_ref[...], wn_ref[...])
    x1 = x_ref[...] + _dot(mixed.astype(BF16), wo_ref[...])
    x1_ref[...] = x1
    ms = jnp.mean(x1 * x1, axis=-1, keepdims=True)
    h_ref[...] = (x1 * lax.rsqrt(ms + EPS) * lg_ref[...]).astype(h_ref.dtype)


def _merge(yd, yn, gates, x2d, wd, wn, wo, lg):
    n, d = x2d.shape
    tm = min(MERGE_TM, n)
    tok = lambda c: pl.BlockSpec((tm, d), lambda i: (i, c))
    full = lambda arr: pl.BlockSpec(arr.shape, lambda i: (0, 0))
    return pl.pallas_call(
        _merge_kernel,
        out_shape=(jax.ShapeDtypeStruct((n, d), F32), jax.ShapeDtypeStruct((n, d), BF16)),
        grid=(n // tm,),
        in_specs=[tok(0), tok(0), tok(0), tok(1), tok(0), full(wd), full(wn), full(wo), full(lg)],
        out_specs=(tok(0), tok(0)),
        compiler_params=pltpu.CompilerParams(
            dimension_semantics=("parallel",), vmem_limit_bytes=VMEM_LIMIT),
        name="merge_outproj",
    )(yd, yn, gates, gates, x2d, wd, wn, wo, lg)


def _mlp_kernel(h_ref, x1_ref, wu_ref, wd_ref, o_ref, acc_ref):
    f = pl.program_id(1)

    @pl.when(f == 0)
    def _():
        acc_ref[...] = x1_ref[...]

    up = jnp.maximum(_dot(h_ref[...], wu_ref[...]), 0.0)
    acc_ref[...] += _dot((up * up).astype(BF16), wd_ref[...])

    @pl.when(f == pl.num_programs(1) - 1)
    def _():
        o_ref[...] = acc_ref[...]


def _mlp(h, x1, wu, wd):
    n, d = x1.shape
    dff = wu.shape[1]
    tm, tf = min(MLP_TM, n), MLP_TF
    return pl.pallas_call(
        _mlp_kernel,
        out_shape=jax.ShapeDtypeStruct((n, d), F32),
        grid=(n // tm, dff // tf),
        in_specs=[
            pl.BlockSpec((tm, d), lambda i, f: (i, 0)),
            pl.BlockSpec((tm, d), lambda i, f: (i, 0)),
            pl.BlockSpec((d, tf), lambda i, f: (0, f)),
            pl.BlockSpec((tf, d), lambda i, f: (f, 0)),
        ],
        out_specs=pl.BlockSpec((tm, d), lambda i, f: (i, 0)),
        scratch_shapes=[pltpu.VMEM((tm, d), F32)],
        compiler_params=pltpu.CompilerParams(
            dimension_semantics=("parallel", "arbitrary"), vmem_limit_bytes=VMEM_LIMIT),
        name="mlp",
    )(h, x1, wu, wd)


def _rope_tables(pos):
    inv_freq = ROPE_THETA ** (-jnp.arange(HALF, dtype=F32) / HALF)
    ang = pos.astype(F32)[:, None] * inv_freq[None, :]
    c, s = jnp.cos(ang), jnp.sin(ang)
    return jnp.concatenate([c, c, c, c], axis=-1), jnp.concatenate([-s, s, -s, s], axis=-1)


def _pad_groups(w, width, offset=0):
    d = w.shape[0]
    w = w.reshape(d, NSA_GROUPS, width)
    return jnp.pad(w, ((0, 0), (0, 0), (offset, LANES - width - offset))).reshape(d, NSA_GROUPS * LANES)


def _additive_mask(valid):
    return jnp.asarray(np.where(valid, 0.0, NEG_TOKEN), F32)


def _layer(x2d, batch, seq, layer, ln_mix_g, w_in, diff_q_norm_g, diff_k_norm_g, diff_lambda_q1,
           diff_lambda_k1, diff_lambda_q2, diff_lambda_k2, diff_subln_g, nsa_q_norm_g, nsa_k_norm_g,
           cmp_pos_k, cmp_pos_v, cmp_k_w1, cmp_k_w2, cmp_v_w1, cmp_v_w2, w_proj_diff, w_proj_nsa,
           w_out, ln_mlp_g, w_mlp_up, w_mlp_down):
    d = x2d.shape[1]
    diff_qk = DIFF_HEADS * 2 * HEAD_DIM
    diff_v = DIFF_HEADS * 2 * HEAD_DIM
    nsa_q = NSA_HEADS * HEAD_DIM
    nsa_kv = NSA_GROUPS * HEAD_DIM
    splits = np.cumsum([diff_qk, diff_qk, diff_v, nsa_q] + [nsa_kv] * 6 + [NSA_HEADS * 3, 2 * d])[:-1]
    (w_dq, w_dk, w_dv, w_nq, w_kc, w_vc, w_ksl, w_vsl, w_kwn, w_vwn, w_ng, w_mg) = jnp.split(
        w_in, [int(c) for c in splits], axis=1)
    scale = HEAD_DIM ** -0.5 * math.log2(math.e)
    pad_kv = lambda w: _pad_groups(w, HEAD_DIM)

    fam_rope = [("dq", w_dq), ("dk", w_dk), ("nq", w_nq), ("ksl", pad_kv(w_ksl)), ("kwn", pad_kv(w_kwn))]
    fam_plain = [("dv", w_dv), ("vsl", pad_kv(w_vsl)), ("vwn", pad_kv(w_vwn)), ("kc", w_kc), ("vc", w_vc)]
    fam_gate = [("mg", w_mg), ("ng", _pad_groups(w_ng, NSA_HPG * 3, HEAD_DIM))]
    col, blocks, tiles = {}, [], []
    for fam in (fam_rope, fam_plain, fam_gate):
        start = sum(b.shape[1] for b in blocks)
        for name, w in fam:
            col[name] = sum(b.shape[1] for b in blocks)
            blocks.append(w)
        width = sum(b.shape[1] for b in blocks) - start
        pad = -width % PROJ_TN
        if pad:
            blocks.append(jnp.zeros((d, pad), w_in.dtype))
        tiles.append((width + pad) // PROJ_TN)
    w_all = jnp.concatenate(blocks, axis=1).astype(BF16)
    gate_base = (tiles[0] + tiles[1]) * PROJ_TN

    tile_g = lambda g, reps: jnp.tile(g.astype(F32), reps)
    gain = jnp.concatenate([
        tile_g(diff_q_norm_g, 2 * DIFF_HEADS) * scale,
        tile_g(diff_k_norm_g, 2 * DIFF_HEADS),
        tile_g(nsa_q_norm_g, NSA_HEADS) * scale,
        tile_g(nsa_k_norm_g[1], 2 * NSA_GROUPS),
        tile_g(nsa_k_norm_g[2], 2 * NSA_GROUPS),
    ])[None, :]
    assert gain.shape[1] == tiles[0] * PROJ_TN
    cos, sin = _rope_tables(jnp.arange(seq))
    mean_heads = np.kron(np.eye(MXU_WIDTH // HEAD_DIM), np.full((HEAD_DIM, HEAD_DIM), 1.0 / HEAD_DIM))
    ab, gates = _inproj(x2d, ln_mix_g.astype(F32)[None, :], w_all, gain, cos, sin,
                        jnp.asarray(mean_heads, BF16), seq, *tiles)

    n_runs = seq // CMP_STRIDE
    dup = lambda w: jnp.concatenate([w, w], axis=1).astype(BF16)
    reps = LANES // HEAD_DIM
    per_group = lambda pos: jnp.tile(pos.astype(F32), (1, reps))
    rows_per_group = lambda w1: jnp.tile(
        w1.reshape(CMP_BLOCK, HEAD_DIM, w1.shape[1]), (1, reps, 1)).astype(BF16)
    cmp_center = jnp.arange(n_runs) * CMP_STRIDE + (CMP_BLOCK - 1) / 2.0
    cos_c, sin_c = _rope_tables(cmp_center)
    gain_c = tile_g(nsa_k_norm_g[0], 2)[None, :]
    kcmp = _compress(ab, col["kc"], seq, per_group(cmp_pos_k), rows_per_group(cmp_k_w1), dup(cmp_k_w2),
                     (gain_c, cos_c, sin_c), True)
    vcmp = _compress(ab, col["vc"], seq, per_group(cmp_pos_v), rows_per_group(cmp_v_w1), dup(cmp_v_w2),
                     (), False)

    lambda_init = 0.8 - 0.6 * math.exp(-0.3 * layer)
    lams = [v.astype(F32)[None, :] for v in (diff_lambda_q1, diff_lambda_k1, diff_lambda_q2, diff_lambda_k2)]
    t_att = min(ATT_T, seq)
    tri = np.arange(t_att)
    y_diff = _diff_attention(ab, _additive_mask(tri[None, :] <= tri[:, None]), lams,
                             diff_subln_g.astype(F32)[None, :], batch, seq, lambda_init, col)

    n_sel = seq // SEL_BLOCK
    cmp_start = np.arange(n_runs) * CMP_STRIDE
    sel_start = np.arange(n_sel) * SEL_BLOCK
    overlap = ((cmp_start[:, None] < sel_start[None, :] + SEL_BLOCK)
               & (cmp_start[:, None] + CMP_BLOCK - 1 >= sel_start[None, :]))
    ovt = np.zeros((LANES, n_runs))
    ovt[HEAD_DIM:HEAD_DIM + n_sel] = overlap.T
    cmp_bias = _additive_mask(cmp_start[None, :] + CMP_BLOCK - 1 <= np.arange(seq)[:, None])
    t_nsa = min(NSA_T, seq)
    tri = np.arange(t_nsa)
    t_win = min(WIN_T, t_nsa)
    r = np.arange(t_win)[:, None]
    c = np.arange(WINDOW + t_win)[None, :]
    window_bias = jnp.stack([_additive_mask((c <= r + off) & (r + off - c < WINDOW))
                             for off in range(0, WINDOW + 1, t_win)])
    y_nsa = _nsa_attention(ab, gates, kcmp, vcmp, jnp.asarray(ovt, BF16), cmp_bias,
                           _additive_mask(tri[None, :] <= tri[:, None]), window_bias, batch, seq, col,
                           col["ng"] - gate_base)

    assert col["mg"] == gate_base
    x1, h2 = _merge(y_diff, y_nsa, gates, x2d, w_proj_diff.astype(BF16), w_proj_nsa.astype(BF16),
                    w_out.astype(BF16), ln_mlp_g.astype(F32)[None, :])
    return _mlp(h2, x1, w_mlp_up.astype(BF16), w_mlp_down.astype(BF16))


def kernel(x, ln_mix_g, w_in, diff_q_norm_g, diff_k_norm_g, diff_lambda_q1, diff_lambda_k1, diff_lambda_q2, diff_lambda_k2, diff_subln_g, nsa_q_norm_g, nsa_k_norm_g, cmp_pos_k, cmp_pos_v, cmp_k_w1, cmp_k_w2, cmp_v_w1, cmp_v_w2, w_proj_diff, w_proj_nsa, w_out, ln_mlp_g, w_mlp_up, w_mlp_down):
    batch, seq, d = x.shape
    params = (ln_mix_g, w_in, diff_q_norm_g, diff_k_norm_g, diff_lambda_q1, diff_lambda_k1, diff_lambda_q2,
              diff_lambda_k2, diff_subln_g, nsa_q_norm_g, nsa_k_norm_g, cmp_pos_k, cmp_pos_v, cmp_k_w1,
              cmp_k_w2, cmp_v_w1, cmp_v_w2, w_proj_diff, w_proj_nsa, w_out, ln_mlp_g, w_mlp_up, w_mlp_down)
    x2d = x.reshape(batch * seq, d)
    for layer in range(ln_mix_g.shape[0]):
        x2d = _layer(x2d, batch, seq, layer, *[prm[layer] for prm in params])
    return x2d.reshape(batch, seq, d)
```
